```python
import jax, jax.numpy as jnp
from jax import lax
import numpy as np

D_MODEL = 1024
BATCH = 2
SEQ = 8192
DEPTH = 4
DEC_BATCH = 16
DEC_SEQ = 32
PAST_LEN = 1024

CHUNK = 64
N_EVEN = (DEPTH + 1) // 2
N_ODD = DEPTH // 2
EPS = 1e-6

POOL_WINDOWS = (2, 4, 8, 16)
POOL_GROUPS = len(POOL_WINDOWS)
POOL_HIST = max(POOL_WINDOWS) - 1
W_A = D_MODEL
POOL_GDIM = W_A // POOL_GROUPS

W_B = D_MODEL
SGU_HEADS = 4
SGU_HDIM = W_B // SGU_HEADS
SGU_CHUNK = 128

EVEN_MIX = W_A + W_B
EVEN_IN = W_A + 2 * W_B + EVEN_MIX

N_HEADS_C = D_MODEL // 128
QK_NOPE = 128
QK_ROPE = 64
V_DIM = 128
KV_LORA = D_MODEL // 4
Q_LORA = 3 * D_MODEL // 8
C_MIX = N_HEADS_C * V_DIM
ODD_IN = Q_LORA + KV_LORA + QK_ROPE + C_MIX
ROPE_BASE = 10000.0
Q_BLOCK = 128
ATTN_SCALE = (QK_NOPE + QK_ROPE) ** -0.5

kernel_name = "hybrid_pool_sgu_mla_streaming_step"


def rms_norm(x, g):
    x32 = x.astype(jnp.float32)
    y = x32 * lax.rsqrt(jnp.mean(x32 * x32, axis=-1, keepdims=True) + EPS)
    return (y * g.astype(jnp.float32)).astype(x.dtype)


def layer_norm(x, g, b):
    x32 = x.astype(jnp.float32)
    mu = jnp.mean(x32, axis=-1, keepdims=True)
    xc = x32 - mu
    var = jnp.mean(xc * xc, axis=-1, keepdims=True)
    y = xc * lax.rsqrt(var + EPS) * g.astype(jnp.float32) + b.astype(jnp.float32)
    return y.astype(x.dtype)


def rope_tables(pos):
    half = QK_ROPE // 2
    freqs = ROPE_BASE ** (-jnp.arange(half, dtype=jnp.float32) / half)
    ang = pos.astype(jnp.float32)[:, None] * freqs[None, :]
    return jnp.cos(ang), jnp.sin(ang)


def apply_rope(x, cos, sin):
    half = QK_ROPE // 2
    x32 = x.astype(jnp.float32)
    x1, x2 = x32[..., :half], x32[..., half:]
    out = jnp.concatenate([x1 * cos - x2 * sin, x2 * cos + x1 * sin], axis=-1)
    return out.astype(x.dtype)


def pool_mixer(a, hist, pos0, w_pool, scale):
    B, T, W = a.shape
    P = hist.shape[1]
    full = jnp.concatenate([hist, a], axis=1).astype(jnp.float32)
    cs = jnp.concatenate([jnp.zeros((B, 1, W), jnp.float32), jnp.cumsum(full, axis=1)], axis=1)
    pos = pos0 + jnp.arange(T)
    means = []
    for g, w in enumerate(POOL_WINDOWS):
        sl = slice(g * POOL_GDIM, (g + 1) * POOL_GDIM)
        wsum = cs[:, P + 1:P + 1 + T, sl] - cs[:, P + 1 - w:P + 1 - w + T, sl]
        cnt = jnp.minimum(pos + 1, w).astype(jnp.float32)[None, :, None]
        means.append(wsum / cnt)
    d = (jnp.concatenate(means, axis=-1) - a.astype(jnp.float32)).astype(a.dtype)
    d = d.reshape(B, T, POOL_GROUPS, POOL_GDIM)
    y = jnp.einsum('btgi,gio->btgo', d, w_pool).reshape(B, T, W)
    return y * scale


def sgu_mixer(uv, ln_g, ln_b, w_s, b_s):
    uv = jax.nn.gelu(uv, approximate=False)
    u, v = uv[..., :W_B], uv[..., W_B:]
    v = layer_norm(v, ln_g, ln_b)
    B, T, W = v.shape
    L = min(T, SGU_CHUNK)
    idx = jnp.arange(L)
    mask = (idx[None, :] // CHUNK) <= (idx[:, None] // CHUNK)
    ws = jnp.where(mask[None], w_s[:, :L, :L], jnp.zeros((), w_s.dtype))
    vc = v.reshape(B, T // L, L, SGU_HEADS, SGU_HDIM)
    mixed = jnp.einsum('gij,bcjgd->bcigd', ws, vc)
    mixed = mixed + jnp.transpose(b_s[:, :L])[None, None, :, :, None]
    return u * mixed.reshape(B, T, W), v


def even_layer(x, pool_hist, pos0, g_pre, g_post, w_in, w_pool, pool_scale,
               ln_g, ln_b, w_s, b_s, w_out):
    h = rms_norm(x, g_pre)
    z = jnp.einsum('btd,de->bte', h, w_in)
    a = z[..., :W_A]
    uv = z[..., W_A:W_A + 2 * W_B]
    gate = z[..., W_A + 2 * W_B:]
    y_a = pool_mixer(a, pool_hist, pos0, w_pool, pool_scale)
    y_b, v = sgu_mixer(uv, ln_g, ln_b, w_s, b_s)
    mix = jnp.concatenate([y_a, y_b], axis=-1) * jax.nn.silu(gate)
    y = jnp.einsum('bte,ed->btd', mix, w_out)
    x = x + rms_norm(y, g_post)
    new_hist = jnp.concatenate([pool_hist, a], axis=1)[:, -POOL_HIST:]
    return x, new_hist, v


def mla_project(x, pos, g_pre, w_in, q_norm, kv_norm, w_q_up, w_kv_up):
    h = rms_norm(x, g_pre)
    z = jnp.einsum('btd,de->bte', h, w_in)
    q_c = z[..., :Q_LORA]
    kv_c = z[..., Q_LORA:Q_LORA + KV_LORA]
    k_r = z[..., Q_LORA + KV_LORA:Q_LORA + KV_LORA + QK_ROPE]
    gate = z[..., Q_LORA + KV_LORA + QK_ROPE:]
    q = jnp.einsum('btc,chd->bthd', rms_norm(q_c, q_norm), w_q_up)
    cos, sin = rope_tables(pos)
    q_rope = apply_rope(q[..., QK_NOPE:], cos[:, None, :], sin[:, None, :])
    k_rope = apply_rope(k_r, cos, sin)
    ckv = rms_norm(kv_c, kv_norm)
    q_abs = jnp.einsum('bthn,chn->bthc', q[..., :QK_NOPE], w_kv_up[..., :QK_NOPE])
    return q_abs, q_rope, ckv, k_rope, gate


def mla_attend(q_abs, q_rope, ckv, krope, q_pos, k_pos):
    s = (jnp.einsum('bqhc,bkc->bhqk', q_abs, ckv)
         + jnp.einsum('bqhr,bkr->bhqk', q_rope, krope)).astype(jnp.float32) * ATTN_SCALE
    mask = (k_pos[None, :] // CHUNK) <= (q_pos[:, None] // CHUNK)
    s = jnp.where(mask[None, None], s, -jnp.inf)
    p = jax.nn.softmax(s, axis=-1).astype(ckv.dtype)
    return jnp.einsum('bhqk,bkc->bqhc', p, ckv)


def mla_attend_blocked(q_abs, q_rope, ckv, krope):
    B, T, H, C = q_abs.shape
    nb = T // Q_BLOCK
    k_pos = jnp.arange(T)
    qa = jnp.swapaxes(q_abs.reshape(B, nb, Q_BLOCK, H, C), 0, 1)
    qr = jnp.swapaxes(q_rope.reshape(B, nb, Q_BLOCK, H, QK_ROPE), 0, 1)
    starts = jnp.arange(nb) * Q_BLOCK

    def one_block(args):
        qa_b, qr_b, s0 = args
        return mla_attend(qa_b, qr_b, ckv, krope, s0 + jnp.arange(Q_BLOCK), k_pos)

    o = lax.map(one_block, (qa, qr, starts))
    return jnp.swapaxes(o, 0, 1).reshape(B, T, H, C)


def mla_output(o_lat, gate, w_kv_up, w_o):
    B, T = o_lat.shape[:2]
    o = jnp.einsum('bthc,chv->bthv', o_lat, w_kv_up[..., QK_NOPE:]).reshape(B, T, C_MIX)
    return jnp.einsum('bte,ed->btd', o * jax.nn.silu(gate), w_o)


def setup_inputs(seed: int = 0) -> dict:
    key = jax.random.key(seed)
    ks = jax.random.split(key, 24)
    f32 = jnp.float32

    def nrm(k, shape, scale):
        return scale * jax.random.normal(k, shape, f32)

    return {
        "x_prompt": nrm(ks[0], (BATCH, SEQ, D_MODEL), 1.0),
        "x_sample": nrm(ks[1], (DEC_BATCH, DEC_SEQ, D_MODEL), 1.0),
        "cache_pool": nrm(ks[2], (N_EVEN, DEC_BATCH, POOL_HIST, W_A), 1.0),
        "cache_ckv": nrm(ks[3], (N_ODD, DEC_BATCH, PAST_LEN, KV_LORA), 1.0),
        "cache_krope": nrm(ks[4], (N_ODD, DEC_BATCH, PAST_LEN, QK_ROPE), 1.0),
        "norm_pre": 1.0 + nrm(ks[5], (DEPTH, D_MODEL), 0.05),
        "norm_post": 1.0 + nrm(ks[6], (DEPTH, D_MODEL), 0.05),
        "w_in_even": nrm(ks[7], (N_EVEN, D_MODEL, EVEN_IN), D_MODEL ** -0.5),
        "w_pool": nrm(ks[8], (N_EVEN, POOL_GROUPS, POOL_GDIM, POOL_GDIM), POOL_GDIM ** -0.5),
        "pool_scale": 1.0 + nrm(ks[9], (N_EVEN, W_A), 0.1),
        "sgu_ln_g": 1.0 + nrm(ks[10], (N_EVEN, W_B), 0.05),
        "sgu_ln_b": nrm(ks[11], (N_EVEN, W_B), 0.02),
        "w_spatial": nrm(ks[12], (N_EVEN, SGU_HEADS, SGU_CHUNK, SGU_CHUNK), 0.5 * SGU_CHUNK ** -0.5),
        "b_spatial": 1.0 + nrm(ks[13], (N_EVEN, SGU_HEADS, SGU_CHUNK), 0.02),
        "w_out_even": nrm(ks[14], (N_EVEN, EVEN_MIX, D_MODEL), EVEN_MIX ** -0.5),
        "w_in_odd": nrm(ks[15], (N_ODD, D_MODEL, ODD_IN), D_MODEL ** -0.5),
        "q_norm": 1.0 + nrm(ks[16], (N_ODD, Q_LORA), 0.05),
        "kv_norm": 1.0 + nrm(ks[17], (N_ODD, KV_LORA), 0.05),
        "w_q_up": nrm(ks[18], (N_ODD, Q_LORA, N_HEADS_C, QK_NOPE + QK_ROPE), Q_LORA ** -0.5),
        "w_kv_up": nrm(ks[19], (N_ODD, KV_LORA, N_HEADS_C, QK_NOPE + V_DIM), KV_LORA ** -0.5),
        "w_o": nrm(ks[20], (N_ODD, C_MIX, D_MODEL), C_MIX ** -0.5),
    }


def reference(x_prompt, x_sample, cache_pool, cache_ckv, cache_krope, norm_pre, norm_post,
              w_in_even, w_pool, pool_scale, sgu_ln_g, sgu_ln_b, w_spatial, b_spatial, w_out_even,
              w_in_odd, q_norm, kv_norm, w_q_up, w_kv_up, w_o):
    past = cache_ckv.shape[2]
    T = x_prompt.shape[1]
    S = x_sample.shape[1]
    pos_p = jnp.arange(T)
    pos_s = past + jnp.arange(S)
    k_pos_s = jnp.arange(past + S)
    xp, xs = x_prompt, x_sample
    pool_p, pool_s, sgu_s = [], [], []
    ckv_p, kr_p, ckv_s, kr_s = [], [], [], []
    for l in range(DEPTH):
        if l % 2 == 0:
            e = l // 2
            ew = (norm_pre[l], norm_post[l], w_in_even[e], w_pool[e], pool_scale[e],
                  sgu_ln_g[e], sgu_ln_b[e], w_spatial[e], b_spatial[e], w_out_even[e])
            zero_hist = jnp.zeros((xp.shape[0], POOL_HIST, W_A), xp.dtype)
            xp, hp, _ = even_layer(xp, zero_hist, 0, *ew)
            xs, hs, vs = even_layer(xs, cache_pool[e], past, *ew)
            pool_p.append(hp)
            pool_s.append(hs)
            sgu_s.append(vs)
        else:
            o = l // 2
            qa, qr, ckv, kr, g = mla_project(xp, pos_p, norm_pre[l], w_in_odd[o], q_norm[o],
                                             kv_norm[o], w_q_up[o], w_kv_up[o])
            ol = mla_attend_blocked(qa, qr, ckv, kr)
            xp = xp + rms_norm(mla_output(ol, g, w_kv_up[o], w_o[o]), norm_post[l])
            ckv_p.append(ckv)
            kr_p.append(kr)
            qa, qr, ckv, kr, g = mla_project(xs, pos_s, norm_pre[l], w_in_odd[o], q_norm[o],
                                             kv_norm[o], w_q_up[o], w_kv_up[o])
            ckv_all = jnp.concatenate([cache_ckv[o], ckv], axis=1)
            kr_all = jnp.concatenate([cache_krope[o], kr], axis=1)
            ol = mla_attend(qa, qr, ckv_all, kr_all, pos_s, k_pos_s)
            xs = xs + rms_norm(mla_output(ol, g, w_kv_up[o], w_o[o]), norm_post[l])
            ckv_s.append(ckv)
            kr_s.append(kr)
    return (xp, xs, jnp.stack(pool_p), jnp.stack(pool_s), jnp.stack(sgu_s),
            jnp.stack(ckv_p), jnp.stack(kr_p), jnp.stack(ckv_s), jnp.stack(kr_s))
```

```python
import functools

import jax
import jax.numpy as jnp
from jax import lax
from jax.experimental import pallas as pl
from jax.experimental.pallas import tpu as pltpu

D_MODEL = 1024
CHUNK = 64
EPS = 1e-6
POOL_WINDOWS = (2, 4, 8, 16)
POOL_HIST = max(POOL_WINDOWS) - 1
W_A = D_MODEL
POOL_GDIM = W_A // len(POOL_WINDOWS)
W_B = D_MODEL
SGU_HEADS = 4
SGU_HDIM = W_B // SGU_HEADS
SGU_CHUNK = 128
EVEN_MIX = W_A + W_B
EVEN_IN = W_A + 2 * W_B + EVEN_MIX
N_HEADS = D_MODEL // 128
QK_NOPE = 128
QK_ROPE = 64
V_DIM = 128
KV_LORA = D_MODEL // 4
Q_LORA = 3 * D_MODEL // 8
C_MIX = N_HEADS * V_DIM
ROPE_BASE = 10000.0
ATTN_SCALE = (QK_NOPE + QK_ROPE) ** -0.5
LOG2E = 1.4426950408889634

LANES = 128
HIST_ROWS = 16
VMEM_LIMIT_BYTES = 56 * 1024 * 1024

ODD_Q0, ODD_KV0, ODD_G0, ODD_KR0 = 0, Q_LORA, Q_LORA + KV_LORA, Q_LORA + KV_LORA + C_MIX
ODD_COLS = ODD_KR0 + 2 * QK_ROPE
QUP_R0 = N_HEADS * QK_NOPE
QUP_S0 = QUP_R0 + N_HEADS * QK_ROPE
QUP_COLS = QUP_S0 + N_HEADS * QK_ROPE

F32 = jnp.float32
BF16 = jnp.bfloat16


def _dot(a, b):
    return jnp.dot(a, b, preferred_element_type=F32)


def _dot_nt(a, b):
    return lax.dot_general(a, b, (((1,), (1,)), ((), ())), preferred_element_type=F32)


def _rms(x, g):
    return x * lax.rsqrt(jnp.mean(x * x, axis=-1, keepdims=True) + EPS) * g


def _silu(x):
    return x * (1.0 / (1.0 + jnp.exp(-x)))


def _gelu(x):
    return 0.5 * x * (1.0 + lax.erf(x * (2.0 ** -0.5)))


def _even_body(x_ref, hist_ref, gpre_ref, gpost_ref, win_ref, wpool_ref, pscale_ref, lng_ref, lnb_ref,
               ws_ref, bst_ref, wout_ref, xo_ref, histo_ref, *rest, nb, tt, seg, pos0, emit_v):
    if emit_v:
        vo_ref, aext_ref, mix_ref = rest
    else:
        aext_ref, mix_ref = rest
    t = pl.program_id(1)
    nt = pl.num_programs(1)
    m = nb * tt

    @pl.when(t == 0)
    def _():
        aext_ref[:, 0:HIST_ROWS, :] = hist_ref[...]

    x = x_ref[...].reshape(m, D_MODEL)
    h = _rms(x, gpre_ref[...]).astype(BF16)

    a = _dot(h, win_ref[:, 0:W_A])
    aext_ref[:, HIST_ROWS:HIST_ROWS + tt, :] = a.reshape(nb, tt, W_A)
    pos = pos0 + t * tt + lax.broadcasted_iota(jnp.int32, (tt, 1), 0)
    for g, w in enumerate(POOL_WINDOWS):
        cs = slice(g * POOL_GDIM, (g + 1) * POOL_GDIM)
        inv_cnt = 1.0 / jnp.minimum(pos + 1, w).astype(F32)
        for s in range(nb):
            cur = aext_ref[s, HIST_ROWS:HIST_ROWS + tt, cs]
            wsum = cur
            for k in range(1, w):
                wsum = wsum + aext_ref[s, HIST_ROWS - k:HIST_ROWS - k + tt, cs]
            mix_ref[s * tt:(s + 1) * tt, cs] = wsum * inv_cnt - cur
        y_a = _dot(mix_ref[:, cs].astype(BF16), wpool_ref[g]) * pscale_ref[:, cs]
        mix_ref[:, cs] = y_a

    uv = _gelu(_dot(h, win_ref[:, W_A:W_A + 2 * W_B]))
    u = uv[:, :W_B]
    v = uv[:, W_B:]
    vc = v - jnp.mean(v, axis=-1, keepdims=True)
    vn = vc * lax.rsqrt(jnp.mean(vc * vc, axis=-1, keepdims=True) + EPS) * lng_ref[...] + lnb_ref[...]
    if emit_v:
        vo_ref[...] = vn.reshape(nb, tt, W_B)
    vb = vn.astype(BF16)
    ii = lax.broadcasted_iota(jnp.int32, (seg, seg), 0)
    jj = lax.broadcasted_iota(jnp.int32, (seg, seg), 1)
    causal = (jj // CHUNK) <= (ii // CHUNK)
    for g in range(SGU_HEADS):
        hs = slice(g * SGU_HDIM, (g + 1) * SGU_HDIM)
        wsg = jnp.where(causal, ws_ref[g, 0:seg, 0:seg], 0.0).astype(BF16)
        bcol = bst_ref[0:seg, g:g + 1]
        for c in range(m // seg):
            rs = slice(c * seg, (c + 1) * seg)
            mixed = _dot(wsg, vb[rs, hs]) + bcol
            mix_ref[rs, W_A + g * SGU_HDIM:W_A + (g + 1) * SGU_HDIM] = u[rs, hs] * mixed

    gate = _dot(h, win_ref[:, W_A + 2 * W_B:])
    mixv = (mix_ref[...] * _silu(gate)).astype(BF16)
    y = _dot(mixv, wout_ref[...])
    xo_ref[...] = (x + _rms(y, gpost_ref[...])).reshape(nb, tt, D_MODEL)

    tail = aext_ref[:, tt:tt + HIST_ROWS, :]

    @pl.when(t == nt - 1)
    def _():
        histo_ref[...] = tail

    aext_ref[:, 0:HIST_ROWS, :] = tail


def _resident(block_shape, index_map):
    return pl.BlockSpec(block_shape, index_map, pipeline_mode=pl.Buffered(1))


def _even_layer(x, hist, lyr, e, p, *, nb, tt, pos0, emit_v, name):
    bx, tx, _ = x.shape
    seg = min(tx, SGU_CHUNK)
    assert bx % nb == 0 and tx % tt == 0 and tt % seg == 0 and tt >= HIST_ROWS and tt % 8 == 0
    grid = (bx // nb, tx // tt)
    row = lambda b, t: (b, t, 0)
    first = lambda b, t: (b, 0, 0)
    in_specs = [
        pl.BlockSpec((nb, tt, D_MODEL), row),
        pl.BlockSpec((nb, HIST_ROWS, W_A), first),
        _resident((None, 1, D_MODEL), lambda b, t: (lyr, 0, 0)),
        _resident((None, 1, D_MODEL), lambda b, t: (lyr, 0, 0)),
        _resident((None, D_MODEL, EVEN_IN), lambda b, t: (e, 0, 0)),
        _resident((None, len(POOL_WINDOWS), POOL_GDIM, POOL_GDIM), lambda b, t: (e, 0, 0, 0)),
        _resident((None, 1, W_A), lambda b, t: (e, 0, 0)),
        _resident((None, 1, W_B), lambda b, t: (e, 0, 0)),
        _resident((None, 1, W_B), lambda b, t: (e, 0, 0)),
        _resident((None, SGU_HEADS, SGU_CHUNK, SGU_CHUNK), lambda b, t: (e, 0, 0, 0)),
        _resident((None, SGU_CHUNK, SGU_HEADS), lambda b, t: (e, 0, 0)),
        _resident((None, EVEN_MIX, D_MODEL), lambda b, t: (e, 0, 0)),
    ]
    out_shape = [jax.ShapeDtypeStruct((bx, tx, D_MODEL), F32), jax.ShapeDtypeStruct((bx, HIST_ROWS, W_A), F32)]
    out_specs = [pl.BlockSpec((nb, tt, D_MODEL), row), pl.BlockSpec((nb, HIST_ROWS, W_A), first)]
    if emit_v:
        out_shape.append(jax.ShapeDtypeStruct((bx, tx, W_B), F32))
        out_specs.append(pl.BlockSpec((nb, tt, W_B), row))
    body = functools.partial(_even_body, nb=nb, tt=tt, seg=seg, pos0=pos0, emit_v=emit_v)
    return pl.pallas_call(
        body,
        grid=grid,
        in_specs=in_specs,
        out_specs=out_specs,
        out_shape=out_shape,
        scratch_shapes=[pltpu.VMEM((nb, HIST_ROWS + tt, W_A), F32), pltpu.VMEM((nb * tt, EVEN_MIX), F32)],
        compiler_params=pltpu.CompilerParams(
            dimension_semantics=("arbitrary", "arbitrary"), vmem_limit_bytes=VMEM_LIMIT_BYTES),
        name=name,
    )(x, hist, p["norm_pre"], p["norm_post"], p["win_e"], p["wpool"], p["pscale"], p["lng"], p["lnb"],
      p["ws"], p["bst"], p["wout_e"])


def _proj_body(x_ref, gpre_ref, win_ref, qn_ref, kvn_ref, wq_ref, wuk_ref, cos_ref, sin_ref,
               qa_ref, qr_ref, ckv_ref, kr_ref, ckvb_ref, krb_ref, gate_ref, *, nb, tt):
    m = nb * tt
    x = x_ref[...].reshape(m, D_MODEL)
    h = _rms(x, gpre_ref[...]).astype(BF16)
    z = _dot(h, win_ref[...])
    gate_ref[...] = z[:, ODD_G0:ODD_KR0].reshape(nb, tt, C_MIX)

    ckv = _rms(z[:, ODD_KV0:ODD_G0], kvn_ref[...])
    ckv_ref[...] = ckv.reshape(nb, tt, KV_LORA)
    ckvb_ref[...] = ckv.astype(BF16).reshape(nb, tt, KV_LORA)

    cos = cos_ref[...]
    sin = sin_ref[...]
    kr = (z[:, ODD_KR0:ODD_KR0 + QK_ROPE] * cos[:, :QK_ROPE]
          + z[:, ODD_KR0 + QK_ROPE:ODD_COLS] * sin[:, :QK_ROPE])
    kr_ref[...] = kr.reshape(nb, tt, QK_ROPE)
    krb_ref[...] = kr.astype(BF16).reshape(nb, tt, QK_ROPE)

    qcn = _rms(z[:, ODD_Q0:ODD_KV0], qn_ref[...]).astype(BF16)
    q = _dot(qcn, wq_ref[...])
    heads_per_vreg = LANES // QK_ROPE
    for c in range(N_HEADS // heads_per_vreg):
        r = (q[:, QUP_R0 + c * LANES:QUP_R0 + (c + 1) * LANES] * cos
             + q[:, QUP_S0 + c * LANES:QUP_S0 + (c + 1) * LANES] * sin).astype(BF16)
        for i in range(heads_per_vreg):
            qr_ref[:, heads_per_vreg * c + i, :, :] = r[:, i * QK_ROPE:(i + 1) * QK_ROPE].reshape(nb, tt, QK_ROPE)
    for hh in range(N_HEADS):
        q_nope = q[:, hh * QK_NOPE:(hh + 1) * QK_NOPE].astype(BF16)
        qa_ref[:, hh, :, :] = _dot(q_nope, wuk_ref[hh]).astype(BF16).reshape(nb, tt, KV_LORA)


def _odd_project(x, cos, sin, lyr, o, p, *, nb, tt, name):
    bx, tx, _ = x.shape
    assert bx % nb == 0 and tx % tt == 0 and tt % 16 == 0
    grid = (bx // nb, tx // tt)
    row = lambda b, t: (b, t, 0)
    hrow = lambda b, t: (b, 0, t, 0)
    in_specs = [
        pl.BlockSpec((nb, tt, D_MODEL), row),
        _resident((None, 1, D_MODEL), lambda b, t: (lyr, 0, 0)),
        _resident((None, D_MODEL, ODD_COLS), lambda b, t: (o, 0, 0)),
        _resident((None, 1, Q_LORA), lambda b, t: (o, 0, 0)),
        _resident((None, 1, KV_LORA), lambda b, t: (o, 0, 0)),
        _resident((None, Q_LORA, QUP_COLS), lambda b, t: (o, 0, 0)),
        _resident((None, N_HEADS, QK_NOPE, KV_LORA), lambda b, t: (o, 0, 0, 0)),
        pl.BlockSpec((nb * tt, LANES), lambda b, t: (t, 0)),
        pl.BlockSpec((nb * tt, LANES), lambda b, t: (t, 0)),
    ]
    out_shape = [
        jax.ShapeDtypeStruct((bx, N_HEADS, tx, KV_LORA), BF16),
        jax.ShapeDtypeStruct((bx, N_HEADS, tx, QK_ROPE), BF16),
        jax.ShapeDtypeStruct((bx, tx, KV_LORA), F32),
        jax.ShapeDtypeStruct((bx, tx, QK_ROPE), F32),
        jax.ShapeDtypeStruct((bx, tx, KV_LORA), BF16),
        jax.ShapeDtypeStruct((bx, tx, QK_ROPE), BF16),
        jax.ShapeDtypeStruct((bx, tx, C_MIX), F32),
    ]
    out_specs = [
        pl.BlockSpec((nb, N_HEADS, tt, KV_LORA), hrow),
        pl.BlockSpec((nb, N_HEADS, tt, QK_ROPE), hrow),
        pl.BlockSpec((nb, tt, KV_LORA), row),
        pl.BlockSpec((nb, tt, QK_ROPE), row),
        pl.BlockSpec((nb, tt, KV_LORA), row),
        pl.BlockSpec((nb, tt, QK_ROPE), row),
        pl.BlockSpec((nb, tt, C_MIX), row),
    ]
    return pl.pallas_call(
        functools.partial(_proj_body, nb=nb, tt=tt),
        grid=grid,
        in_specs=in_specs,
        out_specs=out_specs,
        out_shape=out_shape,
        compiler_params=pltpu.CompilerParams(
            dimension_semantics=("arbitrary", "arbitrary"), vmem_limit_bytes=VMEM_LIMIT_BYTES),
        name=name,
    )(x, p["norm_pre"], p["win_o"], p["qn"], p["kvn"], p["wq"], p["wuk"], cos, sin)


def _attn_body(qa_ref, qr_ref, ckv_ref, kr_ref, gate_ref, x_ref, wuv_ref, wo_ref, gpost_ref, xo_ref,
               m_ref, l_ref, acc_ref, *, tq, tk, q_pos0, kv_len):
    qi = pl.program_id(1)
    rows = N_HEADS * tq
    qa = qa_ref[0].reshape(rows, KV_LORA)
    qr = qr_ref[0].reshape(rows, QK_ROPE)
    m_ref[...] = jnp.full(m_ref.shape, -1e30, F32)
    l_ref[...] = jnp.zeros(l_ref.shape, F32)
    acc_ref[...] = jnp.zeros(acc_ref.shape, F32)

    q0 = q_pos0 + qi * tq
    lim_lo = jnp.minimum((q0 // CHUNK + 1) * CHUNK, kv_len)
    lim_hi = jnp.minimum(((q0 + tq - 1) // CHUNK + 1) * CHUNK, kv_len)
    n_full = lim_lo // tk
    n_need = (lim_hi + tk - 1) // tk
    c_exp = ATTN_SCALE * LOG2E

    def step(j, masked):
        k0 = pl.multiple_of(j * tk, tk)
        kc = ckv_ref[0, pl.ds(k0, tk), :]
        kk = kr_ref[0, pl.ds(k0, tk), :]
        s = _dot_nt(qa, kc) + _dot_nt(qr, kk)
        if masked:
            kpos = k0 + lax.broadcasted_iota(jnp.int32, (1, tq, tk), 2)
            qpos = q0 + lax.broadcasted_iota(jnp.int32, (1, tq, tk), 1)
            ok = jnp.logical_and(kpos // CHUNK <= qpos // CHUNK, kpos < kv_len)
            s = jnp.where(ok, s.reshape(N_HEADS, tq, tk), -jnp.inf).reshape(rows, tk)
        m_old = m_ref[...]
        m_new = jnp.maximum(m_old, jnp.max(s, axis=-1, keepdims=True))
        alpha = jnp.exp2((m_old - m_new) * c_exp)
        p = jnp.exp2((s - pltpu.repeat(m_new, tk // LANES, axis=1)) * c_exp)
        l_ref[...] = alpha * l_ref[...] + jnp.sum(p, axis=-1, keepdims=True)
        acc_ref[...] = acc_ref[...] * pltpu.repeat(alpha, KV_LORA // LANES, axis=1) + _dot(p.astype(BF16), kc)
        m_ref[...] = m_new

    def full_step(j, carry):
        step(j, False)
        return carry

    def masked_step(j, carry):
        step(j, True)
        return carry

    lax.fori_loop(0, n_full, full_step, 0)
    lax.fori_loop(n_full, n_need, masked_step, 0)

    o_lat = (acc_ref[...] * pltpu.repeat(1.0 / l_ref[...], KV_LORA // LANES, axis=1)).astype(BF16)
    o = jnp.concatenate([_dot(o_lat[hh * tq:(hh + 1) * tq], wuv_ref[hh]) for hh in range(N_HEADS)], axis=1)
    y = _dot((o * _silu(gate_ref[0])).astype(BF16), wo_ref[...])
    xo_ref[0] = x_ref[0] + _rms(y, gpost_ref[...])


def _odd_attend(qa, qr, ckvb, krb, gate, x, lyr, o, p, *, tq, tk, q_pos0, kv_len, name):
    bx, tx, _ = x.shape
    tkv = ckvb.shape[1]
    assert tx % tq == 0 and tkv % tk == 0 and tk % LANES == 0 and tq % 16 == 0 and kv_len <= tkv
    rows = N_HEADS * tq
    grid = (bx, tx // tq)
    in_specs = [
        pl.BlockSpec((1, N_HEADS, tq, KV_LORA), lambda b, i: (b, 0, i, 0)),
        pl.BlockSpec((1, N_HEADS, tq, QK_ROPE), lambda b, i: (b, 0, i, 0)),
        pl.BlockSpec((1, tkv, KV_LORA), lambda b, i: (b, 0, 0)),
        pl.BlockSpec((1, tkv, QK_ROPE), lambda b, i: (b, 0, 0)),
        pl.BlockSpec((1, tq, C_MIX), lambda b, i: (b, i, 0)),
        pl.BlockSpec((1, tq, D_MODEL), lambda b, i: (b, i, 0)),
        _resident((None, N_HEADS, KV_LORA, V_DIM), lambda b, i: (o, 0, 0, 0)),
        _resident((None, C_MIX, D_MODEL), lambda b, i: (o, 0, 0)),
        _resident((None, 1, D_MODEL), lambda b, i: (lyr, 0, 0)),
    ]
    return pl.pallas_call(
        functools.partial(_attn_body, tq=tq, tk=tk, q_pos0=q_pos0, kv_len=kv_len),
        grid=grid,
        in_specs=in_specs,
        out_specs=pl.BlockSpec((1, tq, D_MODEL), lambda b, i: (b, i, 0)),
        out_shape=jax.ShapeDtypeStruct((bx, tx, D_MODEL), F32),
        scratch_shapes=[pltpu.VMEM((rows, LANES), F32), pltpu.VMEM((rows, LANES), F32),
                        pltpu.VMEM((rows, KV_LORA), F32)],
        compiler_params=pltpu.CompilerParams(
            dimension_semantics=("arbitrary", "arbitrary"), vmem_limit_bytes=VMEM_LIMIT_BYTES),
        name=name,
    )(qa, qr, ckvb, krb, gate, x, p["wuv"], p["wo"], p["norm_post"])


def _rope_tables(pos):
    half = QK_ROPE // 2
    freqs = ROPE_BASE ** (-jnp.arange(half, dtype=F32) / half)
    ang = pos.astype(F32)[:, None] * freqs[None, :]
    c, s = jnp.cos(ang), jnp.sin(ang)
    reps = LANES // QK_ROPE
    return jnp.tile(jnp.concatenate([c, c], axis=1), (1, reps)), jnp.tile(jnp.concatenate([-s, s], axis=1), (1, reps))


def _half_swap(w):
    half = QK_ROPE // 2
    return jnp.concatenate([w[..., half:], w[..., :half]], axis=-1)


def _prepare_params(norm_pre, norm_post, w_in_even, w_pool, pool_scale, sgu_ln_g, sgu_ln_b, w_spatial, b_spatial,
                    w_out_even, w_in_odd, q_norm, kv_norm, w_q_up, w_kv_up, w_o):
    n_odd = w_in_odd.shape[0]
    kr_w = w_in_odd[..., Q_LORA + KV_LORA:Q_LORA + KV_LORA + QK_ROPE]
    win_o = jnp.concatenate([w_in_odd[..., :Q_LORA + KV_LORA], w_in_odd[..., Q_LORA + KV_LORA + QK_ROPE:],
                             kr_w, _half_swap(kr_w)], axis=-1)
    wq_r = w_q_up[..., QK_NOPE:]
    wq = jnp.concatenate([w_q_up[..., :QK_NOPE].reshape(n_odd, Q_LORA, N_HEADS * QK_NOPE),
                          wq_r.reshape(n_odd, Q_LORA, N_HEADS * QK_ROPE),
                          _half_swap(wq_r).reshape(n_odd, Q_LORA, N_HEADS * QK_ROPE)], axis=-1)
    return {
        "norm_pre": norm_pre[:, None, :],
        "norm_post": norm_post[:, None, :],
        "win_e": w_in_even.astype(BF16),
        "wpool": w_pool.astype(BF16),
        "pscale": pool_scale[:, None, :],
        "lng": sgu_ln_g[:, None, :],
        "lnb": sgu_ln_b[:, None, :],
        "ws": w_spatial,
        "bst": jnp.swapaxes(b_spatial, 1, 2),
        "wout_e": w_out_even.astype(BF16),
        "win_o": win_o.astype(BF16),
        "qn": q_norm[:, None, :],
        "kvn": kv_norm[:, None, :],
        "wq": wq.astype(BF16),
        "wuk": jnp.transpose(w_kv_up[..., :QK_NOPE], (0, 2, 3, 1)).astype(BF16),
        "wuv": jnp.transpose(w_kv_up[..., QK_NOPE:], (0, 2, 1, 3)).astype(BF16),
        "wo": w_o.astype(BF16),
    }


def _tile_sizes(t_prompt):
    tt_even = min(t_prompt, 256)
    tt_proj = min(t_prompt, 256)
    tq = min(t_prompt, 128)
    tk = min(t_prompt, 512)
    return tt_even, tt_proj, tq, tk


def kernel(x_prompt, x_sample, cache_pool, cache_ckv, cache_krope, norm_pre, norm_post, w_in_even, w_pool, pool_scale, sgu_ln_g, sgu_ln_b, w_spatial, b_spatial, w_out_even, w_in_odd, q_norm, kv_norm, w_q_up, w_kv_up, w_o):
    depth = norm_pre.shape[0]
    b, t, _ = x_prompt.shape
    db, s, _ = x_sample.shape
    past = cache_ckv.shape[2]
    p = _prepare_params(norm_pre, norm_post, w_in_even, w_pool, pool_scale, sgu_ln_g, sgu_ln_b, w_spatial,
                        b_spatial, w_out_even, w_in_odd, q_norm, kv_norm, w_q_up, w_kv_up, w_o)
    tt_even, tt_proj, tq, tk = _tile_sizes(t)

    cos_p, sin_p = _rope_tables(jnp.arange(t))
    cos_s, sin_s = _rope_tables(past + jnp.arange(s))
    cos_s, sin_s = jnp.tile(cos_s, (db, 1)), jnp.tile(sin_s, (db, 1))
    kv_len_s = past + s
    tk_s = -(-kv_len_s // LANES) * LANES
    pad_s = tk_s - kv_len_s

    xp, xs = x_prompt, x_sample
    pool_p, pool_s, sgu_s = [], [], []
    ckv_p, kr_p, ckv_s, kr_s = [], [], [], []
    zero_hist = jnp.zeros((b, HIST_ROWS, W_A), F32)
    for lyr in range(depth):
        if lyr % 2 == 0:
            e = lyr // 2
            xp, hp = _even_layer(xp, zero_hist, lyr, e, p, nb=1, tt=tt_even, pos0=0, emit_v=False,
                                 name=f"even{e}_prompt")
            hist_s = jnp.pad(cache_pool[e], ((0, 0), (HIST_ROWS - POOL_HIST, 0), (0, 0)))
            xs, hs, vs = _even_layer(xs, hist_s, lyr, e, p, nb=db, tt=s, pos0=past, emit_v=True,
                                     name=f"even{e}_sample")
            pool_p.append(hp[:, HIST_ROWS - POOL_HIST:])
            pool_s.append(hs[:, HIST_ROWS - POOL_HIST:])
            sgu_s.append(vs)
        else:
            o = lyr // 2
            qa, qr, ckv, kr, ckvb, krb, gate = _odd_project(xp, cos_p, sin_p, lyr, o, p, nb=1, tt=tt_proj,
                                                            name=f"proj{o}_prompt")
            xp = _odd_attend(qa, qr, ckvb, krb, gate, xp, lyr, o, p, tq=tq, tk=tk, q_pos0=0, kv_len=t,
                             name=f"attn{o}_prompt")
            ckv_p.append(ckv)
            kr_p.append(kr)

            qa, qr, ckv, kr, ckvb, krb, gate = _odd_project(xs, cos_s, sin_s, lyr, o, p, nb=db, tt=s,
                                                            name=f"proj{o}_sample")
            ckv_all = jnp.concatenate([cache_ckv[o].astype(BF16), ckvb, jnp.zeros((db, pad_s, KV_LORA), BF16)], axis=1)
            kr_all = jnp.concatenate([cache_krope[o].astype(BF16), krb, jnp.zeros((db, pad_s, QK_ROPE), BF16)], axis=1)
            xs = _odd_attend(qa, qr, ckv_all, kr_all, gate, xs, lyr, o, p, tq=s, tk=tk_s, q_pos0=past,
                             kv_len=kv_len_s, name=f"attn{o}_sample")
            ckv_s.append(ckv)
            kr_s.append(kr)
    return (xp, xs, jnp.stack(pool_p), jnp.stack(pool_s), jnp.stack(sgu_s),
            jnp.stack(ckv_p), jnp.stack(kr_p), jnp.stack(ckv_s), jnp.stack(kr_s))
```

```python
import functools

import jax
import jax.numpy as jnp
from jax import lax
from jax.experimental import pallas as pl
from jax.experimental.pallas import tpu as pltpu

D_MODEL = 1024
CHUNK = 64
EPS = 1e-6
POOL_WINDOWS = (2, 4, 8, 16)
POOL_HIST = max(POOL_WINDOWS) - 1
W_A = D_MODEL
POOL_GDIM = W_A // len(POOL_WINDOWS)
W_B = D_MODEL
SGU_HEADS = 4
SGU_HDIM = W_B // SGU_HEADS
SGU_CHUNK = 128
EVEN_MIX = W_A + W_B
EVEN_IN = W_A + 2 * W_B + EVEN_MIX
N_HEADS = D_MODEL // 128
QK_NOPE = 128
QK_ROPE = 64
V_DIM = 128
KV_LORA = D_MODEL // 4
Q_LORA = 3 * D_MODEL // 8
C_MIX = N_HEADS * V_DIM
ROPE_BASE = 10000.0
ATTN_SCALE = (QK_NOPE + QK_ROPE) ** -0.5
LOG2E = 1.4426950408889634

LANES = 128
HIST_ROWS = 16
VMEM_LIMIT_BYTES = 56 * 1024 * 1024
ATTN_BLOCK_ROWS = 256

ODD_Q0, ODD_KV0, ODD_G0, ODD_KR0 = 0, Q_LORA, Q_LORA + KV_LORA, Q_LORA + KV_LORA + C_MIX
ODD_COLS = ODD_KR0 + 2 * QK_ROPE
QUP_R0 = N_HEADS * QK_NOPE
QUP_S0 = QUP_R0 + N_HEADS * QK_ROPE
QUP_COLS = QUP_S0 + N_HEADS * QK_ROPE

F32 = jnp.float32
BF16 = jnp.bfloat16


def _dot(a, b):
    return jnp.dot(a, b, preferred_element_type=F32)


def _dot_nt(a, b):
    return lax.dot_general(a, b, (((1,), (1,)), ((), ())), preferred_element_type=F32)


def _rms(x, g):
    return x * lax.rsqrt(jnp.mean(x * x, axis=-1, keepdims=True) + EPS) * g


def _silu(x):
    return x * (1.0 / (1.0 + jnp.exp(-x)))


def _gelu(x):
    return 0.5 * x * (1.0 + lax.erf(x * (2.0 ** -0.5)))


def _even_body(x_ref, hist_ref, gpre_ref, gpost_ref, win_ref, wpool_ref, pscale_ref, lng_ref, lnb_ref,
               ws_ref, bst_ref, wout_ref, xo_ref, histo_ref, *rest, nb, tt, seg, pos0, emit_v):
    if emit_v:
        vo_ref, aext_ref, mix_ref = rest
    else:
        aext_ref, mix_ref = rest
    t = pl.program_id(1)
    nt = pl.num_programs(1)
    m = nb * tt

    @pl.when(t == 0)
    def _():
        aext_ref[:, 0:HIST_ROWS, :] = hist_ref[...]

    x = x_ref[...].reshape(m, D_MODEL)
    h = _rms(x, gpre_ref[...]).astype(BF16)

    a = _dot(h, win_ref[:, 0:W_A])
    aext_ref[:, HIST_ROWS:HIST_ROWS + tt, :] = a.reshape(nb, tt, W_A)
    pos = pos0 + t * tt + lax.broadcasted_iota(jnp.int32, (tt, 1), 0)
    for g, w in enumerate(POOL_WINDOWS):
        cs = slice(g * POOL_GDIM, (g + 1) * POOL_GDIM)
        inv_cnt = 1.0 / jnp.minimum(pos + 1, w).astype(F32)
        for s in range(nb):
            cur = aext_ref[s, HIST_ROWS:HIST_ROWS + tt, cs]
            wsum = cur
            for k in range(1, w):
                wsum = wsum + aext_ref[s, HIST_ROWS - k:HIST_ROWS - k + tt, cs]
            mix_ref[s * tt:(s + 1) * tt, cs] = wsum * inv_cnt - cur
        y_a = _dot(mix_ref[:, cs].astype(BF16), wpool_ref[g]) * pscale_ref[:, cs]
        mix_ref[:, cs] = y_a

    uv = _gelu(_dot(h, win_ref[:, W_A:W_A + 2 * W_B]))
    u = uv[:, :W_B]
    v = uv[:, W_B:]
    vc = v - jnp.mean(v, axis=-1, keepdims=True)
    vn = vc * lax.rsqrt(jnp.mean(vc * vc, axis=-1, keepdims=True) + EPS) * lng_ref[...] + lnb_ref[...]
    if emit_v:
        vo_ref[...] = vn.reshape(nb, tt, W_B)
    vb = vn.astype(BF16)
    ii = lax.broadcasted_iota(jnp.int32, (seg, seg), 0)
    jj = lax.broadcasted_iota(jnp.int32, (seg, seg), 1)
    causal = (jj // CHUNK) <= (ii // CHUNK)
    for g in range(SGU_HEADS):
        hs = slice(g * SGU_HDIM, (g + 1) * SGU_HDIM)
        wsg = jnp.where(causal, ws_ref[g, 0:seg, 0:seg], 0.0).astype(BF16)
        bcol = bst_ref[0:seg, g:g + 1]
        for c in range(m // seg):
            rs = slice(c * seg, (c + 1) * seg)
            mixed = _dot(wsg, vb[rs, hs]) + bcol
            mix_ref[rs, W_A + g * SGU_HDIM:W_A + (g + 1) * SGU_HDIM] = u[rs, hs] * mixed

    gate = _dot(h, win_ref[:, W_A + 2 * W_B:])
    mixv = (mix_ref[...] * _silu(gate)).astype(BF16)
    y = _dot(mixv, wout_ref[...])
    xo_ref[...] = (x + _rms(y, gpost_ref[...])).reshape(nb, tt, D_MODEL)

    tail = aext_ref[:, tt:tt + HIST_ROWS, :]

    @pl.when(t == nt - 1)
    def _():
        histo_ref[...] = tail

    aext_ref[:, 0:HIST_ROWS, :] = tail


def _resident(block_shape, index_map):
    return pl.BlockSpec(block_shape, index_map, pipeline_mode=pl.Buffered(1))


def _even_layer(x, hist, lyr, e, p, *, nb, tt, pos0, emit_v, name):
    bx, tx, _ = x.shape
    seg = min(tx, SGU_CHUNK)
    assert bx % nb == 0 and tx % tt == 0 and tt % seg == 0 and tt >= HIST_ROWS and tt % 8 == 0
    grid = (bx // nb, tx // tt)
    row = lambda b, t: (b, t, 0)
    first = lambda b, t: (b, 0, 0)
    in_specs = [
        pl.BlockSpec((nb, tt, D_MODEL), row),
        pl.BlockSpec((nb, HIST_ROWS, W_A), first),
        _resident((None, 1, D_MODEL), lambda b, t: (lyr, 0, 0)),
        _resident((None, 1, D_MODEL), lambda b, t: (lyr, 0, 0)),
        _resident((None, D_MODEL, EVEN_IN), lambda b, t: (e, 0, 0)),
        _resident((None, len(POOL_WINDOWS), POOL_GDIM, POOL_GDIM), lambda b, t: (e, 0, 0, 0)),
        _resident((None, 1, W_A), lambda b, t: (e, 0, 0)),
        _resident((None, 1, W_B), lambda b, t: (e, 0, 0)),
        _resident((None, 1, W_B), lambda b, t: (e, 0, 0)),
        _resident((None, SGU_HEADS, SGU_CHUNK, SGU_CHUNK), lambda b, t: (e, 0, 0, 0)),
        _resident((None, SGU_CHUNK, SGU_HEADS), lambda b, t: (e, 0, 0)),
        _resident((None, EVEN_MIX, D_MODEL), lambda b, t: (e, 0, 0)),
    ]
    out_shape = [jax.ShapeDtypeStruct((bx, tx, D_MODEL), F32), jax.ShapeDtypeStruct((bx, HIST_ROWS, W_A), F32)]
    out_specs = [pl.BlockSpec((nb, tt, D_MODEL), row), pl.BlockSpec((nb, HIST_ROWS, W_A), first)]
    if emit_v:
        out_shape.append(jax.ShapeDtypeStruct((bx, tx, W_B), F32))
        out_specs.append(pl.BlockSpec((nb, tt, W_B), row))
    body = functools.partial(_even_body, nb=nb, tt=tt, seg=seg, pos0=pos0, emit_v=emit_v)
    return pl.pallas_call(
        body,
        grid=grid,
        in_specs=in_specs,
        out_specs=out_specs,
        out_shape=out_shape,
        scratch_shapes=[pltpu.VMEM((nb, HIST_ROWS + tt, W_A), F32), pltpu.VMEM((nb * tt, EVEN_MIX), F32)],
        compiler_params=pltpu.CompilerParams(
            dimension_semantics=("arbitrary", "arbitrary"), vmem_limit_bytes=VMEM_LIMIT_BYTES),
        name=name,
    )(x, hist, p["norm_pre"], p["norm_post"], p["win_e"], p["wpool"], p["pscale"], p["lng"], p["lnb"],
      p["ws"], p["bst"], p["wout_e"])


def _proj_body(x_ref, gpre_ref, win_ref, qn_ref, kvn_ref, wq_ref, wuk_ref, cos_ref, sin_ref,
               qa_ref, qr_ref, ckv_ref, kr_ref, ckvb_ref, krb_ref, gate_ref, *, nb, tt):
    m = nb * tt
    x = x_ref[...].reshape(m, D_MODEL)
    h = _rms(x, gpre_ref[...]).astype(BF16)
    z = _dot(h, win_ref[...])
    gate_ref[...] = z[:, ODD_G0:ODD_KR0].reshape(nb, tt, C_MIX)

    ckv = _rms(z[:, ODD_KV0:ODD_G0], kvn_ref[...])
    ckv_ref[...] = ckv.reshape(nb, tt, KV_LORA)
    ckvb_ref[...] = ckv.astype(BF16).reshape(nb, tt, KV_LORA)

    cos = cos_ref[...]
    sin = sin_ref[...]
    kr = (z[:, ODD_KR0:ODD_KR0 + QK_ROPE] * cos[:, :QK_ROPE]
          + z[:, ODD_KR0 + QK_ROPE:ODD_COLS] * sin[:, :QK_ROPE])
    kr_ref[...] = kr.reshape(nb, tt, QK_ROPE)
    krb_ref[...] = kr.astype(BF16).reshape(nb, tt, QK_ROPE)

    qcn = _rms(z[:, ODD_Q0:ODD_KV0], qn_ref[...]).astype(BF16)
    q = _dot(qcn, wq_ref[...])
    c_exp = ATTN_SCALE * LOG2E
    heads_per_vreg = LANES // QK_ROPE
    for c in range(N_HEADS // heads_per_vreg):
        r = ((q[:, QUP_R0 + c * LANES:QUP_R0 + (c + 1) * LANES] * cos
              + q[:, QUP_S0 + c * LANES:QUP_S0 + (c + 1) * LANES] * sin) * c_exp).astype(BF16)
        for i in range(heads_per_vreg):
            qr_ref[:, heads_per_vreg * c + i, :, :] = r[:, i * QK_ROPE:(i + 1) * QK_ROPE].reshape(nb, tt, QK_ROPE)
    for hh in range(N_HEADS):
        q_nope = q[:, hh * QK_NOPE:(hh + 1) * QK_NOPE].astype(BF16)
        qa_ref[:, hh, :, :] = (_dot(q_nope, wuk_ref[hh]) * c_exp).astype(BF16).reshape(nb, tt, KV_LORA)


def _odd_project(x, cos, sin, lyr, o, p, *, nb, tt, name):
    bx, tx, _ = x.shape
    assert bx % nb == 0 and tx % tt == 0 and tt % 16 == 0
    grid = (bx // nb, tx // tt)
    row = lambda b, t: (b, t, 0)
    hrow = lambda b, t: (b, 0, t, 0)
    in_specs = [
        pl.BlockSpec((nb, tt, D_MODEL), row),
        _resident((None, 1, D_MODEL), lambda b, t: (lyr, 0, 0)),
        _resident((None, D_MODEL, ODD_COLS), lambda b, t: (o, 0, 0)),
        _resident((None, 1, Q_LORA), lambda b, t: (o, 0, 0)),
        _resident((None, 1, KV_LORA), lambda b, t: (o, 0, 0)),
        _resident((None, Q_LORA, QUP_COLS), lambda b, t: (o, 0, 0)),
        _resident((None, N_HEADS, QK_NOPE, KV_LORA), lambda b, t: (o, 0, 0, 0)),
        pl.BlockSpec((nb * tt, LANES), lambda b, t: (t, 0)),
        pl.BlockSpec((nb * tt, LANES), lambda b, t: (t, 0)),
    ]
    out_shape = [
        jax.ShapeDtypeStruct((bx, N_HEADS, tx, KV_LORA), BF16),
        jax.ShapeDtypeStruct((bx, N_HEADS, tx, QK_ROPE), BF16),
        jax.ShapeDtypeStruct((bx, tx, KV_LORA), F32),
        jax.ShapeDtypeStruct((bx, tx, QK_ROPE), F32),
        jax.ShapeDtypeStruct((bx, tx, KV_LORA), BF16),
        jax.ShapeDtypeStruct((bx, tx, QK_ROPE), BF16),
        jax.ShapeDtypeStruct((bx, tx, C_MIX), F32),
    ]
    out_specs = [
        pl.BlockSpec((nb, N_HEADS, tt, KV_LORA), hrow),
        pl.BlockSpec((nb, N_HEADS, tt, QK_ROPE), hrow),
        pl.BlockSpec((nb, tt, KV_LORA), row),
        pl.BlockSpec((nb, tt, QK_ROPE), row),
        pl.BlockSpec((nb, tt, KV_LORA), row),
        pl.BlockSpec((nb, tt, QK_ROPE), row),
        pl.BlockSpec((nb, tt, C_MIX), row),
    ]
    return pl.pallas_call(
        functools.partial(_proj_body, nb=nb, tt=tt),
        grid=grid,
        in_specs=in_specs,
        out_specs=out_specs,
        out_shape=out_shape,
        compiler_params=pltpu.CompilerParams(
            dimension_semantics=("arbitrary", "arbitrary"), vmem_limit_bytes=VMEM_LIMIT_BYTES),
        name=name,
    )(x, p["norm_pre"], p["win_o"], p["qn"], p["kvn"], p["wq"], p["wuk"], cos, sin)


def _attn_body(qa_ref, qr_ref, ckv_ref, kr_ref, gate_ref, x_ref, wuv_ref, wo_ref, gpost_ref, xo_ref,
               m_ref, l_ref, acc_ref, s_ref, *, tq, tk, hb, q_pos0, kv_len):
    qi = pl.program_id(1)
    rb = hb * tq
    m_ref[...] = jnp.full(m_ref.shape, -1e30, F32)
    l_ref[...] = jnp.zeros(l_ref.shape, F32)
    acc_ref[...] = jnp.zeros(acc_ref.shape, F32)

    q0 = q_pos0 + qi * tq
    lim_hi = jnp.minimum(((q0 + tq - 1) // CHUNK + 1) * CHUNK, kv_len)
    n_need = (lim_hi + tk - 1) // tk

    def key_tile(j):
        k0 = pl.multiple_of(j * tk, tk)
        return ckv_ref[0, pl.ds(k0, tk), :], kr_ref[0, pl.ds(k0, tk), :]

    def scores(b, kc, kk):
        qa = qa_ref[0, b * hb:(b + 1) * hb].reshape(rb, KV_LORA)
        qr = qr_ref[0, b * hb:(b + 1) * hb].reshape(rb, QK_ROPE)
        s_ref[b * rb:(b + 1) * rb] = _dot_nt(qa, kc) + _dot_nt(qr, kk)

    def softmax_pv(b, kc, ok):
        rs = slice(b * rb, (b + 1) * rb)
        s = s_ref[rs]
        if ok is not None:
            s = jnp.where(ok, s.reshape(hb, tq, tk), -jnp.inf).reshape(rb, tk)
        m_old = m_ref[rs]
        m_new = jnp.maximum(m_old, jnp.max(s, axis=-1, keepdims=True))
        alpha = jnp.exp2(m_old - m_new)
        p = jnp.exp2(s - pltpu.repeat(m_new, tk // LANES, axis=1))
        l_ref[rs] = alpha * l_ref[rs] + jnp.sum(p, axis=-1, keepdims=True)
        acc_ref[rs] = acc_ref[rs] * pltpu.repeat(alpha, KV_LORA // LANES, axis=1) + _dot(p.astype(BF16), kc)
        m_ref[rs] = m_new

    n_blocks = N_HEADS // hb
    kc0, kk0 = key_tile(0)
    for b in range(n_blocks):
        scores(b, kc0, kk0)

    def pipelined_step(j, carry):
        kc, _ = key_tile(j)
        kc_next, kk_next = key_tile(j + 1)
        for b in range(n_blocks):
            softmax_pv(b, kc, None)
            scores(b, kc_next, kk_next)
        return carry

    lax.fori_loop(0, n_need - 1, pipelined_step, 0)

    last = n_need - 1
    kc, _ = key_tile(last)
    kpos = last * tk + lax.broadcasted_iota(jnp.int32, (1, tq, tk), 2)
    qpos = q0 + lax.broadcasted_iota(jnp.int32, (1, tq, tk), 1)
    ok = jnp.logical_and(kpos // CHUNK <= qpos // CHUNK, kpos < kv_len)
    for b in range(n_blocks):
        softmax_pv(b, kc, ok)

    o_lat = (acc_ref[...] * pltpu.repeat(1.0 / l_ref[...], KV_LORA // LANES, axis=1)).astype(BF16)
    o = jnp.concatenate([_dot(o_lat[hh * tq:(hh + 1) * tq], wuv_ref[hh]) for hh in range(N_HEADS)], axis=1)
    y = _dot((o * _silu(gate_ref[0])).astype(BF16), wo_ref[...])
    xo_ref[0] = x_ref[0] + _rms(y, gpost_ref[...])


def _odd_attend(qa, qr, ckvb, krb, gate, x, lyr, o, p, *, tq, tk, q_pos0, kv_len, name):
    bx, tx, _ = x.shape
    tkv = ckvb.shape[1]
    assert tx % tq == 0 and tkv % tk == 0 and tk % LANES == 0 and tq % 16 == 0 and kv_len <= tkv
    hb = max(1, min(N_HEADS, ATTN_BLOCK_ROWS // tq))
    assert N_HEADS % hb == 0
    assert tkv == tk or (tk % tq == 0 and q_pos0 % tq == 0 and tq % CHUNK == 0)
    rows = N_HEADS * tq
    grid = (bx, tx // tq)
    in_specs = [
        pl.BlockSpec((1, N_HEADS, tq, KV_LORA), lambda b, i: (b, 0, i, 0)),
        pl.BlockSpec((1, N_HEADS, tq, QK_ROPE), lambda b, i: (b, 0, i, 0)),
        pl.BlockSpec((1, tkv, KV_LORA), lambda b, i: (b, 0, 0)),
        pl.BlockSpec((1, tkv, QK_ROPE), lambda b, i: (b, 0, 0)),
        pl.BlockSpec((1, tq, C_MIX), lambda b, i: (b, i, 0)),
        pl.BlockSpec((1, tq, D_MODEL), lambda b, i: (b, i, 0)),
        _resident((None, N_HEADS, KV_LORA, V_DIM), lambda b, i: (o, 0, 0, 0)),
        _resident((None, C_MIX, D_MODEL), lambda b, i: (o, 0, 0)),
        _resident((None, 1, D_MODEL), lambda b, i: (lyr, 0, 0)),
    ]
    return pl.pallas_call(
        functools.partial(_attn_body, tq=tq, tk=tk, hb=hb, q_pos0=q_pos0, kv_len=kv_len),
        grid=grid,
        in_specs=in_specs,
        out_specs=pl.BlockSpec((1, tq, D_MODEL), lambda b, i: (b, i, 0)),
        out_shape=jax.ShapeDtypeStruct((bx, tx, D_MODEL), F32),
        scratch_shapes=[pltpu.VMEM((rows, LANES), F32), pltpu.VMEM((rows, LANES), F32),
                        pltpu.VMEM((rows, KV_LORA), F32), pltpu.VMEM((rows, tk), F32)],
        compiler_params=pltpu.CompilerParams(
            dimension_semantics=("arbitrary", "arbitrary"), vmem_limit_bytes=VMEM_LIMIT_BYTES),
        name=name,
    )(qa, qr, ckvb, krb, gate, x, p["wuv"], p["wo"], p["norm_post"])


def _rope_tables(pos):
    half = QK_ROPE // 2
    freqs = ROPE_BASE ** (-jnp.arange(half, dtype=F32) / half)
    ang = pos.astype(F32)[:, None] * freqs[None, :]
    c, s = jnp.cos(ang), jnp.sin(ang)
    reps = LANES // QK_ROPE
    return jnp.tile(jnp.concatenate([c, c], axis=1), (1, reps)), jnp.tile(jnp.concatenate([-s, s], axis=1), (1, reps))


def _half_swap(w):
    half = QK_ROPE // 2
    return jnp.concatenate([w[..., half:], w[..., :half]], axis=-1)


def _prepare_params(norm_pre, norm_post, w_in_even, w_pool, pool_scale, sgu_ln_g, sgu_ln_b, w_spatial, b_spatial,
                    w_out_even, w_in_odd, q_norm, kv_norm, w_q_up, w_kv_up, w_o):
    n_odd = w_in_odd.shape[0]
    kr_w = w_in_odd[..., Q_LORA + KV_LORA:Q_LORA + KV_LORA + QK_ROPE]
    win_o = jnp.concatenate([w_in_odd[..., :Q_LORA + KV_LORA], w_in_odd[..., Q_LORA + KV_LORA + QK_ROPE:],
                             kr_w, _half_swap(kr_w)], axis=-1)
    wq_r = w_q_up[..., QK_NOPE:]
    wq = jnp.concatenate([w_q_up[..., :QK_NOPE].reshape(n_odd, Q_LORA, N_HEADS * QK_NOPE),
                          wq_r.reshape(n_odd, Q_LORA, N_HEADS * QK_ROPE),
                          _half_swap(wq_r).reshape(n_odd, Q_LORA, N_HEADS * QK_ROPE)], axis=-1)
    return {
        "norm_pre": norm_pre[:, None, :],
        "norm_post": norm_post[:, None, :],
        "win_e": w_in_even.astype(BF16),
        "wpool": w_pool.astype(BF16),
        "pscale": pool_scale[:, None, :],
        "lng": sgu_ln_g[:, None, :],
        "lnb": sgu_ln_b[:, None, :],
        "ws": w_spatial,
        "bst": jnp.swapaxes(b_spatial, 1, 2),
        "wout_e": w_out_even.astype(BF16),
        "win_o": win_o.astype(BF16),
        "qn": q_norm[:, None, :],
        "kvn": kv_norm[:, None, :],
        "wq": wq.astype(BF16),
        "wuk": jnp.transpose(w_kv_up[..., :QK_NOPE], (0, 2, 3, 1)).astype(BF16),
        "wuv": jnp.transpose(w_kv_up[..., QK_NOPE:], (0, 2, 1, 3)).astype(BF16),
        "wo": w_o.astype(BF16),
    }


def _tile_sizes(t_prompt):
    tt_even = min(t_prompt, 256)
    tt_proj = min(t_prompt, 256)
    tq = min(t_prompt, 256)
    tk = min(t_prompt, 512)
    return tt_even, tt_proj, tq, tk


def kernel(x_prompt, x_sample, cache_pool, cache_ckv, cache_krope, norm_pre, norm_post, w_in_even, w_pool, pool_scale, sgu_ln_g, sgu_ln_b, w_spatial, b_spatial, w_out_even, w_in_odd, q_norm, kv_norm, w_q_up, w_kv_up, w_o):
    depth = norm_pre.shape[0]
    b, t, _ = x_prompt.shape
    db, s, _ = x_sample.shape
    past = cache_ckv.shape[2]
    p = _prepare_params(norm_pre, norm_post, w_in_even, w_pool, pool_scale, sgu_ln_g, sgu_ln_b, w_spatial,
                        b_spatial, w_out_even, w_in_odd, q_norm, kv_norm, w_q_up, w_kv_up, w_o)
    tt_even, tt_proj, tq, tk = _tile_sizes(t)

    cos_p, sin_p = _rope_tables(jnp.arange(t))
    cos_s, sin_s = _rope_tables(past + jnp.arange(s))
    cos_s, sin_s = jnp.tile(cos_s, (db, 1)), jnp.tile(sin_s, (db, 1))
    kv_len_s = past + s
    tk_s = -(-kv_len_s // LANES) * LANES
    pad_s = tk_s - kv_len_s

    xp, xs = x_prompt, x_sample
    pool_p, pool_s, sgu_s = [], [], []
    ckv_p, kr_p, ckv_s, kr_s = [], [], [], []
    zero_hist = jnp.zeros((b, HIST_ROWS, W_A), F32)
    for lyr in range(depth):
        if lyr % 2 == 0:
            e = lyr // 2
            xp, hp = _even_layer(xp, zero_hist, lyr, e, p, nb=1, tt=tt_even, pos0=0, emit_v=False,
                                 name=f"even{e}_prompt")
            hist_s = jnp.pad(cache_pool[e], ((0, 0), (HIST_ROWS - POOL_HIST, 0), (0, 0)))
            xs, hs, vs = _even_layer(xs, hist_s, lyr, e, p, nb=db, tt=s, pos0=past, emit_v=True,
                                     name=f"even{e}_sample")
            pool_p.append(hp[:, HIST_ROWS - POOL_HIST:])
            pool_s.append(hs[:, HIST_ROWS - POOL_HIST:])
            sgu_s.append(vs)
        else:
            o = lyr // 2
            qa, qr, ckv, kr, ckvb, krb, gate = _odd_project(xp, cos_p, sin_p, lyr, o, p, nb=1, tt=tt_proj,
                                                            name=f"proj{o}_prompt")
            xp = _odd_attend(qa, qr, ckvb, krb, gate, xp, lyr, o, p, tq=tq, tk=tk, q_pos0=0, kv_len=t,
                             name=f"attn{o}_prompt")
            ckv_p.append(ckv)
            kr_p.append(kr)

            qa, qr, ckv, kr, ckvb, krb, gate = _odd_project(xs, cos_s, sin_s, lyr, o, p, nb=db, tt=s,
                                                            name=f"proj{o}_sample")
            ckv_all = jnp.concatenate([cache_ckv[o].astype(BF16), ckvb, jnp.zeros((db, pad_s, KV_LORA), BF16)], axis=1)
            kr_all = jnp.concatenate([cache_krope[o].astype(BF16), krb, jnp.zeros((db, pad_s, QK_ROPE), BF16)], axis=1)
            xs = _odd_attend(qa, qr, ckv_all, kr_all, gate, xs, lyr, o, p, tq=s, tk=tk_s, q_pos0=past,
                             kv_len=kv_len_s, name=f"attn{o}_sample")
            ckv_s.append(ckv)
            kr_s.append(kr)
    return (xp, xs, jnp.stack(pool_p), jnp.stack(pool_s), jnp.stack(sgu_s),
            jnp.stack(ckv_p), jnp.stack(kr_p), jnp.stack(ckv_s), jnp.stack(kr_s))
```

```python
import functools

import jax
import jax.numpy as jnp
from jax import lax
from jax.experimental import pallas as pl
from jax.experimental.pallas import tpu as pltpu

D_MODEL = 1024
CHUNK = 64
EPS = 1e-6
POOL_WINDOWS = (2, 4, 8, 16)
POOL_HIST = max(POOL_WINDOWS) - 1
W_A = D_MODEL
POOL_GDIM = W_A // len(POOL_WINDOWS)
W_B = D_MODEL
SGU_HEADS = 4
SGU_HDIM = W_B // SGU_HEADS
SGU_CHUNK = 128
EVEN_MIX = W_A + W_B
EVEN_IN = W_A + 2 * W_B + EVEN_MIX
N_HEADS = D_MODEL // 128
QK_NOPE = 128
QK_ROPE = 64
V_DIM = 128
KV_LORA = D_MODEL // 4
Q_LORA = 3 * D_MODEL // 8
C_MIX = N_HEADS * V_DIM
ROPE_BASE = 10000.0
ATTN_SCALE = (QK_NOPE + QK_ROPE) ** -0.5
LOG2E = 1.4426950408889634

LANES = 128
HIST_ROWS = 16
VMEM_LIMIT_BYTES = 56 * 1024 * 1024
ATTN_BLOCK_ROWS = 256
FLAT_BLOCK_ROWS = 512
HEAD_LANES = 2 * LANES
BIAS_LANE0 = LANES + QK_ROPE
MASK_BIAS = -1e30

ODD_Q0, ODD_KV0, ODD_G0, ODD_KR0 = 0, Q_LORA, Q_LORA + KV_LORA, Q_LORA + KV_LORA + C_MIX
ODD_COLS = ODD_KR0 + 2 * QK_ROPE
QUP_R0 = N_HEADS * QK_NOPE
QUP_S0 = QUP_R0 + N_HEADS * QK_ROPE
QUP_COLS = QUP_S0 + N_HEADS * QK_ROPE

F32 = jnp.float32
BF16 = jnp.bfloat16


def _dot(a, b):
    return jnp.dot(a, b, preferred_element_type=F32)


def _dot_nt(a, b):
    return lax.dot_general(a, b, (((1,), (1,)), ((), ())), preferred_element_type=F32)


def _rms(x, g):
    return x * lax.rsqrt(jnp.mean(x * x, axis=-1, keepdims=True) + EPS) * g


def _silu(x):
    return x * (1.0 / (1.0 + jnp.exp(-x)))


def _gelu(x):
    return 0.5 * x * (1.0 + lax.erf(x * (2.0 ** -0.5)))


def _even_body(x_ref, hist_ref, gpre_ref, gpost_ref, win_ref, wpool_ref, pscale_ref, lng_ref, lnb_ref,
               ws_ref, bst_ref, wout_ref, xo_ref, histo_ref, *rest, nb, tt, seg, pos0, emit_v):
    if emit_v:
        vo_ref, aext_ref, mix_ref = rest
    else:
        aext_ref, mix_ref = rest
    t = pl.program_id(1)
    nt = pl.num_programs(1)
    m = nb * tt

    @pl.when(t == 0)
    def _():
        aext_ref[:, 0:HIST_ROWS, :] = hist_ref[...]

    x = x_ref[...].reshape(m, D_MODEL)
    h = _rms(x, gpre_ref[...]).astype(BF16)

    a = _dot(h, win_ref[:, 0:W_A])
    aext_ref[:, HIST_ROWS:HIST_ROWS + tt, :] = a.reshape(nb, tt, W_A)
    pos = pos0 + t * tt + lax.broadcasted_iota(jnp.int32, (tt, 1), 0)
    for g, w in enumerate(POOL_WINDOWS):
        cs = slice(g * POOL_GDIM, (g + 1) * POOL_GDIM)
        inv_cnt = 1.0 / jnp.minimum(pos + 1, w).astype(F32)
        for s in range(nb):
            cur = aext_ref[s, HIST_ROWS:HIST_ROWS + tt, cs]
            wsum = cur
            for k in range(1, w):
                wsum = wsum + aext_ref[s, HIST_ROWS - k:HIST_ROWS - k + tt, cs]
            mix_ref[s * tt:(s + 1) * tt, cs] = wsum * inv_cnt - cur
        y_a = _dot(mix_ref[:, cs].astype(BF16), wpool_ref[g]) * pscale_ref[:, cs]
        mix_ref[:, cs] = y_a

    uv = _gelu(_dot(h, win_ref[:, W_A:W_A + 2 * W_B]))
    u = uv[:, :W_B]
    v = uv[:, W_B:]
    vc = v - jnp.mean(v, axis=-1, keepdims=True)
    vn = vc * lax.rsqrt(jnp.mean(vc * vc, axis=-1, keepdims=True) + EPS) * lng_ref[...] + lnb_ref[...]
    if emit_v:
        vo_ref[...] = vn.reshape(nb, tt, W_B)
    vb = vn.astype(BF16)
    ii = lax.broadcasted_iota(jnp.int32, (seg, seg), 0)
    jj = lax.broadcasted_iota(jnp.int32, (seg, seg), 1)
    causal = (jj // CHUNK) <= (ii // CHUNK)
    for g in range(SGU_HEADS):
        hs = slice(g * SGU_HDIM, (g + 1) * SGU_HDIM)
        wsg = jnp.where(causal, ws_ref[g, 0:seg, 0:seg], 0.0).astype(BF16)
        bcol = bst_ref[0:seg, g:g + 1]
        for c in range(m // seg):
            rs = slice(c * seg, (c + 1) * seg)
            mixed = _dot(wsg, vb[rs, hs]) + bcol
            mix_ref[rs, W_A + g * SGU_HDIM:W_A + (g + 1) * SGU_HDIM] = u[rs, hs] * mixed

    gate = _dot(h, win_ref[:, W_A + 2 * W_B:])
    mixv = (mix_ref[...] * _silu(gate)).astype(BF16)
    y = _dot(mixv, wout_ref[...])
    xo_ref[...] = (x + _rms(y, gpost_ref[...])).reshape(nb, tt, D_MODEL)

    tail = aext_ref[:, tt:tt + HIST_ROWS, :]

    @pl.when(t == nt - 1)
    def _():
        histo_ref[...] = tail

    aext_ref[:, 0:HIST_ROWS, :] = tail


def _resident(block_shape, index_map):
    return pl.BlockSpec(block_shape, index_map, pipeline_mode=pl.Buffered(1))


def _even_layer(x, hist, lyr, e, p, *, nb, tt, pos0, emit_v, name):
    bx, tx, _ = x.shape
    seg = min(tx, SGU_CHUNK)
    assert bx % nb == 0 and tx % tt == 0 and tt % seg == 0 and tt >= HIST_ROWS and tt % 8 == 0
    grid = (bx // nb, tx // tt)
    row = lambda b, t: (b, t, 0)
    first = lambda b, t: (b, 0, 0)
    in_specs = [
        pl.BlockSpec((nb, tt, D_MODEL), row),
        pl.BlockSpec((nb, HIST_ROWS, W_A), first),
        _resident((None, 1, D_MODEL), lambda b, t: (lyr, 0, 0)),
        _resident((None, 1, D_MODEL), lambda b, t: (lyr, 0, 0)),
        _resident((None, D_MODEL, EVEN_IN), lambda b, t: (e, 0, 0)),
        _resident((None, len(POOL_WINDOWS), POOL_GDIM, POOL_GDIM), lambda b, t: (e, 0, 0, 0)),
        _resident((None, 1, W_A), lambda b, t: (e, 0, 0)),
        _resident((None, 1, W_B), lambda b, t: (e, 0, 0)),
        _resident((None, 1, W_B), lambda b, t: (e, 0, 0)),
        _resident((None, SGU_HEADS, SGU_CHUNK, SGU_CHUNK), lambda b, t: (e, 0, 0, 0)),
        _resident((None, SGU_CHUNK, SGU_HEADS), lambda b, t: (e, 0, 0)),
        _resident((None, EVEN_MIX, D_MODEL), lambda b, t: (e, 0, 0)),
    ]
    out_shape = [jax.ShapeDtypeStruct((bx, tx, D_MODEL), F32), jax.ShapeDtypeStruct((bx, HIST_ROWS, W_A), F32)]
    out_specs = [pl.BlockSpec((nb, tt, D_MODEL), row), pl.BlockSpec((nb, HIST_ROWS, W_A), first)]
    if emit_v:
        out_shape.append(jax.ShapeDtypeStruct((bx, tx, W_B), F32))
        out_specs.append(pl.BlockSpec((nb, tt, W_B), row))
    body = functools.partial(_even_body, nb=nb, tt=tt, seg=seg, pos0=pos0, emit_v=emit_v)
    return pl.pallas_call(
        body,
        grid=grid,
        in_specs=in_specs,
        out_specs=out_specs,
        out_shape=out_shape,
        scratch_shapes=[pltpu.VMEM((nb, HIST_ROWS + tt, W_A), F32), pltpu.VMEM((nb * tt, EVEN_MIX), F32)],
        compiler_params=pltpu.CompilerParams(
            dimension_semantics=("arbitrary", "arbitrary"), vmem_limit_bytes=VMEM_LIMIT_BYTES),
        name=name,
    )(x, hist, p["norm_pre"], p["norm_post"], p["win_e"], p["wpool"], p["pscale"], p["lng"], p["lnb"],
      p["ws"], p["bst"], p["wout_e"])


def _proj_body(x_ref, gpre_ref, win_ref, qn_ref, kvn_ref, wq_ref, wuk_ref, cos_ref, sin_ref,
               qa_ref, qr_ref, ckv_ref, kr_ref, ckvb_ref, krb_ref, gate_ref, *, nb, tt):
    m = nb * tt
    x = x_ref[...].reshape(m, D_MODEL)
    h = _rms(x, gpre_ref[...]).astype(BF16)
    z = _dot(h, win_ref[...])
    gate_ref[...] = z[:, ODD_G0:ODD_KR0].reshape(nb, tt, C_MIX)

    ckv = _rms(z[:, ODD_KV0:ODD_G0], kvn_ref[...])
    ckv_ref[...] = ckv.reshape(nb, tt, KV_LORA)
    ckvb_ref[...] = ckv.astype(BF16).reshape(nb, tt, KV_LORA)

    cos = cos_ref[...]
    sin = sin_ref[...]
    kr = (z[:, ODD_KR0:ODD_KR0 + QK_ROPE] * cos[:, :QK_ROPE]
          + z[:, ODD_KR0 + QK_ROPE:ODD_COLS] * sin[:, :QK_ROPE])
    kr_ref[...] = kr.reshape(nb, tt, QK_ROPE)
    krb_ref[...] = kr.astype(BF16).reshape(nb, tt, QK_ROPE)

    qcn = _rms(z[:, ODD_Q0:ODD_KV0], qn_ref[...]).astype(BF16)
    q = _dot(qcn, wq_ref[...])
    c_exp = ATTN_SCALE * LOG2E
    heads_per_vreg = LANES // QK_ROPE
    for c in range(N_HEADS // heads_per_vreg):
        r = ((q[:, QUP_R0 + c * LANES:QUP_R0 + (c + 1) * LANES] * cos
              + q[:, QUP_S0 + c * LANES:QUP_S0 + (c + 1) * LANES] * sin) * c_exp).astype(BF16)
        for i in range(heads_per_vreg):
            qr_ref[:, heads_per_vreg * c + i, :, :] = r[:, i * QK_ROPE:(i + 1) * QK_ROPE].reshape(nb, tt, QK_ROPE)
    for hh in range(N_HEADS):
        q_nope = q[:, hh * QK_NOPE:(hh + 1) * QK_NOPE].astype(BF16)
        qa_ref[:, hh, :, :] = (_dot(q_nope, wuk_ref[hh]) * c_exp).astype(BF16).reshape(nb, tt, KV_LORA)


def _proj_heads_body(x_ref, gpre_ref, win_ref, qn_ref, kvn_ref, wq_ref, wukv_ref, cos_ref, sin_ref, cs_ref,
                     q_ref, k_ref, v_ref, ckv_ref, kr_ref, gate_ref, *, tt, tile):
    x = x_ref[0]
    h = _rms(x, gpre_ref[...]).astype(BF16)
    z = _dot(h, win_ref[...])
    gate_ref[0] = z[:, ODD_G0:ODD_KR0]
    ckv = _rms(z[:, ODD_KV0:ODD_G0], kvn_ref[...])
    ckv_ref[0] = ckv

    lane = lax.broadcasted_iota(jnp.int32, (tt, LANES), 1)
    low = lane < QK_ROPE
    u = z[:, ODD_KR0:ODD_COLS] * cs_ref[...]
    kr_full = u + pltpu.roll(u, QK_ROPE, axis=1)
    kr_ref[0] = kr_full[:, :QK_ROPE]
    pos = pl.program_id(1) * tt + lax.broadcasted_iota(jnp.int32, (tt, LANES), 0)
    onehot = (lane - QK_ROPE == (pos % tile) // CHUNK).astype(F32)
    k_hi = jnp.where(low, kr_full, onehot).astype(BF16)
    v_hi = jnp.ones((tt, LANES), BF16)

    kv = _dot(ckv.astype(BF16), wukv_ref[...])
    qcn = _rms(z[:, ODD_Q0:ODD_KV0], qn_ref[...]).astype(BF16)
    q = _dot(qcn, wq_ref[...])
    c_exp = ATTN_SCALE * LOG2E
    heads_per_vreg = LANES // QK_ROPE
    for hh in range(N_HEADS):
        c, i = divmod(hh, heads_per_vreg)
        r = (q[:, QUP_R0 + c * LANES:QUP_R0 + (c + 1) * LANES] * cos_ref[...]
             + q[:, QUP_S0 + c * LANES:QUP_S0 + (c + 1) * LANES] * sin_ref[...]) * c_exp
        if i:
            r = pltpu.roll(r, LANES - i * QK_ROPE, axis=1)
        q_ref[0, hh, :, 0:LANES] = (q[:, hh * QK_NOPE:(hh + 1) * QK_NOPE] * c_exp).astype(BF16)
        q_ref[0, hh, :, LANES:2 * LANES] = jnp.where(low, r, 0.0).astype(BF16)
        k_ref[0, hh, :, 0:LANES] = kv[:, hh * QK_NOPE:(hh + 1) * QK_NOPE].astype(BF16)
        k_ref[0, hh, :, LANES:2 * LANES] = k_hi
        v_ref[0, hh, :, 0:LANES] = kv[:, C_MIX + hh * V_DIM:C_MIX + (hh + 1) * V_DIM].astype(BF16)
        v_ref[0, hh, :, LANES:2 * LANES] = v_hi


def _odd_project_heads(x, cos, sin, cs, lyr, o, p, *, tt, tile, name):
    bx, tx, _ = x.shape
    assert tx % tt == 0 and tt % 16 == 0 and tile % CHUNK == 0 and tile // CHUNK <= LANES - QK_ROPE
    grid = (bx, tx // tt)
    row = lambda b, t: (b, t, 0)
    hrow = lambda b, t: (b, 0, t, 0)
    tab = pl.BlockSpec((tt, LANES), lambda b, t: (t, 0))
    in_specs = [
        pl.BlockSpec((1, tt, D_MODEL), row),
        _resident((None, 1, D_MODEL), lambda b, t: (lyr, 0, 0)),
        _resident((None, D_MODEL, ODD_COLS), lambda b, t: (o, 0, 0)),
        _resident((None, 1, Q_LORA), lambda b, t: (o, 0, 0)),
        _resident((None, 1, KV_LORA), lambda b, t: (o, 0, 0)),
        _resident((None, Q_LORA, QUP_COLS), lambda b, t: (o, 0, 0)),
        _resident((None, KV_LORA, 2 * C_MIX), lambda b, t: (o, 0, 0)),
        tab, tab, tab,
    ]
    head_arr = jax.ShapeDtypeStruct((bx, N_HEADS, tx, HEAD_LANES), BF16)
    head_spec = pl.BlockSpec((1, N_HEADS, tt, HEAD_LANES), hrow)
    out_shape = [head_arr, head_arr, head_arr,
                 jax.ShapeDtypeStruct((bx, tx, KV_LORA), F32),
                 jax.ShapeDtypeStruct((bx, tx, QK_ROPE), F32),
                 jax.ShapeDtypeStruct((bx, tx, C_MIX), F32)]
    out_specs = [head_spec, head_spec, head_spec,
                 pl.BlockSpec((1, tt, KV_LORA), row),
                 pl.BlockSpec((1, tt, QK_ROPE), row),
                 pl.BlockSpec((1, tt, C_MIX), row)]
    return pl.pallas_call(
        functools.partial(_proj_heads_body, tt=tt, tile=tile),
        grid=grid,
        in_specs=in_specs,
        out_specs=out_specs,
        out_shape=out_shape,
        compiler_params=pltpu.CompilerParams(
            dimension_semantics=("arbitrary", "arbitrary"), vmem_limit_bytes=VMEM_LIMIT_BYTES),
        name=name,
    )(x, p["norm_pre"], p["win_o"], p["qn"], p["kvn"], p["wq"], p["wukv"], cos, sin, cs)


def _flat_attn_body(q_ref, k_ref, v_ref, o_ref, m_ref, acc_ref, s0_ref, s1_ref, dbias_ref, *, tile, nq, rb):
    nblk = tile // rb
    row = lax.broadcasted_iota(jnp.int32, (tile, HEAD_LANES), 0)
    idx = lax.broadcasted_iota(jnp.int32, (tile, HEAD_LANES), 1) - BIAS_LANE0
    hidden = jnp.logical_and(jnp.logical_and(idx >= 0, idx < tile // CHUNK), idx > row // CHUNK)
    dbias_ref[...] = jnp.where(hidden, MASK_BIAS, 0.0).astype(BF16)
    acc_ref[...] = jnp.zeros(acc_ref.shape, F32)

    def scores(qi, j, s_ref):
        k = k_ref[0, 0, pl.ds(pl.multiple_of(j * tile, tile), tile), :]
        diag = qi == j
        for b in range(nblk):
            q = q_ref[0, 0, pl.ds(pl.multiple_of(qi * tile + b * rb, rb), rb), :]
            q = q + jnp.where(diag, dbias_ref[b * rb:(b + 1) * rb], jnp.zeros((rb, HEAD_LANES), BF16))
            s_ref[b * rb:(b + 1) * rb] = _dot_nt(q, k)

    def softmax_pv(qi, j, s_ref):
        v = v_ref[0, 0, pl.ds(pl.multiple_of(j * tile, tile), tile), :]
        first = j == 0
        for b in range(nblk):
            rs = slice(b * rb, (b + 1) * rb)
            s = s_ref[rs]
            m_old = jnp.where(first, -1e30, m_ref[rs])
            m_new = jnp.maximum(m_old, jnp.max(s, axis=-1, keepdims=True))
            alpha = jnp.exp2(m_old - m_new)
            p = jnp.exp2(s - jnp.concatenate([m_new] * (tile // LANES), axis=1))
            acc = (acc_ref[rs] * jnp.concatenate([alpha] * (HEAD_LANES // LANES), axis=1)
                   + _dot(p.astype(BF16), v))
            acc_ref[rs] = acc
            m_ref[rs] = m_new
            o_ref[0, pl.ds(pl.multiple_of(qi * tile + b * rb, rb), rb), :] = (
                acc[:, :V_DIM] * (1.0 / acc[:, V_DIM:]))

    def step(qi, j, s_cur, s_next):
        wrap = j == qi
        qn = jnp.minimum(jnp.where(wrap, qi + 1, qi), nq - 1)
        jn = jnp.where(wrap, 0, j + 1)
        scores(qn, jn, s_next)
        softmax_pv(qi, j, s_cur)
        return qn, jn

    def two_steps(_, carry):
        return step(*step(*carry, s0_ref, s1_ref), s1_ref, s0_ref)

    scores(0, 0, s0_ref)
    n_steps = nq * (nq + 1) // 2
    n_pairs = n_steps // 2
    carry = lax.fori_loop(0, n_pairs, two_steps, (jnp.int32(0), jnp.int32(0)), unroll=2 if n_pairs % 2 == 0 else 1)
    if n_steps % 2:
        step(*carry, s0_ref, s1_ref)


def _flat_attend(q, k, v, *, tile, name):
    bx, nh, tx, _ = q.shape
    assert tx % tile == 0 and tile % FLAT_BLOCK_ROWS == 0
    head = pl.BlockSpec((1, 1, tx, HEAD_LANES), lambda b, hh: (b, hh, 0, 0))
    return pl.pallas_call(
        functools.partial(_flat_attn_body, tile=tile, nq=tx // tile, rb=FLAT_BLOCK_ROWS),
        grid=(bx, nh),
        in_specs=[head, head, head],
        out_specs=pl.BlockSpec((1, tx, V_DIM), lambda b, hh: (b, 0, hh)),
        out_shape=jax.ShapeDtypeStruct((bx, tx, nh * V_DIM), F32),
        scratch_shapes=[pltpu.VMEM((tile, LANES), F32), pltpu.VMEM((tile, HEAD_LANES), F32),
                        pltpu.VMEM((tile, tile), F32), pltpu.VMEM((tile, tile), F32),
                        pltpu.VMEM((tile, HEAD_LANES), BF16)],
        compiler_params=pltpu.CompilerParams(
            dimension_semantics=("arbitrary", "arbitrary"), vmem_limit_bytes=VMEM_LIMIT_BYTES),
        name=name,
    )(q, k, v)


def _out_body(o_ref, gate_ref, x_ref, wo_ref, gpost_ref, xo_ref):
    y = _dot((o_ref[0] * _silu(gate_ref[0])).astype(BF16), wo_ref[...])
    xo_ref[0] = x_ref[0] + _rms(y, gpost_ref[...])


def _odd_output(o_heads, gate, x, lyr, o, p, *, tt, name):
    bx, tx, _ = x.shape
    assert tx % tt == 0
    row = lambda b, t: (b, t, 0)
    return pl.pallas_call(
        _out_body,
        grid=(bx, tx // tt),
        in_specs=[pl.BlockSpec((1, tt, C_MIX), row), pl.BlockSpec((1, tt, C_MIX), row),
                  pl.BlockSpec((1, tt, D_MODEL), row),
                  _resident((None, C_MIX, D_MODEL), lambda b, t: (o, 0, 0)),
                  _resident((None, 1, D_MODEL), lambda b, t: (lyr, 0, 0))],
        out_specs=pl.BlockSpec((1, tt, D_MODEL), row),
        out_shape=jax.ShapeDtypeStruct((bx, tx, D_MODEL), F32),
        compiler_params=pltpu.CompilerParams(
            dimension_semantics=("arbitrary", "arbitrary"), vmem_limit_bytes=VMEM_LIMIT_BYTES),
        name=name,
    )(o_heads, gate, x, p["wo"], p["norm_post"])


def _odd_project(x, cos, sin, lyr, o, p, *, nb, tt, name):
    bx, tx, _ = x.shape
    assert bx % nb == 0 and tx % tt == 0 and tt % 16 == 0
    grid = (bx // nb, tx // tt)
    row = lambda b, t: (b, t, 0)
    hrow = lambda b, t: (b, 0, t, 0)
    in_specs = [
        pl.BlockSpec((nb, tt, D_MODEL), row),
        _resident((None, 1, D_MODEL), lambda b, t: (lyr, 0, 0)),
        _resident((None, D_MODEL, ODD_COLS), lambda b, t: (o, 0, 0)),
        _resident((None, 1, Q_LORA), lambda b, t: (o, 0, 0)),
        _resident((None, 1, KV_LORA), lambda b, t: (o, 0, 0)),
        _resident((None, Q_LORA, QUP_COLS), lambda b, t: (o, 0, 0)),
        _resident((None, N_HEADS, QK_NOPE, KV_LORA), lambda b, t: (o, 0, 0, 0)),
        pl.BlockSpec((nb * tt, LANES), lambda b, t: (t, 0)),
        pl.BlockSpec((nb * tt, LANES), lambda b, t: (t, 0)),
    ]
    out_shape = [
        jax.ShapeDtypeStruct((bx, N_HEADS, tx, KV_LORA), BF16),
        jax.ShapeDtypeStruct((bx, N_HEADS, tx, QK_ROPE), BF16),
        jax.ShapeDtypeStruct((bx, tx, KV_LORA), F32),
        jax.ShapeDtypeStruct((bx, tx, QK_ROPE), F32),
        jax.ShapeDtypeStruct((bx, tx, KV_LORA), BF16),
        jax.ShapeDtypeStruct((bx, tx, QK_ROPE), BF16),
        jax.ShapeDtypeStruct((bx, tx, C_MIX), F32),
    ]
    out_specs = [
        pl.BlockSpec((nb, N_HEADS, tt, KV_LORA), hrow),
        pl.BlockSpec((nb, N_HEADS, tt, QK_ROPE), hrow),
        pl.BlockSpec((nb, tt, KV_LORA), row),
        pl.BlockSpec((nb, tt, QK_ROPE), row),
        pl.BlockSpec((nb, tt, KV_LORA), row),
        pl.BlockSpec((nb, tt, QK_ROPE), row),
        pl.BlockSpec((nb, tt, C_MIX), row),
    ]
    return pl.pallas_call(
        functools.partial(_proj_body, nb=nb, tt=tt),
        grid=grid,
        in_specs=in_specs,
        out_specs=out_specs,
        out_shape=out_shape,
        compiler_params=pltpu.CompilerParams(
            dimension_semantics=("arbitrary", "arbitrary"), vmem_limit_bytes=VMEM_LIMIT_BYTES),
        name=name,
    )(x, p["norm_pre"], p["win_o"], p["qn"], p["kvn"], p["wq"], p["wuk"], cos, sin)


def _attn_body(qa_ref, qr_ref, ckv_ref, kr_ref, gate_ref, x_ref, wuv_ref, wo_ref, gpost_ref, xo_ref,
               m_ref, l_ref, acc_ref, s_ref, *, tq, tk, hb, q_pos0, kv_len):
    qi = pl.program_id(1)
    rb = hb * tq
    m_ref[...] = jnp.full(m_ref.shape, -1e30, F32)
    l_ref[...] = jnp.zeros(l_ref.shape, F32)
    acc_ref[...] = jnp.zeros(acc_ref.shape, F32)

    q0 = q_pos0 + qi * tq
    lim_hi = jnp.minimum(((q0 + tq - 1) // CHUNK + 1) * CHUNK, kv_len)
    n_need = (lim_hi + tk - 1) // tk

    def key_tile(j):
        k0 = pl.multiple_of(j * tk, tk)
        return ckv_ref[0, pl.ds(k0, tk), :], kr_ref[0, pl.ds(k0, tk), :]

    def scores(b, kc, kk):
        qa = qa_ref[0, b * hb:(b + 1) * hb].reshape(rb, KV_LORA)
        qr = qr_ref[0, b * hb:(b + 1) * hb].reshape(rb, QK_ROPE)
        s_ref[b * rb:(b + 1) * rb] = _dot_nt(qa, kc) + _dot_nt(qr, kk)

    def softmax_pv(b, kc, ok):
        rs = slice(b * rb, (b + 1) * rb)
        s = s_ref[rs]
        if ok is not None:
            s = jnp.where(ok, s.reshape(hb, tq, tk), -jnp.inf).reshape(rb, tk)
        m_old = m_ref[rs]
        m_new = jnp.maximum(m_old, jnp.max(s, axis=-1, keepdims=True))
        alpha = jnp.exp2(m_old - m_new)
        p = jnp.exp2(s - jnp.concatenate([m_new] * (tk // LANES), axis=1))
        l_ref[rs] = alpha * l_ref[rs] + jnp.sum(p, axis=-1, keepdims=True)
        acc_ref[rs] = (acc_ref[rs] * jnp.concatenate([alpha] * (KV_LORA // LANES), axis=1)
                       + _dot(p.astype(BF16), kc))
        m_ref[rs] = m_new

    n_blocks = N_HEADS // hb
    kc0, kk0 = key_tile(0)
    for b in range(n_blocks):
        scores(b, kc0, kk0)

    def pipelined_step(j, carry):
        kc, _ = key_tile(j)
        kc_next, kk_next = key_tile(j + 1)
        for b in range(n_blocks):
            softmax_pv(b, kc, None)
            scores(b, kc_next, kk_next)
        return carry

    lax.fori_loop(0, n_need - 1, pipelined_step, 0)

    last = n_need - 1
    kc, _ = key_tile(last)
    kpos = last * tk + lax.broadcasted_iota(jnp.int32, (1, tq, tk), 2)
    qpos = q0 + lax.broadcasted_iota(jnp.int32, (1, tq, tk), 1)
    ok = jnp.logical_and(kpos // CHUNK <= qpos // CHUNK, kpos < kv_len)
    for b in range(n_blocks):
        softmax_pv(b, kc, ok)

    o_lat = (acc_ref[...] * jnp.concatenate([1.0 / l_ref[...]] * (KV_LORA // LANES), axis=1)).astype(BF16)
    o = jnp.concatenate([_dot(o_lat[hh * tq:(hh + 1) * tq], wuv_ref[hh]) for hh in range(N_HEADS)], axis=1)
    y = _dot((o * _silu(gate_ref[0])).astype(BF16), wo_ref[...])
    xo_ref[0] = x_ref[0] + _rms(y, gpost_ref[...])


def _odd_attend(qa, qr, ckvb, krb, gate, x, lyr, o, p, *, tq, tk, q_pos0, kv_len, name):
    bx, tx, _ = x.shape
    tkv = ckvb.shape[1]
    assert tx % tq == 0 and tkv % tk == 0 and tk % LANES == 0 and tq % 16 == 0 and kv_len <= tkv
    hb = max(1, min(N_HEADS, ATTN_BLOCK_ROWS // tq))
    assert N_HEADS % hb == 0
    assert tkv == tk or (tk % tq == 0 and q_pos0 % tq == 0 and tq % CHUNK == 0)
    rows = N_HEADS * tq
    grid = (bx, tx // tq)
    in_specs = [
        pl.BlockSpec((1, N_HEADS, tq, KV_LORA), lambda b, i: (b, 0, i, 0)),
        pl.BlockSpec((1, N_HEADS, tq, QK_ROPE), lambda b, i: (b, 0, i, 0)),
        pl.BlockSpec((1, tkv, KV_LORA), lambda b, i: (b, 0, 0)),
        pl.BlockSpec((1, tkv, QK_ROPE), lambda b, i: (b, 0, 0)),
        pl.BlockSpec((1, tq, C_MIX), lambda b, i: (b, i, 0)),
        pl.BlockSpec((1, tq, D_MODEL), lambda b, i: (b, i, 0)),
        _resident((None, N_HEADS, KV_LORA, V_DIM), lambda b, i: (o, 0, 0, 0)),
        _resident((None, C_MIX, D_MODEL), lambda b, i: (o, 0, 0)),
        _resident((None, 1, D_MODEL), lambda b, i: (lyr, 0, 0)),
    ]
    return pl.pallas_call(
        functools.partial(_attn_body, tq=tq, tk=tk, hb=hb, q_pos0=q_pos0, kv_len=kv_len),
        grid=grid,
        in_specs=in_specs,
        out_specs=pl.BlockSpec((1, tq, D_MODEL), lambda b, i: (b, i, 0)),
        out_shape=jax.ShapeDtypeStruct((bx, tx, D_MODEL), F32),
        scratch_shapes=[pltpu.VMEM((rows, LANES), F32), pltpu.VMEM((rows, LANES), F32),
                        pltpu.VMEM((rows, KV_LORA), F32), pltpu.VMEM((rows, tk), F32)],
        compiler_params=pltpu.CompilerParams(
            dimension_semantics=("arbitrary", "arbitrary"), vmem_limit_bytes=VMEM_LIMIT_BYTES),
        name=name,
    )(qa, qr, ckvb, krb, gate, x, p["wuv"], p["wo"], p["norm_post"])


def _rope_tables(pos):
    half = QK_ROPE // 2
    freqs = ROPE_BASE ** (-jnp.arange(half, dtype=F32) / half)
    ang = pos.astype(F32)[:, None] * freqs[None, :]
    c, s = jnp.cos(ang), jnp.sin(ang)
    reps = LANES // QK_ROPE
    return jnp.tile(jnp.concatenate([c, c], axis=1), (1, reps)), jnp.tile(jnp.concatenate([-s, s], axis=1), (1, reps))


def _half_swap(w):
    half = QK_ROPE // 2
    return jnp.concatenate([w[..., half:], w[..., :half]], axis=-1)


def _prepare_params(norm_pre, norm_post, w_in_even, w_pool, pool_scale, sgu_ln_g, sgu_ln_b, w_spatial, b_spatial,
                    w_out_even, w_in_odd, q_norm, kv_norm, w_q_up, w_kv_up, w_o):
    n_odd = w_in_odd.shape[0]
    kr_w = w_in_odd[..., Q_LORA + KV_LORA:Q_LORA + KV_LORA + QK_ROPE]
    win_o = jnp.concatenate([w_in_odd[..., :Q_LORA + KV_LORA], w_in_odd[..., Q_LORA + KV_LORA + QK_ROPE:],
                             kr_w, _half_swap(kr_w)], axis=-1)
    wq_r = w_q_up[..., QK_NOPE:]
    wq = jnp.concatenate([w_q_up[..., :QK_NOPE].reshape(n_odd, Q_LORA, N_HEADS * QK_NOPE),
                          wq_r.reshape(n_odd, Q_LORA, N_HEADS * QK_ROPE),
                          _half_swap(wq_r).reshape(n_odd, Q_LORA, N_HEADS * QK_ROPE)], axis=-1)
    return {
        "norm_pre": norm_pre[:, None, :],
        "norm_post": norm_post[:, None, :],
        "win_e": w_in_even.astype(BF16),
        "wpool": w_pool.astype(BF16),
        "pscale": pool_scale[:, None, :],
        "lng": sgu_ln_g[:, None, :],
        "lnb": sgu_ln_b[:, None, :],
        "ws": w_spatial,
        "bst": jnp.swapaxes(b_spatial, 1, 2),
        "wout_e": w_out_even.astype(BF16),
        "win_o": win_o.astype(BF16),
        "qn": q_norm[:, None, :],
        "kvn": kv_norm[:, None, :],
        "wq": wq.astype(BF16),
        "wuk": jnp.transpose(w_kv_up[..., :QK_NOPE], (0, 2, 3, 1)).astype(BF16),
        "wuv": jnp.transpose(w_kv_up[..., QK_NOPE:], (0, 2, 1, 3)).astype(BF16),
        "wukv": jnp.concatenate([w_kv_up[..., :QK_NOPE].reshape(n_odd, KV_LORA, N_HEADS * QK_NOPE),
                                 w_kv_up[..., QK_NOPE:].reshape(n_odd, KV_LORA, C_MIX)], axis=-1).astype(BF16),
        "wo": w_o.astype(BF16),
    }


def _tile_sizes(t_prompt):
    tt_even = min(t_prompt, 256)
    tt_proj = min(t_prompt, 256)
    tile = min(t_prompt, 1024)
    tt_out = min(t_prompt, 512)
    return tt_even, tt_proj, tile, tt_out


def kernel(x_prompt, x_sample, cache_pool, cache_ckv, cache_krope, norm_pre, norm_post, w_in_even, w_pool, pool_scale, sgu_ln_g, sgu_ln_b, w_spatial, b_spatial, w_out_even, w_in_odd, q_norm, kv_norm, w_q_up, w_kv_up, w_o):
    depth = norm_pre.shape[0]
    b, t, _ = x_prompt.shape
    db, s, _ = x_sample.shape
    past = cache_ckv.shape[2]
    p = _prepare_params(norm_pre, norm_post, w_in_even, w_pool, pool_scale, sgu_ln_g, sgu_ln_b, w_spatial,
                        b_spatial, w_out_even, w_in_odd, q_norm, kv_norm, w_q_up, w_kv_up, w_o)
    tt_even, tt_proj, tile, tt_out = _tile_sizes(t)

    cos_p, sin_p = _rope_tables(jnp.arange(t))
    cs_p = jnp.concatenate([cos_p[:, :QK_ROPE], sin_p[:, :QK_ROPE]], axis=1)
    cos_s, sin_s = _rope_tables(past + jnp.arange(s))
    cos_s, sin_s = jnp.tile(cos_s, (db, 1)), jnp.tile(sin_s, (db, 1))
    kv_len_s = past + s
    tk_s = -(-kv_len_s // LANES) * LANES
    pad_s = tk_s - kv_len_s

    xp, xs = x_prompt, x_sample
    pool_p, pool_s, sgu_s = [], [], []
    ckv_p, kr_p, ckv_s, kr_s = [], [], [], []
    zero_hist = jnp.zeros((b, HIST_ROWS, W_A), F32)
    for lyr in range(depth):
        if lyr % 2 == 0:
            e = lyr // 2
            xp, hp = _even_layer(xp, zero_hist, lyr, e, p, nb=1, tt=tt_even, pos0=0, emit_v=False,
                                 name=f"even{e}_prompt")
            hist_s = jnp.pad(cache_pool[e], ((0, 0), (HIST_ROWS - POOL_HIST, 0), (0, 0)))
            xs, hs, vs = _even_layer(xs, hist_s, lyr, e, p, nb=db, tt=s, pos0=past, emit_v=True,
                                     name=f"even{e}_sample")
            pool_p.append(hp[:, HIST_ROWS - POOL_HIST:])
            pool_s.append(hs[:, HIST_ROWS - POOL_HIST:])
            sgu_s.append(vs)
        else:
            o = lyr // 2
            qh, kh, vh, ckv, kr, gate = _odd_project_heads(xp, cos_p, sin_p, cs_p, lyr, o, p, tt=tt_proj, tile=tile,
                                                           name=f"proj{o}_prompt")
            o_heads = _flat_attend(qh, kh, vh, tile=tile, name=f"attn{o}_prompt")
            xp = _odd_output(o_heads, gate, xp, lyr, o, p, tt=tt_out, name=f"out{o}_prompt")
            ckv_p.append(ckv)
            kr_p.append(kr)

            qa, qr, ckv, kr, ckvb, krb, gate = _odd_project(xs, cos_s, sin_s, lyr, o, p, nb=db, tt=s,
                                                            name=f"proj{o}_sample")
            ckv_all = jnp.concatenate([cache_ckv[o].astype(BF16), ckvb, jnp.zeros((db, pad_s, KV_LORA), BF16)], axis=1)
            kr_all = jnp.concatenate([cache_krope[o].astype(BF16), krb, jnp.zeros((db, pad_s, QK_ROPE), BF16)], axis=1)
            xs = _odd_attend(qa, qr, ckv_all, kr_all, gate, xs, lyr, o, p, tq=s, tk=tk_s, q_pos0=past,
                             kv_len=kv_len_s, name=f"attn{o}_sample")
            ckv_s.append(ckv)
            kr_s.append(kr)
    return (xp, xs, jnp.stack(pool_p), jnp.stack(pool_s), jnp.stack(sgu_s),
            jnp.stack(ckv_p), jnp.stack(kr_p), jnp.stack(ckv_s), jnp.stack(kr_s))
```

```python
import functools

import jax
import jax.numpy as jnp
import numpy as np
from jax import lax
from jax.experimental import pallas as pl
from jax.experimental.pallas import tpu as pltpu

D_MODEL = 1024
CHUNK = 64
EPS = 1e-6
POOL_WINDOWS = (2, 4, 8, 16)
POOL_HIST = max(POOL_WINDOWS) - 1
W_A = D_MODEL
POOL_GDIM = W_A // len(POOL_WINDOWS)
W_B = D_MODEL
SGU_HEADS = 4
SGU_HDIM = W_B // SGU_HEADS
SGU_CHUNK = 128
EVEN_MIX = W_A + W_B
EVEN_IN = W_A + 2 * W_B + EVEN_MIX
N_HEADS = D_MODEL // 128
QK_NOPE = 128
QK_ROPE = 64
V_DIM = 128
KV_LORA = D_MODEL // 4
Q_LORA = 3 * D_MODEL // 8
C_MIX = N_HEADS * V_DIM
ROPE_BASE = 10000.0
ATTN_SCALE = (QK_NOPE + QK_ROPE) ** -0.5
LOG2E = 1.4426950408889634

LANES = 128
HIST_ROWS = 16
VMEM_LIMIT_BYTES = 56 * 1024 * 1024
ATTN_BLOCK_ROWS = 256
FLAT_BLOCK_ROWS = 512
HEAD_LANES = 2 * LANES
BIAS_LANE0 = LANES + QK_ROPE
MASK_BIAS = -1e30

ODD_Q0, ODD_KV0, ODD_G0, ODD_KR0 = 0, Q_LORA, Q_LORA + KV_LORA, Q_LORA + KV_LORA + C_MIX
ODD_COLS = ODD_KR0 + 2 * QK_ROPE
QUP_R0 = N_HEADS * QK_NOPE
QUP_S0 = QUP_R0 + N_HEADS * QK_ROPE
QUP_COLS = QUP_S0 + N_HEADS * QK_ROPE

F32 = jnp.float32
BF16 = jnp.bfloat16


def _dot(a, b):
    return jnp.dot(a, b, preferred_element_type=F32)


def _dot_nt(a, b):
    return lax.dot_general(a, b, (((1,), (1,)), ((), ())), preferred_element_type=F32)


def _rms(x, g):
    return x * lax.rsqrt(jnp.mean(x * x, axis=-1, keepdims=True) + EPS) * g


def _silu(x):
    return x * (1.0 / (1.0 + jnp.exp(-x)))


def _gelu(x):
    return 0.5 * x * (1.0 + lax.erf(x * (2.0 ** -0.5)))


def _even_body(x_ref, hist_ref, gpre_ref, gpost_ref, win_ref, wpool_ref, pscale_ref, lng_ref, lnb_ref,
               ws_ref, bst_ref, wout_ref, xo_ref, histo_ref, *rest, nb, tt, seg, pos0, emit_v, nsub):
    if emit_v:
        vo_ref, aext_ref, mix_ref = rest
    else:
        aext_ref, mix_ref = rest
    t = pl.program_id(1)
    nt = pl.num_programs(1)
    m = nb * tt
    rsz = m // nsub
    rps = tt // nsub if nb == 1 else tt
    st = [dict() for _ in range(nsub)]

    @pl.when(t == 0)
    def _():
        aext_ref[:, 0:HIST_ROWS, :] = hist_ref[...]

    def rows(i):
        return slice(i * rsz, (i + 1) * rsz)

    def stage_a(i):
        x = x_ref[0, rows(i), :] if nb == 1 else x_ref[...].reshape(m, D_MODEL)
        h = _rms(x, gpre_ref[...]).astype(BF16)
        a = _dot(h, win_ref[:, 0:W_A])
        if nb == 1:
            aext_ref[0, HIST_ROWS + i * rsz:HIST_ROWS + (i + 1) * rsz, :] = a
        else:
            aext_ref[:, HIST_ROWS:HIST_ROWS + tt, :] = a.reshape(nb, tt, W_A)
        st[i]["h"] = h

    def stage_uv(i):
        st[i]["uv"] = _dot(st[i]["h"], win_ref[:, W_A:W_A + 2 * W_B])

    def stage_gate(i):
        st[i]["gate"] = _dot(st[i].pop("h"), win_ref[:, W_A + 2 * W_B:])

    def stage_pool(i):
        r0 = HIST_ROWS + (i * rsz if nb == 1 else 0)
        pos = pos0 + t * tt + (r0 - HIST_ROWS) + lax.broadcasted_iota(jnp.int32, (rps, 1), 0)
        for g, w in enumerate(POOL_WINDOWS):
            cs = slice(g * POOL_GDIM, (g + 1) * POOL_GDIM)
            inv_cnt = 1.0 / jnp.minimum(pos + 1, w).astype(F32)
            for s in range(nb):
                cur = aext_ref[s, r0:r0 + rps, cs]
                wsum = cur
                for k in range(1, w):
                    wsum = wsum + aext_ref[s, r0 - k:r0 - k + rps, cs]
                mix_ref[i * rsz + s * rps:i * rsz + (s + 1) * rps, cs] = wsum * inv_cnt - cur
            y_a = _dot(mix_ref[rows(i), cs].astype(BF16), wpool_ref[g]) * pscale_ref[:, cs]
            mix_ref[rows(i), cs] = y_a

    def stage_sgu(i):
        uv = _gelu(st[i].pop("uv"))
        u = uv[:, :W_B]
        v = uv[:, W_B:]
        vc = v - jnp.mean(v, axis=-1, keepdims=True)
        vn = vc * lax.rsqrt(jnp.mean(vc * vc, axis=-1, keepdims=True) + EPS) * lng_ref[...] + lnb_ref[...]
        if emit_v:
            vo_ref[...] = vn.reshape(nb, tt, W_B)
        vb = vn.astype(BF16)
        ii = lax.broadcasted_iota(jnp.int32, (seg, seg), 0)
        jj = lax.broadcasted_iota(jnp.int32, (seg, seg), 1)
        causal = (jj // CHUNK) <= (ii // CHUNK)
        for g in range(SGU_HEADS):
            hs = slice(g * SGU_HDIM, (g + 1) * SGU_HDIM)
            wsg = jnp.where(causal, ws_ref[g, 0:seg, 0:seg], 0.0).astype(BF16)
            bcol = bst_ref[0:seg, g:g + 1]
            for c in range(rsz // seg):
                rs = slice(c * seg, (c + 1) * seg)
                mixed = _dot(wsg, vb[rs, hs]) + bcol
                mix_ref[i * rsz + c * seg:i * rsz + (c + 1) * seg, W_A + g * SGU_HDIM:W_A + (g + 1) * SGU_HDIM] = (
                    u[rs, hs] * mixed)

    def stage_out(i):
        mixv = (mix_ref[rows(i)] * _silu(st[i].pop("gate"))).astype(BF16)
        y = _rms(_dot(mixv, wout_ref[...]), gpost_ref[...])
        if nb == 1:
            xo_ref[0, rows(i), :] = x_ref[0, rows(i), :] + y
        else:
            xo_ref[...] = x_ref[...] + y.reshape(nb, tt, D_MODEL)

    if nsub == 1:
        order = [(stage_a, 0), (stage_uv, 0), (stage_pool, 0), (stage_gate, 0), (stage_sgu, 0), (stage_out, 0)]
    else:
        order = [(stage_a, 0), (stage_uv, 0), (stage_a, 1), (stage_sgu, 0), (stage_uv, 1), (stage_pool, 0),
                 (stage_gate, 0), (stage_sgu, 1), (stage_pool, 1), (stage_out, 0), (stage_gate, 1), (stage_out, 1)]
    for stage, i in order:
        stage(i)

    tail = aext_ref[:, tt:tt + HIST_ROWS, :]

    @pl.when(t == nt - 1)
    def _():
        histo_ref[...] = tail

    aext_ref[:, 0:HIST_ROWS, :] = tail


def _resident(block_shape, index_map):
    return pl.BlockSpec(block_shape, index_map, pipeline_mode=pl.Buffered(1))


def _even_layer(x, hist, lyr, e, p, *, nb, tt, pos0, emit_v, name):
    bx, tx, _ = x.shape
    seg = min(tx, SGU_CHUNK)
    assert bx % nb == 0 and tx % tt == 0 and tt % seg == 0 and tt >= HIST_ROWS and tt % 8 == 0
    grid = (bx // nb, tx // tt)
    row = lambda b, t: (b, t, 0)
    first = lambda b, t: (b, 0, 0)
    in_specs = [
        pl.BlockSpec((nb, tt, D_MODEL), row),
        pl.BlockSpec((nb, HIST_ROWS, W_A), first),
        _resident((None, 1, D_MODEL), lambda b, t: (lyr, 0, 0)),
        _resident((None, 1, D_MODEL), lambda b, t: (lyr, 0, 0)),
        _resident((None, D_MODEL, EVEN_IN), lambda b, t: (e, 0, 0)),
        _resident((None, len(POOL_WINDOWS), POOL_GDIM, POOL_GDIM), lambda b, t: (e, 0, 0, 0)),
        _resident((None, 1, W_A), lambda b, t: (e, 0, 0)),
        _resident((None, 1, W_B), lambda b, t: (e, 0, 0)),
        _resident((None, 1, W_B), lambda b, t: (e, 0, 0)),
        _resident((None, SGU_HEADS, SGU_CHUNK, SGU_CHUNK), lambda b, t: (e, 0, 0, 0)),
        _resident((None, SGU_CHUNK, SGU_HEADS), lambda b, t: (e, 0, 0)),
        _resident((None, EVEN_MIX, D_MODEL), lambda b, t: (e, 0, 0)),
    ]
    out_shape = [jax.ShapeDtypeStruct((bx, tx, D_MODEL), F32), jax.ShapeDtypeStruct((bx, HIST_ROWS, W_A), F32)]
    out_specs = [pl.BlockSpec((nb, tt, D_MODEL), row), pl.BlockSpec((nb, HIST_ROWS, W_A), first)]
    if emit_v:
        out_shape.append(jax.ShapeDtypeStruct((bx, tx, W_B), F32))
        out_specs.append(pl.BlockSpec((nb, tt, W_B), row))
    nsub = 2 if nb == 1 and tt % (2 * seg) == 0 else 1
    body = functools.partial(_even_body, nb=nb, tt=tt, seg=seg, pos0=pos0, emit_v=emit_v, nsub=nsub)
    return pl.pallas_call(
        body,
        grid=grid,
        in_specs=in_specs,
        out_specs=out_specs,
        out_shape=out_shape,
        scratch_shapes=[pltpu.VMEM((nb, HIST_ROWS + tt, W_A), F32), pltpu.VMEM((nb * tt, EVEN_MIX), F32)],
        compiler_params=pltpu.CompilerParams(
            dimension_semantics=("arbitrary", "arbitrary"), vmem_limit_bytes=VMEM_LIMIT_BYTES),
        name=name,
    )(x, hist, p["norm_pre"], p["norm_post"], p["win_e"], p["wpool"], p["pscale"], p["lng"], p["lnb"],
      p["ws"], p["bst"], p["wout_e"])


def _proj_body(x_ref, gpre_ref, win_ref, qn_ref, kvn_ref, wq_ref, wuk_ref, cos_ref, sin_ref,
               qa_ref, qr_ref, ckv_ref, kr_ref, ckvb_ref, krb_ref, gate_ref, *, nb, tt):
    m = nb * tt
    x = x_ref[...].reshape(m, D_MODEL)
    h = _rms(x, gpre_ref[...]).astype(BF16)
    z = _dot(h, win_ref[...])
    gate_ref[...] = z[:, ODD_G0:ODD_KR0].reshape(nb, tt, C_MIX)

    ckv = _rms(z[:, ODD_KV0:ODD_G0], kvn_ref[...])
    ckv_ref[...] = ckv.reshape(nb, tt, KV_LORA)
    ckvb_ref[...] = ckv.astype(BF16).reshape(nb, tt, KV_LORA)

    cos = cos_ref[...]
    sin = sin_ref[...]
    kr = (z[:, ODD_KR0:ODD_KR0 + QK_ROPE] * cos[:, :QK_ROPE]
          + z[:, ODD_KR0 + QK_ROPE:ODD_COLS] * sin[:, :QK_ROPE])
    kr_ref[...] = kr.reshape(nb, tt, QK_ROPE)
    krb_ref[...] = kr.astype(BF16).reshape(nb, tt, QK_ROPE)

    qcn = _rms(z[:, ODD_Q0:ODD_KV0], qn_ref[...]).astype(BF16)
    q = _dot(qcn, wq_ref[...])
    c_exp = ATTN_SCALE * LOG2E
    heads_per_vreg = LANES // QK_ROPE
    for c in range(N_HEADS // heads_per_vreg):
        r = ((q[:, QUP_R0 + c * LANES:QUP_R0 + (c + 1) * LANES] * cos
              + q[:, QUP_S0 + c * LANES:QUP_S0 + (c + 1) * LANES] * sin) * c_exp).astype(BF16)
        for i in range(heads_per_vreg):
            qr_ref[:, heads_per_vreg * c + i, :, :] = r[:, i * QK_ROPE:(i + 1) * QK_ROPE].reshape(nb, tt, QK_ROPE)
    for hh in range(N_HEADS):
        q_nope = q[:, hh * QK_NOPE:(hh + 1) * QK_NOPE].astype(BF16)
        qa_ref[:, hh, :, :] = (_dot(q_nope, wuk_ref[hh]) * c_exp).astype(BF16).reshape(nb, tt, KV_LORA)


def _proj_heads_body(x_ref, gpre_ref, win_ref, qn_ref, kvn_ref, wq_ref, wukv_ref, cos_ref, sin_ref, cs_ref,
                     q_ref, k_ref, v_ref, ckv_ref, kr_ref, gate_ref, *, tt, tile):
    x = x_ref[0]
    h = _rms(x, gpre_ref[...]).astype(BF16)
    z = _dot(h, win_ref[...])
    gate_ref[0] = z[:, ODD_G0:ODD_KR0].astype(BF16)
    ckv = _rms(z[:, ODD_KV0:ODD_G0], kvn_ref[...])
    ckv_ref[0] = ckv

    lane = lax.broadcasted_iota(jnp.int32, (tt, LANES), 1)
    low = lane < QK_ROPE
    u = z[:, ODD_KR0:ODD_COLS] * cs_ref[...]
    kr_full = u + pltpu.roll(u, QK_ROPE, axis=1)
    kr_ref[0] = kr_full[:, :QK_ROPE]
    pos = pl.program_id(1) * tt + lax.broadcasted_iota(jnp.int32, (tt, LANES), 0)
    onehot = (lane - QK_ROPE == (pos % tile) // CHUNK).astype(F32)
    k_hi = jnp.where(low, kr_full, onehot).astype(BF16)
    v_hi = jnp.ones((tt, LANES), BF16)

    kv = _dot(ckv.astype(BF16), wukv_ref[...])
    qcn = _rms(z[:, ODD_Q0:ODD_KV0], qn_ref[...]).astype(BF16)
    q = _dot(qcn, wq_ref[...])
    c_exp = ATTN_SCALE * LOG2E
    heads_per_vreg = LANES // QK_ROPE
    for hh in range(N_HEADS):
        c, i = divmod(hh, heads_per_vreg)
        r = (q[:, QUP_R0 + c * LANES:QUP_R0 + (c + 1) * LANES] * cos_ref[...]
             + q[:, QUP_S0 + c * LANES:QUP_S0 + (c + 1) * LANES] * sin_ref[...]) * c_exp
        if i:
            r = pltpu.roll(r, LANES - i * QK_ROPE, axis=1)
        q_ref[0, hh, :, 0:LANES] = (q[:, hh * QK_NOPE:(hh + 1) * QK_NOPE] * c_exp).astype(BF16)
        q_ref[0, hh, :, LANES:2 * LANES] = jnp.where(low, r, 0.0).astype(BF16)
        k_ref[0, hh, :, 0:LANES] = kv[:, hh * QK_NOPE:(hh + 1) * QK_NOPE].astype(BF16)
        k_ref[0, hh, :, LANES:2 * LANES] = k_hi
        v_ref[0, hh, :, 0:LANES] = kv[:, C_MIX + hh * V_DIM:C_MIX + (hh + 1) * V_DIM].astype(BF16)
        v_ref[0, hh, :, LANES:2 * LANES] = v_hi


def _odd_project_heads(x, cos, sin, cs, lyr, o, p, *, tt, tile, name):
    bx, tx, _ = x.shape
    assert tx % tt == 0 and tt % 16 == 0 and tile % CHUNK == 0 and tile // CHUNK <= LANES - QK_ROPE
    grid = (bx, tx // tt)
    row = lambda b, t: (b, t, 0)
    hrow = lambda b, t: (b, 0, t, 0)
    tab = pl.BlockSpec((tt, LANES), lambda b, t: (t, 0))
    in_specs = [
        pl.BlockSpec((1, tt, D_MODEL), row),
        _resident((None, 1, D_MODEL), lambda b, t: (lyr, 0, 0)),
        _resident((None, D_MODEL, ODD_COLS), lambda b, t: (o, 0, 0)),
        _resident((None, 1, Q_LORA), lambda b, t: (o, 0, 0)),
        _resident((None, 1, KV_LORA), lambda b, t: (o, 0, 0)),
        _resident((None, Q_LORA, QUP_COLS), lambda b, t: (o, 0, 0)),
        _resident((None, KV_LORA, 2 * C_MIX), lambda b, t: (o, 0, 0)),
        tab, tab, tab,
    ]
    head_arr = jax.ShapeDtypeStruct((bx, N_HEADS, tx, HEAD_LANES), BF16)
    head_spec = pl.BlockSpec((1, N_HEADS, tt, HEAD_LANES), hrow)
    out_shape = [head_arr, head_arr, head_arr,
                 jax.ShapeDtypeStruct((bx, tx, KV_LORA), F32),
                 jax.ShapeDtypeStruct((bx, tx, QK_ROPE), F32),
                 jax.ShapeDtypeStruct((bx, tx, C_MIX), BF16)]
    out_specs = [head_spec, head_spec, head_spec,
                 pl.BlockSpec((1, tt, KV_LORA), row),
                 pl.BlockSpec((1, tt, QK_ROPE), row),
                 pl.BlockSpec((1, tt, C_MIX), row)]
    return pl.pallas_call(
        functools.partial(_proj_heads_body, tt=tt, tile=tile),
        grid=grid,
        in_specs=in_specs,
        out_specs=out_specs,
        out_shape=out_shape,
        compiler_params=pltpu.CompilerParams(
            dimension_semantics=("arbitrary", "arbitrary"), vmem_limit_bytes=VMEM_LIMIT_BYTES),
        name=name,
    )(x, p["norm_pre"], p["win_o"], p["qn"], p["kvn"], p["wq"], p["wukv"], cos, sin, cs)


def _flat_attn_body(q_ref, k_ref, v_ref, o_ref, m_ref, acc_ref, s0_ref, s1_ref, dbias_ref, *, tile, nq, rb):
    nblk = tile // rb
    row = lax.broadcasted_iota(jnp.int32, (tile, HEAD_LANES), 0)
    idx = lax.broadcasted_iota(jnp.int32, (tile, HEAD_LANES), 1) - BIAS_LANE0
    hidden = jnp.logical_and(jnp.logical_and(idx >= 0, idx < tile // CHUNK), idx > row // CHUNK)
    dbias_ref[...] = jnp.where(hidden, MASK_BIAS, 0.0).astype(BF16)
    acc_ref[...] = jnp.zeros(acc_ref.shape, F32)

    def scores(qi, j, s_ref):
        k = k_ref[0, 0, pl.ds(pl.multiple_of(j * tile, tile), tile), :]
        diag = qi == j
        for b in range(nblk):
            q = q_ref[0, 0, pl.ds(pl.multiple_of(qi * tile + b * rb, rb), rb), :]
            q = q + jnp.where(diag, dbias_ref[b * rb:(b + 1) * rb], jnp.zeros((rb, HEAD_LANES), BF16))
            s_ref[b * rb:(b + 1) * rb] = _dot_nt(q, k)

    def softmax_pv(qi, j, s_ref):
        v = v_ref[0, 0, pl.ds(pl.multiple_of(j * tile, tile), tile), :]
        first = j == 0
        for b in range(nblk):
            rs = slice(b * rb, (b + 1) * rb)
            s = s_ref[rs]
            m_old = jnp.where(first, -1e30, m_ref[rs])
            m_new = jnp.maximum(m_old, jnp.max(s, axis=-1, keepdims=True))
            alpha = jnp.exp2(m_old - m_new)
            p = jnp.exp2(s - jnp.concatenate([m_new] * (tile // LANES), axis=1))
            acc = (acc_ref[rs] * jnp.concatenate([alpha] * (HEAD_LANES // LANES), axis=1)
                   + _dot(p.astype(BF16), v))
            acc_ref[rs] = acc
            m_ref[rs] = m_new
            o_ref[0, pl.ds(pl.multiple_of(qi * tile + b * rb, rb), rb), :] = (
                acc[:, :V_DIM] * (1.0 / acc[:, V_DIM:])).astype(BF16)

    def step(qi, j, s_cur, s_next):
        wrap = j == qi
        qn = jnp.minimum(jnp.where(wrap, qi + 1, qi), nq - 1)
        jn = jnp.where(wrap, 0, j + 1)
        scores(qn, jn, s_next)
        softmax_pv(qi, j, s_cur)
        return qn, jn

    def two_steps(_, carry):
        return step(*step(*carry, s0_ref, s1_ref), s1_ref, s0_ref)

    scores(0, 0, s0_ref)
    n_steps = nq * (nq + 1) // 2
    n_pairs = n_steps // 2
    carry = lax.fori_loop(0, n_pairs, two_steps, (jnp.int32(0), jnp.int32(0)), unroll=2 if n_pairs % 2 == 0 else 1)
    if n_steps % 2:
        step(*carry, s0_ref, s1_ref)


def _flat_attend(q, k, v, *, tile, name):
    bx, nh, tx, _ = q.shape
    assert tx % tile == 0 and tile % FLAT_BLOCK_ROWS == 0
    head = pl.BlockSpec((1, 1, tx, HEAD_LANES), lambda b, hh: (b, hh, 0, 0))
    return pl.pallas_call(
        functools.partial(_flat_attn_body, tile=tile, nq=tx // tile, rb=FLAT_BLOCK_ROWS),
        grid=(bx, nh),
        in_specs=[head, head, head],
        out_specs=pl.BlockSpec((1, tx, V_DIM), lambda b, hh: (b, 0, hh)),
        out_shape=jax.ShapeDtypeStruct((bx, tx, nh * V_DIM), BF16),
        scratch_shapes=[pltpu.VMEM((tile, LANES), F32), pltpu.VMEM((tile, HEAD_LANES), F32),
                        pltpu.VMEM((tile, tile), F32), pltpu.VMEM((tile, tile), F32),
                        pltpu.VMEM((tile, HEAD_LANES), BF16)],
        compiler_params=pltpu.CompilerParams(
            dimension_semantics=("arbitrary", "arbitrary"), vmem_limit_bytes=VMEM_LIMIT_BYTES),
        name=name,
    )(q, k, v)


def _out_body(o_ref, gate_ref, x_ref, wo_ref, gpost_ref, xo_ref):
    y = _dot((o_ref[0].astype(F32) * _silu(gate_ref[0].astype(F32))).astype(BF16), wo_ref[...])
    xo_ref[0] = x_ref[0] + _rms(y, gpost_ref[...])


def _odd_output(o_heads, gate, x, lyr, o, p, *, tt, name):
    bx, tx, _ = x.shape
    assert tx % tt == 0
    row = lambda b, t: (b, t, 0)
    return pl.pallas_call(
        _out_body,
        grid=(bx, tx // tt),
        in_specs=[pl.BlockSpec((1, tt, C_MIX), row), pl.BlockSpec((1, tt, C_MIX), row),
                  pl.BlockSpec((1, tt, D_MODEL), row),
                  _resident((None, C_MIX, D_MODEL), lambda b, t: (o, 0, 0)),
                  _resident((None, 1, D_MODEL), lambda b, t: (lyr, 0, 0))],
        out_specs=pl.BlockSpec((1, tt, D_MODEL), row),
        out_shape=jax.ShapeDtypeStruct((bx, tx, D_MODEL), F32),
        compiler_params=pltpu.CompilerParams(
            dimension_semantics=("arbitrary", "arbitrary"), vmem_limit_bytes=VMEM_LIMIT_BYTES),
        name=name,
    )(o_heads, gate, x, p["wo"], p["norm_post"])


def _odd_project(x, cos, sin, lyr, o, p, *, nb, tt, name):
    bx, tx, _ = x.shape
    assert bx % nb == 0 and tx % tt == 0 and tt % 16 == 0
    grid = (bx // nb, tx // tt)
    row = lambda b, t: (b, t, 0)
    hrow = lambda b, t: (b, 0, t, 0)
    in_specs = [
        pl.BlockSpec((nb, tt, D_MODEL), row),
        _resident((None, 1, D_MODEL), lambda b, t: (lyr, 0, 0)),
        _resident((None, D_MODEL, ODD_COLS), lambda b, t: (o, 0, 0)),
        _resident((None, 1, Q_LORA), lambda b, t: (o, 0, 0)),
        _resident((None, 1, KV_LORA), lambda b, t: (o, 0, 0)),
        _resident((None, Q_LORA, QUP_COLS), lambda b, t: (o, 0, 0)),
        _resident((None, N_HEADS, QK_NOPE, KV_LORA), lambda b, t: (o, 0, 0, 0)),
        pl.BlockSpec((nb * tt, LANES), lambda b, t: (t, 0)),
        pl.BlockSpec((nb * tt, LANES), lambda b, t: (t, 0)),
    ]
    out_shape = [
        jax.ShapeDtypeStruct((bx, N_HEADS, tx, KV_LORA), BF16),
        jax.ShapeDtypeStruct((bx, N_HEADS, tx, QK_ROPE), BF16),
        jax.ShapeDtypeStruct((bx, tx, KV_LORA), F32),
        jax.ShapeDtypeStruct((bx, tx, QK_ROPE), F32),
        jax.ShapeDtypeStruct((bx, tx, KV_LORA), BF16),
        jax.ShapeDtypeStruct((bx, tx, QK_ROPE), BF16),
        jax.ShapeDtypeStruct((bx, tx, C_MIX), F32),
    ]
    out_specs = [
        pl.BlockSpec((nb, N_HEADS, tt, KV_LORA), hrow),
        pl.BlockSpec((nb, N_HEADS, tt, QK_ROPE), hrow),
        pl.BlockSpec((nb, tt, KV_LORA), row),
        pl.BlockSpec((nb, tt, QK_ROPE), row),
        pl.BlockSpec((nb, tt, KV_LORA), row),
        pl.BlockSpec((nb, tt, QK_ROPE), row),
        pl.BlockSpec((nb, tt, C_MIX), row),
    ]
    return pl.pallas_call(
        functools.partial(_proj_body, nb=nb, tt=tt),
        grid=grid,
        in_specs=in_specs,
        out_specs=out_specs,
        out_shape=out_shape,
        compiler_params=pltpu.CompilerParams(
            dimension_semantics=("arbitrary", "arbitrary"), vmem_limit_bytes=VMEM_LIMIT_BYTES),
        name=name,
    )(x, p["norm_pre"], p["win_o"], p["qn"], p["kvn"], p["wq"], p["wuk"], cos, sin)


def _attn_body(qa_ref, qr_ref, ckv_ref, kr_ref, gate_ref, x_ref, wuv_ref, wo_ref, gpost_ref, xo_ref,
               m_ref, l_ref, acc_ref, s_ref, *, tq, tk, hb, q_pos0, kv_len):
    qi = pl.program_id(1)
    rb = hb * tq
    m_ref[...] = jnp.full(m_ref.shape, -1e30, F32)
    l_ref[...] = jnp.zeros(l_ref.shape, F32)
    acc_ref[...] = jnp.zeros(acc_ref.shape, F32)

    q0 = q_pos0 + qi * tq
    lim_hi = jnp.minimum(((q0 + tq - 1) // CHUNK + 1) * CHUNK, kv_len)
    n_need = (lim_hi + tk - 1) // tk

    def key_tile(j):
        k0 = pl.multiple_of(j * tk, tk)
        return ckv_ref[0, pl.ds(k0, tk), :], kr_ref[0, pl.ds(k0, tk), :]

    def scores(b, kc, kk):
        qa = qa_ref[0, b * hb:(b + 1) * hb].reshape(rb, KV_LORA)
        qr = qr_ref[0, b * hb:(b + 1) * hb].reshape(rb, QK_ROPE)
        s_ref[b * rb:(b + 1) * rb] = _dot_nt(qa, kc) + _dot_nt(qr, kk)

    def softmax_pv(b, kc, ok):
        rs = slice(b * rb, (b + 1) * rb)
        s = s_ref[rs]
        if ok is not None:
            s = jnp.where(ok, s.reshape(hb, tq, tk), -jnp.inf).reshape(rb, tk)
        m_old = m_ref[rs]
        m_new = jnp.maximum(m_old, jnp.max(s, axis=-1, keepdims=True))
        alpha = jnp.exp2(m_old - m_new)
        p = jnp.exp2(s - jnp.concatenate([m_new] * (tk // LANES), axis=1))
        l_ref[rs] = alpha * l_ref[rs] + jnp.sum(p, axis=-1, keepdims=True)
        acc_ref[rs] = (acc_ref[rs] * jnp.concatenate([alpha] * (KV_LORA // LANES), axis=1)
                       + _dot(p.astype(BF16), kc))
        m_ref[rs] = m_new

    n_blocks = N_HEADS // hb
    kc0, kk0 = key_tile(0)
    for b in range(n_blocks):
        scores(b, kc0, kk0)

    def pipelined_step(j, carry):
        kc, _ = key_tile(j)
        kc_next, kk_next = key_tile(j + 1)
        for b in range(n_blocks):
            softmax_pv(b, kc, None)
            scores(b, kc_next, kk_next)
        return carry

    lax.fori_loop(0, n_need - 1, pipelined_step, 0)

    last = n_need - 1
    kc, _ = key_tile(last)
    kpos = last * tk + lax.broadcasted_iota(jnp.int32, (1, tq, tk), 2)
    qpos = q0 + lax.broadcasted_iota(jnp.int32, (1, tq, tk), 1)
    ok = jnp.logical_and(kpos // CHUNK <= qpos // CHUNK, kpos < kv_len)
    for b in range(n_blocks):
        softmax_pv(b, kc, ok)

    o_lat = (acc_ref[...] * jnp.concatenate([1.0 / l_ref[...]] * (KV_LORA // LANES), axis=1)).astype(BF16)
    o = jnp.concatenate([_dot(o_lat[hh * tq:(hh + 1) * tq], wuv_ref[hh]) for hh in range(N_HEADS)], axis=1)
    y = _dot((o * _silu(gate_ref[0])).astype(BF16), wo_ref[...])
    xo_ref[0] = x_ref[0] + _rms(y, gpost_ref[...])


def _odd_attend(qa, qr, ckvb, krb, gate, x, lyr, o, p, *, tq, tk, q_pos0, kv_len, name):
    bx, tx, _ = x.shape
    tkv = ckvb.shape[1]
    assert tx % tq == 0 and tkv % tk == 0 and tk % LANES == 0 and tq % 16 == 0 and kv_len <= tkv
    hb = max(1, min(N_HEADS, ATTN_BLOCK_ROWS // tq))
    assert N_HEADS % hb == 0
    assert tkv == tk or (tk % tq == 0 and q_pos0 % tq == 0 and tq % CHUNK == 0)
    rows = N_HEADS * tq
    grid = (bx, tx // tq)
    in_specs = [
        pl.BlockSpec((1, N_HEADS, tq, KV_LORA), lambda b, i: (b, 0, i, 0)),
        pl.BlockSpec((1, N_HEADS, tq, QK_ROPE), lambda b, i: (b, 0, i, 0)),
        pl.BlockSpec((1, tkv, KV_LORA), lambda b, i: (b, 0, 0)),
        pl.BlockSpec((1, tkv, QK_ROPE), lambda b, i: (b, 0, 0)),
        pl.BlockSpec((1, tq, C_MIX), lambda b, i: (b, i, 0)),
        pl.BlockSpec((1, tq, D_MODEL), lambda b, i: (b, i, 0)),
        _resident((None, N_HEADS, KV_LORA, V_DIM), lambda b, i: (o, 0, 0, 0)),
        _resident((None, C_MIX, D_MODEL), lambda b, i: (o, 0, 0)),
        _resident((None, 1, D_MODEL), lambda b, i: (lyr, 0, 0)),
    ]
    return pl.pallas_call(
        functools.partial(_attn_body, tq=tq, tk=tk, hb=hb, q_pos0=q_pos0, kv_len=kv_len),
        grid=grid,
        in_specs=in_specs,
        out_specs=pl.BlockSpec((1, tq, D_MODEL), lambda b, i: (b, i, 0)),
        out_shape=jax.ShapeDtypeStruct((bx, tx, D_MODEL), F32),
        scratch_shapes=[pltpu.VMEM((rows, LANES), F32), pltpu.VMEM((rows, LANES), F32),
                        pltpu.VMEM((rows, KV_LORA), F32), pltpu.VMEM((rows, tk), F32)],
        compiler_params=pltpu.CompilerParams(
            dimension_semantics=("arbitrary", "arbitrary"), vmem_limit_bytes=VMEM_LIMIT_BYTES),
        name=name,
    )(qa, qr, ckvb, krb, gate, x, p["wuv"], p["wo"], p["norm_post"])


def _rope_tables(pos0, n):
    half = QK_ROPE // 2
    freqs = ROPE_BASE ** (-np.arange(half, dtype=np.float64) / half)
    ang = (pos0 + np.arange(n, dtype=np.float64))[:, None] * freqs[None, :]
    c, s = jnp.asarray(np.cos(ang), F32), jnp.asarray(np.sin(ang), F32)
    reps = LANES // QK_ROPE
    return jnp.tile(jnp.concatenate([c, c], axis=1), (1, reps)), jnp.tile(jnp.concatenate([-s, s], axis=1), (1, reps))


def _half_swap(w):
    half = QK_ROPE // 2
    return jnp.concatenate([w[..., half:], w[..., :half]], axis=-1)


def _prepare_params(norm_pre, norm_post, w_in_even, w_pool, pool_scale, sgu_ln_g, sgu_ln_b, w_spatial, b_spatial,
                    w_out_even, w_in_odd, q_norm, kv_norm, w_q_up, w_kv_up, w_o):
    n_odd = w_in_odd.shape[0]
    kr_w = w_in_odd[..., Q_LORA + KV_LORA:Q_LORA + KV_LORA + QK_ROPE]
    win_o = jnp.concatenate([w_in_odd[..., :Q_LORA + KV_LORA], w_in_odd[..., Q_LORA + KV_LORA + QK_ROPE:],
                             kr_w, _half_swap(kr_w)], axis=-1)
    wq_r = w_q_up[..., QK_NOPE:]
    wq = jnp.concatenate([w_q_up[..., :QK_NOPE].reshape(n_odd, Q_LORA, N_HEADS * QK_NOPE),
                          wq_r.reshape(n_odd, Q_LORA, N_HEADS * QK_ROPE),
                          _half_swap(wq_r).reshape(n_odd, Q_LORA, N_HEADS * QK_ROPE)], axis=-1)
    return {
        "norm_pre": norm_pre[:, None, :],
        "norm_post": norm_post[:, None, :],
        "win_e": w_in_even.astype(BF16),
        "wpool": w_pool.astype(BF16),
        "pscale": pool_scale[:, None, :],
        "lng": sgu_ln_g[:, None, :],
        "lnb": sgu_ln_b[:, None, :],
        "ws": w_spatial,
        "bst": jnp.swapaxes(b_spatial, 1, 2),
        "wout_e": w_out_even.astype(BF16),
        "win_o": win_o.astype(BF16),
        "qn": q_norm[:, None, :],
        "kvn": kv_norm[:, None, :],
        "wq": wq.astype(BF16),
        "wuk": jnp.transpose(w_kv_up[..., :QK_NOPE], (0, 2, 3, 1)).astype(BF16),
        "wuv": jnp.transpose(w_kv_up[..., QK_NOPE:], (0, 2, 1, 3)).astype(BF16),
        "wukv": jnp.concatenate([w_kv_up[..., :QK_NOPE].reshape(n_odd, KV_LORA, N_HEADS * QK_NOPE),
                                 w_kv_up[..., QK_NOPE:].reshape(n_odd, KV_LORA, C_MIX)], axis=-1).astype(BF16),
        "wo": w_o.astype(BF16),
    }


def _tile_sizes(t_prompt):
    tt_even = min(t_prompt, 512)
    tt_proj = min(t_prompt, 512)
    tile = min(t_prompt, 1024)
    tt_out = min(t_prompt, 512)
    return tt_even, tt_proj, tile, tt_out


def kernel(x_prompt, x_sample, cache_pool, cache_ckv, cache_krope, norm_pre, norm_post, w_in_even, w_pool, pool_scale, sgu_ln_g, sgu_ln_b, w_spatial, b_spatial, w_out_even, w_in_odd, q_norm, kv_norm, w_q_up, w_kv_up, w_o):
    depth = norm_pre.shape[0]
    b, t, _ = x_prompt.shape
    db, s, _ = x_sample.shape
    past = cache_ckv.shape[2]
    p = _prepare_params(norm_pre, norm_post, w_in_even, w_pool, pool_scale, sgu_ln_g, sgu_ln_b, w_spatial,
                        b_spatial, w_out_even, w_in_odd, q_norm, kv_norm, w_q_up, w_kv_up, w_o)
    tt_even, tt_proj, tile, tt_out = _tile_sizes(t)

    cos_p, sin_p = _rope_tables(0, t)
    cs_p = jnp.concatenate([cos_p[:, :QK_ROPE], sin_p[:, :QK_ROPE]], axis=1)
    cos_s, sin_s = _rope_tables(past, s)
    cos_s, sin_s = jnp.tile(cos_s, (db, 1)), jnp.tile(sin_s, (db, 1))
    kv_len_s = past + s
    tk_s = -(-kv_len_s // LANES) * LANES
    pad_s = tk_s - kv_len_s

    xp, xs = x_prompt, x_sample
    pool_p, pool_s, sgu_s = [], [], []
    ckv_p, kr_p, ckv_s, kr_s = [], [], [], []
    zero_hist = jnp.zeros((b, HIST_ROWS, W_A), F32)
    for lyr in range(depth):
        if lyr % 2 == 0:
            e = lyr // 2
            xp, hp = _even_layer(xp, zero_hist, lyr, e, p, nb=1, tt=tt_even, pos0=0, emit_v=False,
                                 name=f"even{e}_prompt")
            hist_s = jnp.pad(cache_pool[e], ((0, 0), (HIST_ROWS - POOL_HIST, 0), (0, 0)))
            xs, hs, vs = _even_layer(xs, hist_s, lyr, e, p, nb=db, tt=s, pos0=past, emit_v=True,
                                     name=f"even{e}_sample")
            pool_p.append(hp[:, HIST_ROWS - POOL_HIST:])
            pool_s.append(hs[:, HIST_ROWS - POOL_HIST:])
            sgu_s.append(vs)
        else:
            o = lyr // 2
            qh, kh, vh, ckv, kr, gate = _odd_project_heads(xp, cos_p, sin_p, cs_p, lyr, o, p, tt=tt_proj, tile=tile,
                                                           name=f"proj{o}_prompt")
            o_heads = _flat_attend(qh, kh, vh, tile=tile, name=f"attn{o}_prompt")
            xp = _odd_output(o_heads, gate, xp, lyr, o, p, tt=tt_out, name=f"out{o}_prompt")
            ckv_p.append(ckv)
            kr_p.append(kr)

            qa, qr, ckv, kr, ckvb, krb, gate = _odd_project(xs, cos_s, sin_s, lyr, o, p, nb=db, tt=s,
                                                            name=f"proj{o}_sample")
            ckv_all = jnp.concatenate([cache_ckv[o].astype(BF16), ckvb, jnp.zeros((db, pad_s, KV_LORA), BF16)], axis=1)
            kr_all = jnp.concatenate([cache_krope[o].astype(BF16), krb, jnp.zeros((db, pad_s, QK_ROPE), BF16)], axis=1)
            xs = _odd_attend(qa, qr, ckv_all, kr_all, gate, xs, lyr, o, p, tq=s, tk=tk_s, q_pos0=past,
                             kv_len=kv_len_s, name=f"attn{o}_sample")
            ckv_s.append(ckv)
            kr_s.append(kr)
    return (xp, xs, jnp.stack(pool_p), jnp.stack(pool_s), jnp.stack(sgu_s),
            jnp.stack(ckv_p), jnp.stack(kr_p), jnp.stack(ckv_s), jnp.stack(kr_s))
```

```python
import functools

import jax
import jax.numpy as jnp
import numpy as np
from jax import lax
from jax.experimental import pallas as pl
from jax.experimental.pallas import tpu as pltpu

D_MODEL = 1024
CHUNK = 64
EPS = 1e-6
POOL_WINDOWS = (2, 4, 8, 16)
POOL_HIST = max(POOL_WINDOWS) - 1
W_A = D_MODEL
POOL_GDIM = W_A // len(POOL_WINDOWS)
W_B = D_MODEL
SGU_HEADS = 4
SGU_HDIM = W_B // SGU_HEADS
SGU_CHUNK = 128
EVEN_MIX = W_A + W_B
EVEN_IN = W_A + 2 * W_B + EVEN_MIX
N_HEADS = D_MODEL // 128
QK_NOPE = 128
QK_ROPE = 64
V_DIM = 128
KV_LORA = D_MODEL // 4
Q_LORA = 3 * D_MODEL // 8
C_MIX = N_HEADS * V_DIM
ROPE_BASE = 10000.0
ATTN_SCALE = (QK_NOPE + QK_ROPE) ** -0.5
LOG2E = 1.4426950408889634

LANES = 128
HIST_ROWS = 16
VMEM_LIMIT_BYTES = 56 * 1024 * 1024
FLAT_BLOCK_ROWS = 512
HEAD_LANES = 2 * LANES
BIAS_LANE0 = LANES + QK_ROPE
MASK_BIAS = -1e30

ODD_Q0, ODD_KV0, ODD_G0, ODD_KR0 = 0, Q_LORA, Q_LORA + KV_LORA, Q_LORA + KV_LORA + C_MIX
ODD_COLS = ODD_KR0 + 2 * QK_ROPE
QUP_R0 = N_HEADS * QK_NOPE
QUP_S0 = QUP_R0 + N_HEADS * QK_ROPE
QUP_COLS = QUP_S0 + N_HEADS * QK_ROPE

F32 = jnp.float32
BF16 = jnp.bfloat16


def _dot(a, b):
    return jnp.dot(a, b, preferred_element_type=F32)


def _dot_nt(a, b):
    return lax.dot_general(a, b, (((1,), (1,)), ((), ())), preferred_element_type=F32)


def _rms(x, g):
    return x * lax.rsqrt(jnp.mean(x * x, axis=-1, keepdims=True) + EPS) * g


def _silu(x):
    return x * (1.0 / (1.0 + jnp.exp(-x)))


def _gelu(x):
    return 0.5 * x * (1.0 + lax.erf(x * (2.0 ** -0.5)))


def _even_body(x_ref, hist_ref, gpre_ref, gpost_ref, win_ref, wpool_ref, pscale_ref, lng_ref, lnb_ref,
               ws_ref, bst_ref, wout_ref, xo_ref, histo_ref, *rest, nb, tt, seg, pos0, emit_v, nsub):
    if emit_v:
        vo_ref, aext_ref, mix_ref = rest
    else:
        aext_ref, mix_ref = rest
    t = pl.program_id(1)
    nt = pl.num_programs(1)
    m = nb * tt
    rsz = m // nsub
    rps = tt // nsub if nb == 1 else tt
    st = [dict() for _ in range(nsub)]

    @pl.when(t == 0)
    def _():
        aext_ref[:, 0:HIST_ROWS, :] = hist_ref[...]

    def rows(i):
        return slice(i * rsz, (i + 1) * rsz)

    def stage_a(i):
        x = x_ref[0, rows(i), :] if nb == 1 else x_ref[...].reshape(m, D_MODEL)
        h = _rms(x, gpre_ref[...]).astype(BF16)
        a = _dot(h, win_ref[:, 0:W_A])
        if nb == 1:
            aext_ref[0, HIST_ROWS + i * rsz:HIST_ROWS + (i + 1) * rsz, :] = a
        else:
            aext_ref[:, HIST_ROWS:HIST_ROWS + tt, :] = a.reshape(nb, tt, W_A)
        st[i]["h"] = h

    def stage_uv(i):
        st[i]["uv"] = _dot(st[i]["h"], win_ref[:, W_A:W_A + 2 * W_B])

    def stage_gate(i):
        st[i]["gate"] = _dot(st[i].pop("h"), win_ref[:, W_A + 2 * W_B:])

    def stage_pool(i):
        r0 = HIST_ROWS + (i * rsz if nb == 1 else 0)
        pos = pos0 + t * tt + (r0 - HIST_ROWS) + lax.broadcasted_iota(jnp.int32, (rps, 1), 0)
        for g, w in enumerate(POOL_WINDOWS):
            cs = slice(g * POOL_GDIM, (g + 1) * POOL_GDIM)
            inv_cnt = 1.0 / jnp.minimum(pos + 1, w).astype(F32)
            for s in range(nb):
                cur = aext_ref[s, r0:r0 + rps, cs]
                wsum = cur
                for k in range(1, w):
                    wsum = wsum + aext_ref[s, r0 - k:r0 - k + rps, cs]
                mix_ref[i * rsz + s * rps:i * rsz + (s + 1) * rps, cs] = wsum * inv_cnt - cur
            y_a = _dot(mix_ref[rows(i), cs].astype(BF16), wpool_ref[g]) * pscale_ref[:, cs]
            mix_ref[rows(i), cs] = y_a

    def stage_sgu(i):
        uv = _gelu(st[i].pop("uv"))
        u = uv[:, :W_B]
        v = uv[:, W_B:]
        vc = v - jnp.mean(v, axis=-1, keepdims=True)
        vn = vc * lax.rsqrt(jnp.mean(vc * vc, axis=-1, keepdims=True) + EPS) * lng_ref[...] + lnb_ref[...]
        if emit_v:
            vo_ref[...] = vn.reshape(nb, tt, W_B)
        vb = vn.astype(BF16)
        ii = lax.broadcasted_iota(jnp.int32, (seg, seg), 0)
        jj = lax.broadcasted_iota(jnp.int32, (seg, seg), 1)
        causal = (jj // CHUNK) <= (ii // CHUNK)
        for g in range(SGU_HEADS):
            hs = slice(g * SGU_HDIM, (g + 1) * SGU_HDIM)
            wsg = jnp.where(causal, ws_ref[g, 0:seg, 0:seg], 0.0).astype(BF16)
            bcol = bst_ref[0:seg, g:g + 1]
            for c in range(rsz // seg):
                rs = slice(c * seg, (c + 1) * seg)
                mixed = _dot(wsg, vb[rs, hs]) + bcol
                mix_ref[i * rsz + c * seg:i * rsz + (c + 1) * seg, W_A + g * SGU_HDIM:W_A + (g + 1) * SGU_HDIM] = (
                    u[rs, hs] * mixed)

    def stage_out(i):
        mixv = (mix_ref[rows(i)] * _silu(st[i].pop("gate"))).astype(BF16)
        y = _rms(_dot(mixv, wout_ref[...]), gpost_ref[...])
        if nb == 1:
            xo_ref[0, rows(i), :] = x_ref[0, rows(i), :] + y
        else:
            xo_ref[...] = x_ref[...] + y.reshape(nb, tt, D_MODEL)

    if nsub == 1:
        order = [(stage_a, 0), (stage_uv, 0), (stage_pool, 0), (stage_gate, 0), (stage_sgu, 0), (stage_out, 0)]
    else:
        order = [(stage_a, 0), (stage_uv, 0), (stage_a, 1), (stage_sgu, 0), (stage_uv, 1), (stage_pool, 0),
                 (stage_gate, 0), (stage_sgu, 1), (stage_pool, 1), (stage_out, 0), (stage_gate, 1), (stage_out, 1)]
    for stage, i in order:
        stage(i)

    tail = aext_ref[:, tt:tt + HIST_ROWS, :]

    @pl.when(t == nt - 1)
    def _():
        histo_ref[...] = tail

    aext_ref[:, 0:HIST_ROWS, :] = tail


def _resident(block_shape, index_map):
    return pl.BlockSpec(block_shape, index_map, pipeline_mode=pl.Buffered(1))


def _even_layer(x, hist, lyr, e, p, *, nb, tt, pos0, emit_v, name):
    bx, tx, _ = x.shape
    seg = min(tx, SGU_CHUNK)
    assert bx % nb == 0 and tx % tt == 0 and tt % seg == 0 and tt >= HIST_ROWS and tt % 8 == 0
    grid = (bx // nb, tx // tt)
    row = lambda b, t: (b, t, 0)
    first = lambda b, t: (b, 0, 0)
    in_specs = [
        pl.BlockSpec((nb, tt, D_MODEL), row),
        pl.BlockSpec((nb, HIST_ROWS, W_A), first),
        _resident((None, 1, D_MODEL), lambda b, t: (lyr, 0, 0)),
        _resident((None, 1, D_MODEL), lambda b, t: (lyr, 0, 0)),
        _resident((None, D_MODEL, EVEN_IN), lambda b, t: (e, 0, 0)),
        _resident((None, len(POOL_WINDOWS), POOL_GDIM, POOL_GDIM), lambda b, t: (e, 0, 0, 0)),
        _resident((None, 1, W_A), lambda b, t: (e, 0, 0)),
        _resident((None, 1, W_B), lambda b, t: (e, 0, 0)),
        _resident((None, 1, W_B), lambda b, t: (e, 0, 0)),
        _resident((None, SGU_HEADS, SGU_CHUNK, SGU_CHUNK), lambda b, t: (e, 0, 0, 0)),
        _resident((None, SGU_CHUNK, SGU_HEADS), lambda b, t: (e, 0, 0)),
        _resident((None, EVEN_MIX, D_MODEL), lambda b, t: (e, 0, 0)),
    ]
    out_shape = [jax.ShapeDtypeStruct((bx, tx, D_MODEL), F32), jax.ShapeDtypeStruct((bx, HIST_ROWS, W_A), F32)]
    out_specs = [pl.BlockSpec((nb, tt, D_MODEL), row), pl.BlockSpec((nb, HIST_ROWS, W_A), first)]
    if emit_v:
        out_shape.append(jax.ShapeDtypeStruct((bx, tx, W_B), F32))
        out_specs.append(pl.BlockSpec((nb, tt, W_B), row))
    nsub = 2 if nb == 1 and tt % (2 * seg) == 0 else 1
    body = functools.partial(_even_body, nb=nb, tt=tt, seg=seg, pos0=pos0, emit_v=emit_v, nsub=nsub)
    return pl.pallas_call(
        body,
        grid=grid,
        in_specs=in_specs,
        out_specs=out_specs,
        out_shape=out_shape,
        scratch_shapes=[pltpu.VMEM((nb, HIST_ROWS + tt, W_A), F32), pltpu.VMEM((nb * tt, EVEN_MIX), F32)],
        compiler_params=pltpu.CompilerParams(
            dimension_semantics=("arbitrary", "arbitrary"), vmem_limit_bytes=VMEM_LIMIT_BYTES),
        name=name,
    )(x, hist, p["norm_pre"], p["norm_post"], p["win_e"], p["wpool"], p["pscale"], p["lng"], p["lnb"],
      p["ws"], p["bst"], p["wout_e"])


def _proj_body(x_ref, gpre_ref, win_ref, qn_ref, kvn_ref, wq_ref, wuk_ref, cos_ref, sin_ref,
               qa_ref, qr_ref, ckv_ref, kr_ref, ckvb_ref, krb_ref, gate_ref, *, nb, tt):
    m = nb * tt
    x = x_ref[...].reshape(m, D_MODEL)
    h = _rms(x, gpre_ref[...]).astype(BF16)
    z = _dot(h, win_ref[...])
    gate_ref[...] = z[:, ODD_G0:ODD_KR0].reshape(nb, tt, C_MIX)

    ckv = _rms(z[:, ODD_KV0:ODD_G0], kvn_ref[...])
    ckv_ref[...] = ckv.reshape(nb, tt, KV_LORA)
    ckvb_ref[...] = ckv.astype(BF16).reshape(nb, tt, KV_LORA)

    cos = cos_ref[...]
    sin = sin_ref[...]
    kr = (z[:, ODD_KR0:ODD_KR0 + QK_ROPE] * cos[:, :QK_ROPE]
          + z[:, ODD_KR0 + QK_ROPE:ODD_COLS] * sin[:, :QK_ROPE])
    kr_ref[...] = kr.reshape(nb, tt, QK_ROPE)
    krb_ref[...] = kr.astype(BF16).reshape(nb, tt, QK_ROPE)

    qcn = _rms(z[:, ODD_Q0:ODD_KV0], qn_ref[...]).astype(BF16)
    q = _dot(qcn, wq_ref[...])
    c_exp = ATTN_SCALE * LOG2E
    heads_per_vreg = LANES // QK_ROPE
    for c in range(N_HEADS // heads_per_vreg):
        r = ((q[:, QUP_R0 + c * LANES:QUP_R0 + (c + 1) * LANES] * cos
              + q[:, QUP_S0 + c * LANES:QUP_S0 + (c + 1) * LANES] * sin) * c_exp).astype(BF16)
        for i in range(heads_per_vreg):
            qr_ref[:, heads_per_vreg * c + i, :, :] = r[:, i * QK_ROPE:(i + 1) * QK_ROPE].reshape(nb, tt, QK_ROPE)
    for hh in range(N_HEADS):
        q_nope = q[:, hh * QK_NOPE:(hh + 1) * QK_NOPE].astype(BF16)
        qa_ref[:, hh, :, :] = (_dot(q_nope, wuk_ref[hh]) * c_exp).astype(BF16).reshape(nb, tt, KV_LORA)


def _proj_heads_body(x_ref, gpre_ref, win_ref, qn_ref, kvn_ref, wq_ref, wukv_ref, cos_ref, sin_ref, cs_ref,
                     q_ref, k_ref, v_ref, ckv_ref, kr_ref, gate_ref, *, tt, tile):
    x = x_ref[0]
    h = _rms(x, gpre_ref[...]).astype(BF16)
    z = _dot(h, win_ref[...])
    gate_ref[0] = z[:, ODD_G0:ODD_KR0].astype(BF16)
    ckv = _rms(z[:, ODD_KV0:ODD_G0], kvn_ref[...])
    ckv_ref[0] = ckv

    lane = lax.broadcasted_iota(jnp.int32, (tt, LANES), 1)
    low = lane < QK_ROPE
    u = z[:, ODD_KR0:ODD_COLS] * cs_ref[...]
    kr_full = u + pltpu.roll(u, QK_ROPE, axis=1)
    kr_ref[0] = kr_full[:, :QK_ROPE]
    pos = pl.program_id(1) * tt + lax.broadcasted_iota(jnp.int32, (tt, LANES), 0)
    onehot = (lane - QK_ROPE == (pos % tile) // CHUNK).astype(F32)
    k_hi = jnp.where(low, kr_full, onehot).astype(BF16)
    v_hi = jnp.ones((tt, LANES), BF16)

    kv = _dot(ckv.astype(BF16), wukv_ref[...])
    qcn = _rms(z[:, ODD_Q0:ODD_KV0], qn_ref[...]).astype(BF16)
    q = _dot(qcn, wq_ref[...])
    c_exp = ATTN_SCALE * LOG2E
    heads_per_vreg = LANES // QK_ROPE
    for hh in range(N_HEADS):
        c, i = divmod(hh, heads_per_vreg)
        r = (q[:, QUP_R0 + c * LANES:QUP_R0 + (c + 1) * LANES] * cos_ref[...]
             + q[:, QUP_S0 + c * LANES:QUP_S0 + (c + 1) * LANES] * sin_ref[...]) * c_exp
        if i:
            r = pltpu.roll(r, LANES - i * QK_ROPE, axis=1)
        q_ref[0, hh, :, 0:LANES] = (q[:, hh * QK_NOPE:(hh + 1) * QK_NOPE] * c_exp).astype(BF16)
        q_ref[0, hh, :, LANES:2 * LANES] = jnp.where(low, r, 0.0).astype(BF16)
        k_ref[0, hh, :, 0:LANES] = kv[:, hh * QK_NOPE:(hh + 1) * QK_NOPE].astype(BF16)
        k_ref[0, hh, :, LANES:2 * LANES] = k_hi
        v_ref[0, hh, :, 0:LANES] = kv[:, C_MIX + hh * V_DIM:C_MIX + (hh + 1) * V_DIM].astype(BF16)
        v_ref[0, hh, :, LANES:2 * LANES] = v_hi


def _odd_project_heads(x, cos, sin, cs, lyr, o, p, *, tt, tile, name):
    bx, tx, _ = x.shape
    assert tx % tt == 0 and tt % 16 == 0 and tile % CHUNK == 0 and tile // CHUNK <= LANES - QK_ROPE
    grid = (bx, tx // tt)
    row = lambda b, t: (b, t, 0)
    hrow = lambda b, t: (b, 0, t, 0)
    tab = pl.BlockSpec((tt, LANES), lambda b, t: (t, 0))
    in_specs = [
        pl.BlockSpec((1, tt, D_MODEL), row),
        _resident((None, 1, D_MODEL), lambda b, t: (lyr, 0, 0)),
        _resident((None, D_MODEL, ODD_COLS), lambda b, t: (o, 0, 0)),
        _resident((None, 1, Q_LORA), lambda b, t: (o, 0, 0)),
        _resident((None, 1, KV_LORA), lambda b, t: (o, 0, 0)),
        _resident((None, Q_LORA, QUP_COLS), lambda b, t: (o, 0, 0)),
        _resident((None, KV_LORA, 2 * C_MIX), lambda b, t: (o, 0, 0)),
        tab, tab, tab,
    ]
    head_arr = jax.ShapeDtypeStruct((bx, N_HEADS, tx, HEAD_LANES), BF16)
    head_spec = pl.BlockSpec((1, N_HEADS, tt, HEAD_LANES), hrow)
    out_shape = [head_arr, head_arr, head_arr,
                 jax.ShapeDtypeStruct((bx, tx, KV_LORA), F32),
                 jax.ShapeDtypeStruct((bx, tx, QK_ROPE), F32),
                 jax.ShapeDtypeStruct((bx, tx, C_MIX), BF16)]
    out_specs = [head_spec, head_spec, head_spec,
                 pl.BlockSpec((1, tt, KV_LORA), row),
                 pl.BlockSpec((1, tt, QK_ROPE), row),
                 pl.BlockSpec((1, tt, C_MIX), row)]
    return pl.pallas_call(
        functools.partial(_proj_heads_body, tt=tt, tile=tile),
        grid=grid,
        in_specs=in_specs,
        out_specs=out_specs,
        out_shape=out_shape,
        compiler_params=pltpu.CompilerParams(
            dimension_semantics=("arbitrary", "arbitrary"), vmem_limit_bytes=VMEM_LIMIT_BYTES),
        name=name,
    )(x, p["norm_pre"], p["win_o"], p["qn"], p["kvn"], p["wq"], p["wukv"], cos, sin, cs)


def _flat_attn_body(q_ref, k_ref, v_ref, o_ref, m_ref, acc_ref, s0_ref, s1_ref, dbias_ref, *, tile, nq, rb):
    nblk = tile // rb
    row = lax.broadcasted_iota(jnp.int32, (tile, HEAD_LANES), 0)
    idx = lax.broadcasted_iota(jnp.int32, (tile, HEAD_LANES), 1) - BIAS_LANE0
    hidden = jnp.logical_and(jnp.logical_and(idx >= 0, idx < tile // CHUNK), idx > row // CHUNK)
    dbias_ref[...] = jnp.where(hidden, MASK_BIAS, 0.0).astype(BF16)
    acc_ref[...] = jnp.zeros(acc_ref.shape, F32)

    def scores(qi, j, s_ref):
        k = k_ref[0, 0, pl.ds(pl.multiple_of(j * tile, tile), tile), :]
        diag = qi == j
        for b in range(nblk):
            q = q_ref[0, 0, pl.ds(pl.multiple_of(qi * tile + b * rb, rb), rb), :]
            q = q + jnp.where(diag, dbias_ref[b * rb:(b + 1) * rb], jnp.zeros((rb, HEAD_LANES), BF16))
            s_ref[b * rb:(b + 1) * rb] = _dot_nt(q, k)

    def softmax_pv(qi, j, s_ref):
        v = v_ref[0, 0, pl.ds(pl.multiple_of(j * tile, tile), tile), :]
        first = j == 0
        for b in range(nblk):
            rs = slice(b * rb, (b + 1) * rb)
            s = s_ref[rs]
            m_old = jnp.where(first, -1e30, m_ref[rs])
            m_new = jnp.maximum(m_old, jnp.max(s, axis=-1, keepdims=True))
            alpha = jnp.exp2(m_old - m_new)
            p = jnp.exp2(s - jnp.concatenate([m_new] * (tile // LANES), axis=1))
            acc = (acc_ref[rs] * jnp.concatenate([alpha] * (HEAD_LANES // LANES), axis=1)
                   + _dot(p.astype(BF16), v))
            acc_ref[rs] = acc
            m_ref[rs] = m_new
            o_ref[0, pl.ds(pl.multiple_of(qi * tile + b * rb, rb), rb), :] = (
                acc[:, :V_DIM] * (1.0 / acc[:, V_DIM:])).astype(BF16)

    def step(qi, j, s_cur, s_next):
        wrap = j == qi
        qn = jnp.minimum(jnp.where(wrap, qi + 1, qi), nq - 1)
        jn = jnp.where(wrap, 0, j + 1)
        scores(qn, jn, s_next)
        softmax_pv(qi, j, s_cur)
        return qn, jn

    def two_steps(_, carry):
        return step(*step(*carry, s0_ref, s1_ref), s1_ref, s0_ref)

    scores(0, 0, s0_ref)
    n_steps = nq * (nq + 1) // 2
    n_pairs = n_steps // 2
    carry = lax.fori_loop(0, n_pairs, two_steps, (jnp.int32(0), jnp.int32(0)), unroll=2 if n_pairs % 2 == 0 else 1)
    if n_steps % 2:
        step(*carry, s0_ref, s1_ref)


def _flat_attend(q, k, v, *, tile, name):
    bx, nh, tx, _ = q.shape
    assert tx % tile == 0 and tile % FLAT_BLOCK_ROWS == 0
    head = pl.BlockSpec((1, 1, tx, HEAD_LANES), lambda b, hh: (b, hh, 0, 0))
    return pl.pallas_call(
        functools.partial(_flat_attn_body, tile=tile, nq=tx // tile, rb=FLAT_BLOCK_ROWS),
        grid=(bx, nh),
        in_specs=[head, head, head],
        out_specs=pl.BlockSpec((1, tx, V_DIM), lambda b, hh: (b, 0, hh)),
        out_shape=jax.ShapeDtypeStruct((bx, tx, nh * V_DIM), BF16),
        scratch_shapes=[pltpu.VMEM((tile, LANES), F32), pltpu.VMEM((tile, HEAD_LANES), F32),
                        pltpu.VMEM((tile, tile), F32), pltpu.VMEM((tile, tile), F32),
                        pltpu.VMEM((tile, HEAD_LANES), BF16)],
        compiler_params=pltpu.CompilerParams(
            dimension_semantics=("arbitrary", "arbitrary"), vmem_limit_bytes=VMEM_LIMIT_BYTES),
        name=name,
    )(q, k, v)


def _out_body(o_ref, gate_ref, x_ref, wo_ref, gpost_ref, xo_ref):
    y = _dot((o_ref[0].astype(F32) * _silu(gate_ref[0].astype(F32))).astype(BF16), wo_ref[...])
    xo_ref[0] = x_ref[0] + _rms(y, gpost_ref[...])


def _odd_output(o_heads, gate, x, lyr, o, p, *, tt, name):
    bx, tx, _ = x.shape
    assert tx % tt == 0
    row = lambda b, t: (b, t, 0)
    return pl.pallas_call(
        _out_body,
        grid=(bx, tx // tt),
        in_specs=[pl.BlockSpec((1, tt, C_MIX), row), pl.BlockSpec((1, tt, C_MIX), row),
                  pl.BlockSpec((1, tt, D_MODEL), row),
                  _resident((None, C_MIX, D_MODEL), lambda b, t: (o, 0, 0)),
                  _resident((None, 1, D_MODEL), lambda b, t: (lyr, 0, 0))],
        out_specs=pl.BlockSpec((1, tt, D_MODEL), row),
        out_shape=jax.ShapeDtypeStruct((bx, tx, D_MODEL), F32),
        compiler_params=pltpu.CompilerParams(
            dimension_semantics=("arbitrary", "arbitrary"), vmem_limit_bytes=VMEM_LIMIT_BYTES),
        name=name,
    )(o_heads, gate, x, p["wo"], p["norm_post"])


def _odd_project(x, cos, sin, lyr, o, p, *, nb, tt, name):
    bx, tx, _ = x.shape
    assert bx % nb == 0 and tx % tt == 0 and tt % 16 == 0
    grid = (bx // nb, tx // tt)
    row = lambda b, t: (b, t, 0)
    hrow = lambda b, t: (b, 0, t, 0)
    in_specs = [
        pl.BlockSpec((nb, tt, D_MODEL), row),
        _resident((None, 1, D_MODEL), lambda b, t: (lyr, 0, 0)),
        _resident((None, D_MODEL, ODD_COLS), lambda b, t: (o, 0, 0)),
        _resident((None, 1, Q_LORA), lambda b, t: (o, 0, 0)),
        _resident((None, 1, KV_LORA), lambda b, t: (o, 0, 0)),
        _resident((None, Q_LORA, QUP_COLS), lambda b, t: (o, 0, 0)),
        _resident((None, N_HEADS, QK_NOPE, KV_LORA), lambda b, t: (o, 0, 0, 0)),
        pl.BlockSpec((nb * tt, LANES), lambda b, t: (t, 0)),
        pl.BlockSpec((nb * tt, LANES), lambda b, t: (t, 0)),
    ]
    out_shape = [
        jax.ShapeDtypeStruct((bx, N_HEADS, tx, KV_LORA), BF16),
        jax.ShapeDtypeStruct((bx, N_HEADS, tx, QK_ROPE), BF16),
        jax.ShapeDtypeStruct((bx, tx, KV_LORA), F32),
        jax.ShapeDtypeStruct((bx, tx, QK_ROPE), F32),
        jax.ShapeDtypeStruct((bx, tx, KV_LORA), BF16),
        jax.ShapeDtypeStruct((bx, tx, QK_ROPE), BF16),
        jax.ShapeDtypeStruct((bx, tx, C_MIX), F32),
    ]
    out_specs = [
        pl.BlockSpec((nb, N_HEADS, tt, KV_LORA), hrow),
        pl.BlockSpec((nb, N_HEADS, tt, QK_ROPE), hrow),
        pl.BlockSpec((nb, tt, KV_LORA), row),
        pl.BlockSpec((nb, tt, QK_ROPE), row),
        pl.BlockSpec((nb, tt, KV_LORA), row),
        pl.BlockSpec((nb, tt, QK_ROPE), row),
        pl.BlockSpec((nb, tt, C_MIX), row),
    ]
    return pl.pallas_call(
        functools.partial(_proj_body, nb=nb, tt=tt),
        grid=grid,
        in_specs=in_specs,
        out_specs=out_specs,
        out_shape=out_shape,
        compiler_params=pltpu.CompilerParams(
            dimension_semantics=("arbitrary", "arbitrary"), vmem_limit_bytes=VMEM_LIMIT_BYTES),
        name=name,
    )(x, p["norm_pre"], p["win_o"], p["qn"], p["kvn"], p["wq"], p["wuk"], cos, sin)


def _cached_attn_body(qa_ref, qr_ref, cckv_ref, ckr_ref, nckv_ref, nkr_ref, o_ref, *, tq, past):
    rows = N_HEADS * tq
    qa = qa_ref[0].reshape(rows, KV_LORA)
    qr = qr_ref[0].reshape(rows, QK_ROPE)
    kc = cckv_ref[0].astype(BF16)
    kn = nckv_ref[0]
    s_old = _dot_nt(qa, kc) + _dot_nt(qr, ckr_ref[0].astype(BF16))
    s_new = _dot_nt(qa, kn) + _dot_nt(qr, nkr_ref[0])
    kpos = past + lax.broadcasted_iota(jnp.int32, (1, tq, tq), 2)
    qpos = past + lax.broadcasted_iota(jnp.int32, (1, tq, tq), 1)
    s_new = jnp.where(kpos // CHUNK <= qpos // CHUNK, s_new.reshape(N_HEADS, tq, tq), -jnp.inf).reshape(rows, tq)
    m = jnp.maximum(jnp.max(s_old, axis=-1, keepdims=True), jnp.max(s_new, axis=-1, keepdims=True))
    p_old = jnp.exp2(s_old - m)
    p_new = jnp.exp2(s_new - m)
    l = jnp.sum(p_old, axis=-1, keepdims=True) + jnp.sum(p_new, axis=-1, keepdims=True)
    o_lat = (_dot(p_old.astype(BF16), kc) + _dot(p_new.astype(BF16), kn)) * (1.0 / l)
    o_ref[:, 0] = o_lat.astype(BF16).reshape(N_HEADS, tq, KV_LORA)


def _cached_attend(qa, qr, cache_ckv, cache_krope, ckvb, krb, o, *, name):
    bx, _, tq, _ = qa.shape
    past = cache_ckv.shape[2]
    assert tq % 16 == 0
    in_specs = [
        pl.BlockSpec((1, N_HEADS, tq, KV_LORA), lambda b: (b, 0, 0, 0)),
        pl.BlockSpec((1, N_HEADS, tq, QK_ROPE), lambda b: (b, 0, 0, 0)),
        pl.BlockSpec((None, 1, past, KV_LORA), lambda b: (o, b, 0, 0)),
        pl.BlockSpec((None, 1, past, QK_ROPE), lambda b: (o, b, 0, 0)),
        pl.BlockSpec((1, tq, KV_LORA), lambda b: (b, 0, 0)),
        pl.BlockSpec((1, tq, QK_ROPE), lambda b: (b, 0, 0)),
    ]
    return pl.pallas_call(
        functools.partial(_cached_attn_body, tq=tq, past=past),
        grid=(bx,),
        in_specs=in_specs,
        out_specs=pl.BlockSpec((N_HEADS, 1, tq, KV_LORA), lambda b: (0, b, 0, 0)),
        out_shape=jax.ShapeDtypeStruct((N_HEADS, bx, tq, KV_LORA), BF16),
        compiler_params=pltpu.CompilerParams(
            dimension_semantics=("arbitrary",), vmem_limit_bytes=VMEM_LIMIT_BYTES),
        name=name,
    )(qa, qr, cache_ckv, cache_krope, ckvb, krb)


def _latent_out_body(olat_ref, gate_ref, x_ref, wuv_ref, wo_ref, gpost_ref, xo_ref):
    o = jnp.concatenate([_dot(olat_ref[hh], wuv_ref[hh]) for hh in range(N_HEADS)], axis=1)
    y = _dot((o * _silu(gate_ref[...])).astype(BF16), wo_ref[...])
    xo_ref[...] = x_ref[...] + _rms(y, gpost_ref[...])


def _latent_output(o_lat, gate, x, lyr, o, p, *, name):
    m = x.shape[0]
    whole = lambda shape: pl.BlockSpec(shape, lambda i: (0,) * len(shape))
    return pl.pallas_call(
        _latent_out_body,
        grid=(1,),
        in_specs=[whole((N_HEADS, m, KV_LORA)), whole((m, C_MIX)), whole((m, D_MODEL)),
                  pl.BlockSpec((None, N_HEADS, KV_LORA, V_DIM), lambda i: (o, 0, 0, 0)),
                  pl.BlockSpec((None, C_MIX, D_MODEL), lambda i: (o, 0, 0)),
                  pl.BlockSpec((None, 1, D_MODEL), lambda i: (lyr, 0, 0))],
        out_specs=whole((m, D_MODEL)),
        out_shape=jax.ShapeDtypeStruct((m, D_MODEL), F32),
        compiler_params=pltpu.CompilerParams(
            dimension_semantics=("arbitrary",), vmem_limit_bytes=VMEM_LIMIT_BYTES),
        name=name,
    )(o_lat, gate, x, p["wuv"], p["wo"], p["norm_post"])


def _rope_tables(pos0, n, copies=1):
    half = QK_ROPE // 2
    freqs = ROPE_BASE ** (-np.arange(half, dtype=np.float64) / half)
    ang = (pos0 + np.arange(n, dtype=np.float64))[:, None] * freqs[None, :]
    c, s = np.cos(ang), np.sin(ang)
    cos64, sin64 = np.concatenate([c, c], axis=1), np.concatenate([-s, s], axis=1)
    tables = (np.tile(cos64, (copies, LANES // QK_ROPE)), np.tile(sin64, (copies, LANES // QK_ROPE)),
              np.tile(np.concatenate([cos64, sin64], axis=1), (copies, 1)))
    return tuple(jnp.asarray(tab, F32) for tab in tables)


def _half_swap(w):
    half = QK_ROPE // 2
    return jnp.concatenate([w[..., half:], w[..., :half]], axis=-1)


def _prepare_params(norm_pre, norm_post, w_in_even, w_pool, pool_scale, sgu_ln_g, sgu_ln_b, w_spatial, b_spatial,
                    w_out_even, w_in_odd, q_norm, kv_norm, w_q_up, w_kv_up, w_o):
    n_odd = w_in_odd.shape[0]
    kr_w = w_in_odd[..., Q_LORA + KV_LORA:Q_LORA + KV_LORA + QK_ROPE]
    win_o = jnp.concatenate([w_in_odd[..., :Q_LORA + KV_LORA], w_in_odd[..., Q_LORA + KV_LORA + QK_ROPE:],
                             kr_w, _half_swap(kr_w)], axis=-1)
    wq_r = w_q_up[..., QK_NOPE:]
    wq = jnp.concatenate([w_q_up[..., :QK_NOPE].reshape(n_odd, Q_LORA, N_HEADS * QK_NOPE),
                          wq_r.reshape(n_odd, Q_LORA, N_HEADS * QK_ROPE),
                          _half_swap(wq_r).reshape(n_odd, Q_LORA, N_HEADS * QK_ROPE)], axis=-1)
    return {
        "norm_pre": norm_pre[:, None, :],
        "norm_post": norm_post[:, None, :],
        "win_e": w_in_even.astype(BF16),
        "wpool": w_pool.astype(BF16),
        "pscale": pool_scale[:, None, :],
        "lng": sgu_ln_g[:, None, :],
        "lnb": sgu_ln_b[:, None, :],
        "ws": w_spatial,
        "bst": jnp.swapaxes(b_spatial, 1, 2),
        "wout_e": w_out_even.astype(BF16),
        "win_o": win_o.astype(BF16),
        "qn": q_norm[:, None, :],
        "kvn": kv_norm[:, None, :],
        "wq": wq.astype(BF16),
        "wuk": jnp.transpose(w_kv_up[..., :QK_NOPE], (0, 2, 3, 1)).astype(BF16),
        "wuv": jnp.transpose(w_kv_up[..., QK_NOPE:], (0, 2, 1, 3)).astype(BF16),
        "wukv": jnp.concatenate([w_kv_up[..., :QK_NOPE].reshape(n_odd, KV_LORA, N_HEADS * QK_NOPE),
                                 w_kv_up[..., QK_NOPE:].reshape(n_odd, KV_LORA, C_MIX)], axis=-1).astype(BF16),
        "wo": w_o.astype(BF16),
    }


def _tile_sizes(t_prompt):
    tt_even = min(t_prompt, 512)
    tt_proj = min(t_prompt, 512)
    tile = min(t_prompt, 1024)
    tt_out = min(t_prompt, 512)
    return tt_even, tt_proj, tile, tt_out


def kernel(x_prompt, x_sample, cache_pool, cache_ckv, cache_krope, norm_pre, norm_post, w_in_even, w_pool, pool_scale, sgu_ln_g, sgu_ln_b, w_spatial, b_spatial, w_out_even, w_in_odd, q_norm, kv_norm, w_q_up, w_kv_up, w_o):
    depth = norm_pre.shape[0]
    b, t, _ = x_prompt.shape
    db, s, _ = x_sample.shape
    past = cache_ckv.shape[2]
    p = _prepare_params(norm_pre, norm_post, w_in_even, w_pool, pool_scale, sgu_ln_g, sgu_ln_b, w_spatial,
                        b_spatial, w_out_even, w_in_odd, q_norm, kv_norm, w_q_up, w_kv_up, w_o)
    tt_even, tt_proj, tile, tt_out = _tile_sizes(t)

    cos_p, sin_p, cs_p = _rope_tables(0, t)
    cos_s, sin_s, _ = _rope_tables(past, s, copies=db)

    xp, xs = x_prompt, x_sample
    pool_p, pool_s, sgu_s = [], [], []
    ckv_p, kr_p, ckv_s, kr_s = [], [], [], []
    zero_hist = jnp.zeros((b, HIST_ROWS, W_A), F32)
    for lyr in range(depth):
        if lyr % 2 == 0:
            e = lyr // 2
            xp, hp = _even_layer(xp, zero_hist, lyr, e, p, nb=1, tt=tt_even, pos0=0, emit_v=False,
                                 name=f"even{e}_prompt")
            hist_s = jnp.pad(cache_pool[e], ((0, 0), (HIST_ROWS - POOL_HIST, 0), (0, 0)))
            xs, hs, vs = _even_layer(xs, hist_s, lyr, e, p, nb=db, tt=s, pos0=past, emit_v=True,
                                     name=f"even{e}_sample")
            pool_p.append(hp[:, HIST_ROWS - POOL_HIST:])
            pool_s.append(hs[:, HIST_ROWS - POOL_HIST:])
            sgu_s.append(vs)
        else:
            o = lyr // 2
            qh, kh, vh, ckv, kr, gate = _odd_project_heads(xp, cos_p, sin_p, cs_p, lyr, o, p, tt=tt_proj, tile=tile,
                                                           name=f"proj{o}_prompt")
            o_heads = _flat_attend(qh, kh, vh, tile=tile, name=f"attn{o}_prompt")
            xp = _odd_output(o_heads, gate, xp, lyr, o, p, tt=tt_out, name=f"out{o}_prompt")
            ckv_p.append(ckv)
            kr_p.append(kr)

            qa, qr, ckv, kr, ckvb, krb, gate = _odd_project(xs, cos_s, sin_s, lyr, o, p, nb=db, tt=s,
                                                            name=f"proj{o}_sample")
            o_lat = _cached_attend(qa, qr, cache_ckv, cache_krope, ckvb, krb, o, name=f"attn{o}_sample")
            xs = _latent_output(o_lat.reshape(N_HEADS, db * s, KV_LORA), gate.reshape(db * s, C_MIX),
                                xs.reshape(db * s, D_MODEL), lyr, o, p, name=f"out{o}_sample").reshape(db, s, D_MODEL)
            ckv_s.append(ckv)
            kr_s.append(kr)
    return (xp, xs, jnp.stack(pool_p), jnp.stack(pool_s), jnp.stack(sgu_s),
            jnp.stack(ckv_p), jnp.stack(kr_p), jnp.stack(ckv_s), jnp.stack(kr_s))
```

```python
import functools

import jax
import jax.numpy as jnp
import numpy as np
from jax import lax
from jax.experimental import pallas as pl
from jax.experimental.pallas import tpu as pltpu

D_MODEL = 1024
CHUNK = 64
EPS = 1e-6
POOL_WINDOWS = (2, 4, 8, 16)
POOL_HIST = max(POOL_WINDOWS) - 1
W_A = D_MODEL
POOL_GDIM = W_A // len(POOL_WINDOWS)
W_B = D_MODEL
SGU_HEADS = 4
SGU_HDIM = W_B // SGU_HEADS
SGU_CHUNK = 128
EVEN_MIX = W_A + W_B
EVEN_IN = W_A + 2 * W_B + EVEN_MIX
N_HEADS = D_MODEL // 128
QK_NOPE = 128
QK_ROPE = 64
V_DIM = 128
KV_LORA = D_MODEL // 4
Q_LORA = 3 * D_MODEL // 8
C_MIX = N_HEADS * V_DIM
ROPE_BASE = 10000.0
ATTN_SCALE = (QK_NOPE + QK_ROPE) ** -0.5
LOG2E = 1.4426950408889634

LANES = 128
SUBLANES = 8
BF16_ROWS = 16
HIST_ROWS = 16
VMEM_LIMIT_BYTES = 56 * 1024 * 1024
FLAT_BLOCK_ROWS = 256
VT_ROWS = V_DIM + BF16_ROWS
HEAD_LANES = 2 * LANES
BIAS_LANE0 = LANES + QK_ROPE
MASK_BIAS = -1e30

ODD_Q0, ODD_KV0, ODD_G0, ODD_KR0 = 0, Q_LORA, Q_LORA + KV_LORA, Q_LORA + KV_LORA + C_MIX
ODD_COLS = ODD_KR0 + 2 * QK_ROPE
QUP_R0 = N_HEADS * QK_NOPE
QUP_S0 = QUP_R0 + N_HEADS * QK_ROPE
QUP_COLS = QUP_S0 + N_HEADS * QK_ROPE

F32 = jnp.float32
BF16 = jnp.bfloat16


def _dot(a, b):
    return jnp.dot(a, b, preferred_element_type=F32)


def _dot_nt(a, b):
    return lax.dot_general(a, b, (((1,), (1,)), ((), ())), preferred_element_type=F32)


def _rms(x, g):
    return x * lax.rsqrt(jnp.mean(x * x, axis=-1, keepdims=True) + EPS) * g


def _silu(x):
    return x * (1.0 / (1.0 + jnp.exp(-x)))


def _gelu(x):
    return 0.5 * x * (1.0 + lax.erf(x * (2.0 ** -0.5)))


def _even_body(x_ref, hist_ref, gpre_ref, gpost_ref, win_ref, wpool_ref, pscale_ref, lng_ref, lnb_ref,
               ws_ref, bst_ref, wout_ref, xo_ref, histo_ref, *rest, nb, tt, seg, pos0, emit_v, nsub):
    if emit_v:
        vo_ref, aext_ref, mix_ref = rest
    else:
        aext_ref, mix_ref = rest
    t = pl.program_id(1)
    nt = pl.num_programs(1)
    m = nb * tt
    rsz = m // nsub
    rps = tt // nsub if nb == 1 else tt
    st = [dict() for _ in range(nsub)]

    @pl.when(t == 0)
    def _():
        aext_ref[:, 0:HIST_ROWS, :] = hist_ref[...]

    def rows(i):
        return slice(i * rsz, (i + 1) * rsz)

    def stage_a(i):
        x = x_ref[0, rows(i), :] if nb == 1 else x_ref[...].reshape(m, D_MODEL)
        h = _rms(x, gpre_ref[...]).astype(BF16)
        a = _dot(h, win_ref[:, 0:W_A])
        if nb == 1:
            aext_ref[0, HIST_ROWS + i * rsz:HIST_ROWS + (i + 1) * rsz, :] = a
        else:
            aext_ref[:, HIST_ROWS:HIST_ROWS + tt, :] = a.reshape(nb, tt, W_A)
        st[i]["h"] = h

    def stage_uv(i):
        st[i]["uv"] = _dot(st[i]["h"], win_ref[:, W_A:W_A + 2 * W_B])

    def stage_gate(i):
        st[i]["gate"] = _dot(st[i].pop("h"), win_ref[:, W_A + 2 * W_B:])

    def stage_pool(i):
        r0 = HIST_ROWS + (i * rsz if nb == 1 else 0)
        pos = pos0 + t * tt + (r0 - HIST_ROWS) + lax.broadcasted_iota(jnp.int32, (rps, 1), 0)
        for g, w in enumerate(POOL_WINDOWS):
            cs = slice(g * POOL_GDIM, (g + 1) * POOL_GDIM)
            inv_cnt = 1.0 / jnp.minimum(pos + 1, w).astype(F32)
            for s in range(nb):
                wsum = aext_ref[s, r0 - HIST_ROWS:r0 + rps, cs]
                k = 1
                while k < w:
                    wsum = wsum + pltpu.roll(wsum, k, axis=0)
                    k *= 2
                cur = aext_ref[s, r0:r0 + rps, cs]
                mix_ref[i * rsz + s * rps:i * rsz + (s + 1) * rps, cs] = wsum[HIST_ROWS:] * inv_cnt - cur
            y_a = _dot(mix_ref[rows(i), cs].astype(BF16), wpool_ref[g]) * pscale_ref[:, cs]
            mix_ref[rows(i), cs] = y_a

    def stage_sgu(i):
        uv = _gelu(st[i].pop("uv"))
        u = uv[:, :W_B]
        v = uv[:, W_B:]
        vc = v - jnp.mean(v, axis=-1, keepdims=True)
        vn = vc * lax.rsqrt(jnp.mean(vc * vc, axis=-1, keepdims=True) + EPS) * lng_ref[...] + lnb_ref[...]
        if emit_v:
            vo_ref[...] = vn.reshape(nb, tt, W_B)
        vb = vn.astype(BF16)
        ii = lax.broadcasted_iota(jnp.int32, (seg, seg), 0)
        jj = lax.broadcasted_iota(jnp.int32, (seg, seg), 1)
        causal = (jj // CHUNK) <= (ii // CHUNK)
        for g in range(SGU_HEADS):
            hs = slice(g * SGU_HDIM, (g + 1) * SGU_HDIM)
            wsg = jnp.where(causal, ws_ref[g, 0:seg, 0:seg], 0.0).astype(BF16)
            bcol = bst_ref[0:seg, g:g + 1]
            for c in range(rsz // seg):
                rs = slice(c * seg, (c + 1) * seg)
                mixed = _dot(wsg, vb[rs, hs]) + bcol
                mix_ref[i * rsz + c * seg:i * rsz + (c + 1) * seg, W_A + g * SGU_HDIM:W_A + (g + 1) * SGU_HDIM] = (
                    u[rs, hs] * mixed)

    def stage_out(i):
        mixv = (mix_ref[rows(i)] * _silu(st[i].pop("gate"))).astype(BF16)
        y = _rms(_dot(mixv, wout_ref[...]), gpost_ref[...])
        if nb == 1:
            xo_ref[0, rows(i), :] = x_ref[0, rows(i), :] + y
        else:
            xo_ref[...] = x_ref[...] + y.reshape(nb, tt, D_MODEL)

    if nsub == 1:
        order = [(stage_a, 0), (stage_uv, 0), (stage_pool, 0), (stage_gate, 0), (stage_sgu, 0), (stage_out, 0)]
    else:
        order = [(stage_a, 0), (stage_uv, 0), (stage_a, 1), (stage_sgu, 0), (stage_uv, 1), (stage_pool, 0),
                 (stage_gate, 0), (stage_sgu, 1), (stage_pool, 1), (stage_out, 0), (stage_gate, 1), (stage_out, 1)]
    for stage, i in order:
        stage(i)

    tail = aext_ref[:, tt:tt + HIST_ROWS, :]

    @pl.when(t == nt - 1)
    def _():
        histo_ref[...] = tail

    aext_ref[:, 0:HIST_ROWS, :] = tail


def _resident(block_shape, index_map):
    return pl.BlockSpec(block_shape, index_map, pipeline_mode=pl.Buffered(1))


def _even_layer(x, hist, lyr, e, p, *, nb, tt, pos0, emit_v, name):
    bx, tx, _ = x.shape
    seg = min(tx, SGU_CHUNK)
    assert bx % nb == 0 and tx % tt == 0 and tt % seg == 0 and tt >= HIST_ROWS and tt % 8 == 0
    grid = (bx // nb, tx // tt)
    row = lambda b, t: (b, t, 0)
    first = lambda b, t: (b, 0, 0)
    in_specs = [
        pl.BlockSpec((nb, tt, D_MODEL), row),
        pl.BlockSpec((nb, HIST_ROWS, W_A), first),
        _resident((None, 1, D_MODEL), lambda b, t: (lyr, 0, 0)),
        _resident((None, 1, D_MODEL), lambda b, t: (lyr, 0, 0)),
        _resident((None, D_MODEL, EVEN_IN), lambda b, t: (e, 0, 0)),
        _resident((None, len(POOL_WINDOWS), POOL_GDIM, POOL_GDIM), lambda b, t: (e, 0, 0, 0)),
        _resident((None, 1, W_A), lambda b, t: (e, 0, 0)),
        _resident((None, 1, W_B), lambda b, t: (e, 0, 0)),
        _resident((None, 1, W_B), lambda b, t: (e, 0, 0)),
        _resident((None, SGU_HEADS, SGU_CHUNK, SGU_CHUNK), lambda b, t: (e, 0, 0, 0)),
        _resident((None, SGU_CHUNK, SGU_HEADS), lambda b, t: (e, 0, 0)),
        _resident((None, EVEN_MIX, D_MODEL), lambda b, t: (e, 0, 0)),
    ]
    out_shape = [jax.ShapeDtypeStruct((bx, tx, D_MODEL), F32), jax.ShapeDtypeStruct((bx, HIST_ROWS, W_A), F32)]
    out_specs = [pl.BlockSpec((nb, tt, D_MODEL), row), pl.BlockSpec((nb, HIST_ROWS, W_A), first)]
    if emit_v:
        out_shape.append(jax.ShapeDtypeStruct((bx, tx, W_B), F32))
        out_specs.append(pl.BlockSpec((nb, tt, W_B), row))
    nsub = 2 if nb == 1 and tt % (2 * seg) == 0 else 1
    body = functools.partial(_even_body, nb=nb, tt=tt, seg=seg, pos0=pos0, emit_v=emit_v, nsub=nsub)
    return pl.pallas_call(
        body,
        grid=grid,
        in_specs=in_specs,
        out_specs=out_specs,
        out_shape=out_shape,
        scratch_shapes=[pltpu.VMEM((nb, HIST_ROWS + tt, W_A), F32), pltpu.VMEM((nb * tt, EVEN_MIX), F32)],
        compiler_params=pltpu.CompilerParams(
            dimension_semantics=("arbitrary", "arbitrary"), vmem_limit_bytes=VMEM_LIMIT_BYTES),
        name=name,
    )(x, hist, p["norm_pre"], p["norm_post"], p["win_e"], p["wpool"], p["pscale"], p["lng"], p["lnb"],
      p["ws"], p["bst"], p["wout_e"])


def _proj_body(x_ref, gpre_ref, win_ref, qn_ref, kvn_ref, wq_ref, wuk_ref, cos_ref, sin_ref,
               qa_ref, qr_ref, ckv_ref, kr_ref, ckvb_ref, krb_ref, gate_ref, *, nb, tt):
    m = nb * tt
    x = x_ref[...].reshape(m, D_MODEL)
    h = _rms(x, gpre_ref[...]).astype(BF16)
    z = _dot(h, win_ref[...])
    gate_ref[...] = z[:, ODD_G0:ODD_KR0].reshape(nb, tt, C_MIX)

    ckv = _rms(z[:, ODD_KV0:ODD_G0], kvn_ref[...])
    ckv_ref[...] = ckv.reshape(nb, tt, KV_LORA)
    ckvb_ref[...] = ckv.astype(BF16).reshape(nb, tt, KV_LORA)

    cos = cos_ref[...]
    sin = sin_ref[...]
    kr = (z[:, ODD_KR0:ODD_KR0 + QK_ROPE] * cos[:, :QK_ROPE]
          + z[:, ODD_KR0 + QK_ROPE:ODD_COLS] * sin[:, :QK_ROPE])
    kr_ref[...] = kr.reshape(nb, tt, QK_ROPE)
    krb_ref[...] = kr.astype(BF16).reshape(nb, tt, QK_ROPE)

    qcn = _rms(z[:, ODD_Q0:ODD_KV0], qn_ref[...]).astype(BF16)
    q = _dot(qcn, wq_ref[...])
    c_exp = ATTN_SCALE * LOG2E
    heads_per_vreg = LANES // QK_ROPE
    for c in range(N_HEADS // heads_per_vreg):
        r = ((q[:, QUP_R0 + c * LANES:QUP_R0 + (c + 1) * LANES] * cos
              + q[:, QUP_S0 + c * LANES:QUP_S0 + (c + 1) * LANES] * sin) * c_exp).astype(BF16)
        for i in range(heads_per_vreg):
            qr_ref[:, heads_per_vreg * c + i, :, :] = r[:, i * QK_ROPE:(i + 1) * QK_ROPE].reshape(nb, tt, QK_ROPE)
    for hh in range(N_HEADS):
        q_nope = q[:, hh * QK_NOPE:(hh + 1) * QK_NOPE].astype(BF16)
        qa_ref[:, hh, :, :] = (_dot(q_nope, wuk_ref[hh]) * c_exp).astype(BF16).reshape(nb, tt, KV_LORA)


def _proj_heads_body(x_ref, gpre_ref, win_ref, qn_ref, kvn_ref, wq_ref, wukv_ref, cos_ref, sin_ref, cs_ref,
                     q_ref, k_ref, vt_ref, ckv_ref, kr_ref, gate_ref, *, tt, tile):
    x = x_ref[0]
    h = _rms(x, gpre_ref[...]).astype(BF16)
    z = _dot(h, win_ref[...])
    gate_ref[0] = z[:, ODD_G0:ODD_KR0].astype(BF16)
    ckv = _rms(z[:, ODD_KV0:ODD_G0], kvn_ref[...])
    ckv_ref[0] = ckv

    lane = lax.broadcasted_iota(jnp.int32, (tt, LANES), 1)
    low = lane < QK_ROPE
    u = z[:, ODD_KR0:ODD_COLS] * cs_ref[...]
    kr_full = u + pltpu.roll(u, QK_ROPE, axis=1)
    kr_ref[0] = kr_full[:, :QK_ROPE]
    pos = pl.program_id(1) * tt + lax.broadcasted_iota(jnp.int32, (tt, LANES), 0)
    onehot = (lane - QK_ROPE == (pos % tile) // CHUNK).astype(F32)
    k_hi = jnp.where(low, kr_full, onehot).astype(BF16)
    vt_ones = jnp.ones((VT_ROWS - V_DIM, tt), BF16)

    kv = _dot(ckv.astype(BF16), wukv_ref[...])
    qcn = _rms(z[:, ODD_Q0:ODD_KV0], qn_ref[...]).astype(BF16)
    q = _dot(qcn, wq_ref[...])
    c_exp = ATTN_SCALE * LOG2E
    heads_per_vreg = LANES // QK_ROPE
    for hh in range(N_HEADS):
        c, i = divmod(hh, heads_per_vreg)
        r = (q[:, QUP_R0 + c * LANES:QUP_R0 + (c + 1) * LANES] * cos_ref[...]
             + q[:, QUP_S0 + c * LANES:QUP_S0 + (c + 1) * LANES] * sin_ref[...]) * c_exp
        if i:
            r = pltpu.roll(r, LANES - i * QK_ROPE, axis=1)
        q_ref[0, hh, :, 0:LANES] = (q[:, hh * QK_NOPE:(hh + 1) * QK_NOPE] * c_exp).astype(BF16)
        q_ref[0, hh, :, LANES:2 * LANES] = jnp.where(low, r, 0.0).astype(BF16)
        k_ref[0, hh, :, 0:LANES] = kv[:, hh * QK_NOPE:(hh + 1) * QK_NOPE].astype(BF16)
        k_ref[0, hh, :, LANES:2 * LANES] = k_hi
        vt_ref[0, hh, 0:V_DIM, :] = kv[:, C_MIX + hh * V_DIM:C_MIX + (hh + 1) * V_DIM].T.astype(BF16)
        vt_ref[0, hh, V_DIM:VT_ROWS, :] = vt_ones


def _odd_project_heads(x, cos, sin, cs, lyr, o, p, *, tt, tile, name):
    bx, tx, _ = x.shape
    assert tx % tt == 0 and tt % 16 == 0 and tile % CHUNK == 0 and tile // CHUNK <= LANES - QK_ROPE
    grid = (bx, tx // tt)
    row = lambda b, t: (b, t, 0)
    hrow = lambda b, t: (b, 0, t, 0)
    tab = pl.BlockSpec((tt, LANES), lambda b, t: (t, 0))
    in_specs = [
        pl.BlockSpec((1, tt, D_MODEL), row),
        _resident((None, 1, D_MODEL), lambda b, t: (lyr, 0, 0)),
        _resident((None, D_MODEL, ODD_COLS), lambda b, t: (o, 0, 0)),
        _resident((None, 1, Q_LORA), lambda b, t: (o, 0, 0)),
        _resident((None, 1, KV_LORA), lambda b, t: (o, 0, 0)),
        _resident((None, Q_LORA, QUP_COLS), lambda b, t: (o, 0, 0)),
        _resident((None, KV_LORA, 2 * C_MIX), lambda b, t: (o, 0, 0)),
        tab, tab, tab,
    ]
    head_arr = jax.ShapeDtypeStruct((bx, N_HEADS, tx, HEAD_LANES), BF16)
    head_spec = pl.BlockSpec((1, N_HEADS, tt, HEAD_LANES), hrow)
    out_shape = [head_arr, head_arr, jax.ShapeDtypeStruct((bx, N_HEADS, VT_ROWS, tx), BF16),
                 jax.ShapeDtypeStruct((bx, tx, KV_LORA), F32),
                 jax.ShapeDtypeStruct((bx, tx, QK_ROPE), F32),
                 jax.ShapeDtypeStruct((bx, tx, C_MIX), BF16)]
    out_specs = [head_spec, head_spec, pl.BlockSpec((1, N_HEADS, VT_ROWS, tt), lambda b, t: (b, 0, 0, t)),
                 pl.BlockSpec((1, tt, KV_LORA), row),
                 pl.BlockSpec((1, tt, QK_ROPE), row),
                 pl.BlockSpec((1, tt, C_MIX), row)]
    return pl.pallas_call(
        functools.partial(_proj_heads_body, tt=tt, tile=tile),
        grid=grid,
        in_specs=in_specs,
        out_specs=out_specs,
        out_shape=out_shape,
        compiler_params=pltpu.CompilerParams(
            dimension_semantics=("arbitrary", "arbitrary"), vmem_limit_bytes=VMEM_LIMIT_BYTES),
        name=name,
    )(x, p["norm_pre"], p["win_o"], p["qn"], p["kvn"], p["wq"], p["wukv"], cos, sin, cs)


def _flat_attn_body(q_ref, k_ref, vt_ref, o_ref, m_ref, acc_ref, s0_ref, s1_ref, dbias_ref, *, tile, nq, rb):
    nblk = tile // rb
    row = lax.broadcasted_iota(jnp.int32, (tile, HEAD_LANES), 0)
    idx = lax.broadcasted_iota(jnp.int32, (tile, HEAD_LANES), 1) - BIAS_LANE0
    hidden = jnp.logical_and(jnp.logical_and(idx >= 0, idx < tile // CHUNK), idx > row // CHUNK)
    dbias_ref[...] = jnp.where(hidden, MASK_BIAS, 0.0).astype(BF16)
    acc_ref[...] = jnp.zeros(acc_ref.shape, F32)

    def scores(b, qi, j, s_ref):
        k = k_ref[0, 0, pl.ds(pl.multiple_of(j * tile, tile), tile), :]
        q = q_ref[0, 0, pl.ds(pl.multiple_of(qi * tile + b * rb, rb), rb), :]
        q = q + jnp.where(qi == j, dbias_ref[b * rb:(b + 1) * rb], jnp.zeros((rb, HEAD_LANES), BF16))
        s_ref[:, b * rb:(b + 1) * rb] = _dot_nt(k, q)

    def softmax_pv(b, qi, j, s_ref):
        vt = vt_ref[0, 0, :, pl.ds(pl.multiple_of(j * tile, tile), tile)]
        cs = slice(b * rb, (b + 1) * rb)
        s = s_ref[:, cs]
        m_old = jnp.where(j == 0, -1e30, m_ref[0:1, cs])
        m_new = jnp.maximum(m_old, jnp.max(s, axis=0, keepdims=True))
        alpha = jnp.exp2(m_old - m_new)
        p = jnp.exp2(s - m_new)
        acc = acc_ref[:, cs] * alpha + _dot(vt, p.astype(BF16))
        acc_ref[:, cs] = acc
        m_ref[:, cs] = jnp.broadcast_to(m_new, (m_ref.shape[0], rb))
        o_t = acc[0:V_DIM] * (1.0 / acc[V_DIM:V_DIM + 1])
        o_ref[0, pl.ds(pl.multiple_of(qi * tile + b * rb, rb), rb), :] = o_t.T.astype(BF16)

    def step(qi, j, s_cur, s_next):
        wrap = j == qi
        qn = jnp.minimum(jnp.where(wrap, qi + 1, qi), nq - 1)
        jn = jnp.where(wrap, 0, j + 1)
        for b in range(nblk):
            scores(b, qn, jn, s_next)
            softmax_pv(b, qi, j, s_cur)
        return qn, jn

    def two_steps(_, carry):
        return step(*step(*carry, s0_ref, s1_ref), s1_ref, s0_ref)

    for b in range(nblk):
        scores(b, 0, 0, s0_ref)
    n_steps = nq * (nq + 1) // 2
    n_pairs = n_steps // 2
    unroll = max(u for u in (1, 2, 3) if n_pairs % u == 0)
    carry = lax.fori_loop(0, n_pairs, two_steps, (jnp.int32(0), jnp.int32(0)), unroll=unroll)
    if n_steps % 2:
        step(*carry, s0_ref, s1_ref)


def _flat_attend(q, k, vt, *, tile, name):
    bx, nh, tx, _ = q.shape
    assert tx % tile == 0 and tile % FLAT_BLOCK_ROWS == 0
    head = pl.BlockSpec((1, 1, tx, HEAD_LANES), lambda b, hh: (b, hh, 0, 0))
    return pl.pallas_call(
        functools.partial(_flat_attn_body, tile=tile, nq=tx // tile, rb=FLAT_BLOCK_ROWS),
        grid=(bx, nh),
        in_specs=[head, head, pl.BlockSpec((1, 1, VT_ROWS, tx), lambda b, hh: (b, hh, 0, 0))],
        out_specs=pl.BlockSpec((1, tx, V_DIM), lambda b, hh: (b, 0, hh)),
        out_shape=jax.ShapeDtypeStruct((bx, tx, nh * V_DIM), BF16),
        scratch_shapes=[pltpu.VMEM((SUBLANES, tile), F32), pltpu.VMEM((VT_ROWS, tile), F32),
                        pltpu.VMEM((tile, tile), F32), pltpu.VMEM((tile, tile), F32),
                        pltpu.VMEM((tile, HEAD_LANES), BF16)],
        compiler_params=pltpu.CompilerParams(
            dimension_semantics=("arbitrary", "arbitrary"), vmem_limit_bytes=VMEM_LIMIT_BYTES),
        name=name,
    )(q, k, vt)


def _out_body(o_ref, gate_ref, x_ref, wo_ref, gpost_ref, xo_ref):
    y = _dot((o_ref[0].astype(F32) * _silu(gate_ref[0].astype(F32))).astype(BF16), wo_ref[...])
    xo_ref[0] = x_ref[0] + _rms(y, gpost_ref[...])


def _odd_output(o_heads, gate, x, lyr, o, p, *, tt, name):
    bx, tx, _ = x.shape
    assert tx % tt == 0
    row = lambda b, t: (b, t, 0)
    return pl.pallas_call(
        _out_body,
        grid=(bx, tx // tt),
        in_specs=[pl.BlockSpec((1, tt, C_MIX), row), pl.BlockSpec((1, tt, C_MIX), row),
                  pl.BlockSpec((1, tt, D_MODEL), row),
                  _resident((None, C_MIX, D_MODEL), lambda b, t: (o, 0, 0)),
                  _resident((None, 1, D_MODEL), lambda b, t: (lyr, 0, 0))],
        out_specs=pl.BlockSpec((1, tt, D_MODEL), row),
        out_shape=jax.ShapeDtypeStruct((bx, tx, D_MODEL), F32),
        compiler_params=pltpu.CompilerParams(
            dimension_semantics=("arbitrary", "arbitrary"), vmem_limit_bytes=VMEM_LIMIT_BYTES),
        name=name,
    )(o_heads, gate, x, p["wo"], p["norm_post"])


def _odd_project(x, cos, sin, lyr, o, p, *, nb, tt, name):
    bx, tx, _ = x.shape
    assert bx % nb == 0 and tx % tt == 0 and tt % 16 == 0
    grid = (bx // nb, tx // tt)
    row = lambda b, t: (b, t, 0)
    hrow = lambda b, t: (b, 0, t, 0)
    in_specs = [
        pl.BlockSpec((nb, tt, D_MODEL), row),
        _resident((None, 1, D_MODEL), lambda b, t: (lyr, 0, 0)),
        _resident((None, D_MODEL, ODD_COLS), lambda b, t: (o, 0, 0)),
        _resident((None, 1, Q_LORA), lambda b, t: (o, 0, 0)),
        _resident((None, 1, KV_LORA), lambda b, t: (o, 0, 0)),
        _resident((None, Q_LORA, QUP_COLS), lambda b, t: (o, 0, 0)),
        _resident((None, N_HEADS, QK_NOPE, KV_LORA), lambda b, t: (o, 0, 0, 0)),
        pl.BlockSpec((nb * tt, LANES), lambda b, t: (t, 0)),
        pl.BlockSpec((nb * tt, LANES), lambda b, t: (t, 0)),
    ]
    out_shape = [
        jax.ShapeDtypeStruct((bx, N_HEADS, tx, KV_LORA), BF16),
        jax.ShapeDtypeStruct((bx, N_HEADS, tx, QK_ROPE), BF16),
        jax.ShapeDtypeStruct((bx, tx, KV_LORA), F32),
        jax.ShapeDtypeStruct((bx, tx, QK_ROPE), F32),
        jax.ShapeDtypeStruct((bx, tx, KV_LORA), BF16),
        jax.ShapeDtypeStruct((bx, tx, QK_ROPE), BF16),
        jax.ShapeDtypeStruct((bx, tx, C_MIX), F32),
    ]
    out_specs = [
        pl.BlockSpec((nb, N_HEADS, tt, KV_LORA), hrow),
        pl.BlockSpec((nb, N_HEADS, tt, QK_ROPE), hrow),
        pl.BlockSpec((nb, tt, KV_LORA), row),
        pl.BlockSpec((nb, tt, QK_ROPE), row),
        pl.BlockSpec((nb, tt, KV_LORA), row),
        pl.BlockSpec((nb, tt, QK_ROPE), row),
        pl.BlockSpec((nb, tt, C_MIX), row),
    ]
    return pl.pallas_call(
        functools.partial(_proj_body, nb=nb, tt=tt),
        grid=grid,
        in_specs=in_specs,
        out_specs=out_specs,
        out_shape=out_shape,
        compiler_params=pltpu.CompilerParams(
            dimension_semantics=("arbitrary", "arbitrary"), vmem_limit_bytes=VMEM_LIMIT_BYTES),
        name=name,
    )(x, p["norm_pre"], p["win_o"], p["qn"], p["kvn"], p["wq"], p["wuk"], cos, sin)


def _cached_attn_body(qa_ref, qr_ref, cckv_ref, ckrt_ref, nckv_ref, nkr_ref, o_ref, *, tq, past):
    rows = N_HEADS * tq
    qa = qa_ref[0].reshape(rows, KV_LORA)
    qr = qr_ref[0].reshape(rows, QK_ROPE)
    kc = cckv_ref[0].astype(BF16)
    kn = nckv_ref[0]
    s_old = _dot_nt(qa, kc) + _dot(qr, ckrt_ref[0].astype(BF16))
    s_new = _dot_nt(qa, kn) + _dot_nt(qr, nkr_ref[0])
    kpos = past + lax.broadcasted_iota(jnp.int32, (1, tq, tq), 2)
    qpos = past + lax.broadcasted_iota(jnp.int32, (1, tq, tq), 1)
    s_new = jnp.where(kpos // CHUNK <= qpos // CHUNK, s_new.reshape(N_HEADS, tq, tq), -jnp.inf).reshape(rows, tq)
    m = jnp.maximum(jnp.max(s_old, axis=-1, keepdims=True), jnp.max(s_new, axis=-1, keepdims=True))
    p_old = jnp.exp2(s_old - m)
    p_new = jnp.exp2(s_new - m)
    l = jnp.sum(p_old, axis=-1, keepdims=True) + jnp.sum(p_new, axis=-1, keepdims=True)
    o_lat = (_dot(p_old.astype(BF16), kc) + _dot(p_new.astype(BF16), kn)) * (1.0 / l)
    o_ref[:, 0] = o_lat.astype(BF16).reshape(N_HEADS, tq, KV_LORA)


def _cached_attend(qa, qr, cache_ckv, cache_krope_t, ckvb, krb, o, *, name):
    bx, _, tq, _ = qa.shape
    past = cache_ckv.shape[2]
    assert tq % 16 == 0
    in_specs = [
        pl.BlockSpec((1, N_HEADS, tq, KV_LORA), lambda b: (b, 0, 0, 0)),
        pl.BlockSpec((1, N_HEADS, tq, QK_ROPE), lambda b: (b, 0, 0, 0)),
        pl.BlockSpec((None, 1, past, KV_LORA), lambda b: (o, b, 0, 0)),
        pl.BlockSpec((None, 1, QK_ROPE, past), lambda b: (o, b, 0, 0)),
        pl.BlockSpec((1, tq, KV_LORA), lambda b: (b, 0, 0)),
        pl.BlockSpec((1, tq, QK_ROPE), lambda b: (b, 0, 0)),
    ]
    return pl.pallas_call(
        functools.partial(_cached_attn_body, tq=tq, past=past),
        grid=(bx,),
        in_specs=in_specs,
        out_specs=pl.BlockSpec((N_HEADS, 1, tq, KV_LORA), lambda b: (0, b, 0, 0)),
        out_shape=jax.ShapeDtypeStruct((N_HEADS, bx, tq, KV_LORA), BF16),
        compiler_params=pltpu.CompilerParams(
            dimension_semantics=("arbitrary",), vmem_limit_bytes=VMEM_LIMIT_BYTES),
        name=name,
    )(qa, qr, cache_ckv, cache_krope_t, ckvb, krb)


def _latent_out_body(olat_ref, gate_ref, x_ref, wuv_ref, wo_ref, gpost_ref, xo_ref):
    o = jnp.concatenate([_dot(olat_ref[hh], wuv_ref[hh]) for hh in range(N_HEADS)], axis=1)
    y = _dot((o * _silu(gate_ref[...])).astype(BF16), wo_ref[...])
    xo_ref[...] = x_ref[...] + _rms(y, gpost_ref[...])


def _latent_output(o_lat, gate, x, lyr, o, p, *, name):
    m = x.shape[0]
    whole = lambda shape: pl.BlockSpec(shape, lambda i: (0,) * len(shape))
    return pl.pallas_call(
        _latent_out_body,
        grid=(1,),
        in_specs=[whole((N_HEADS, m, KV_LORA)), whole((m, C_MIX)), whole((m, D_MODEL)),
                  pl.BlockSpec((None, N_HEADS, KV_LORA, V_DIM), lambda i: (o, 0, 0, 0)),
                  pl.BlockSpec((None, C_MIX, D_MODEL), lambda i: (o, 0, 0)),
                  pl.BlockSpec((None, 1, D_MODEL), lambda i: (lyr, 0, 0))],
        out_specs=whole((m, D_MODEL)),
        out_shape=jax.ShapeDtypeStruct((m, D_MODEL), F32),
        compiler_params=pltpu.CompilerParams(
            dimension_semantics=("arbitrary",), vmem_limit_bytes=VMEM_LIMIT_BYTES),
        name=name,
    )(o_lat, gate, x, p["wuv"], p["wo"], p["norm_post"])


def _rope_tables(pos0, n, copies=1):
    half = QK_ROPE // 2
    freqs = ROPE_BASE ** (-np.arange(half, dtype=np.float64) / half)
    ang = (pos0 + np.arange(n, dtype=np.float64))[:, None] * freqs[None, :]
    c, s = np.cos(ang), np.sin(ang)
    cos64, sin64 = np.concatenate([c, c], axis=1), np.concatenate([-s, s], axis=1)
    tables = (np.tile(cos64, (copies, LANES // QK_ROPE)), np.tile(sin64, (copies, LANES // QK_ROPE)),
              np.tile(np.concatenate([cos64, sin64], axis=1), (copies, 1)))
    return tuple(jnp.asarray(tab, F32) for tab in tables)


def _half_swap(w):
    half = QK_ROPE // 2
    return jnp.concatenate([w[..., half:], w[..., :half]], axis=-1)


def _prepare_params(norm_pre, norm_post, w_in_even, w_pool, pool_scale, sgu_ln_g, sgu_ln_b, w_spatial, b_spatial,
                    w_out_even, w_in_odd, q_norm, kv_norm, w_q_up, w_kv_up, w_o):
    n_odd = w_in_odd.shape[0]
    kr_w = w_in_odd[..., Q_LORA + KV_LORA:Q_LORA + KV_LORA + QK_ROPE]
    win_o = jnp.concatenate([w_in_odd[..., :Q_LORA + KV_LORA], w_in_odd[..., Q_LORA + KV_LORA + QK_ROPE:],
                             kr_w, _half_swap(kr_w)], axis=-1)
    wq_r = w_q_up[..., QK_NOPE:]
    wq = jnp.concatenate([w_q_up[..., :QK_NOPE].reshape(n_odd, Q_LORA, N_HEADS * QK_NOPE),
                          wq_r.reshape(n_odd, Q_LORA, N_HEADS * QK_ROPE),
                          _half_swap(wq_r).reshape(n_odd, Q_LORA, N_HEADS * QK_ROPE)], axis=-1)
    return {
        "norm_pre": norm_pre[:, None, :],
        "norm_post": norm_post[:, None, :],
        "win_e": w_in_even.astype(BF16),
        "wpool": w_pool.astype(BF16),
        "pscale": pool_scale[:, None, :],
        "lng": sgu_ln_g[:, None, :],
        "lnb": sgu_ln_b[:, None, :],
        "ws": w_spatial,
        "bst": jnp.swapaxes(b_spatial, 1, 2),
        "wout_e": w_out_even.astype(BF16),
        "win_o": win_o.astype(BF16),
        "qn": q_norm[:, None, :],
        "kvn": kv_norm[:, None, :],
        "wq": wq.astype(BF16),
        "wuk": jnp.transpose(w_kv_up[..., :QK_NOPE], (0, 2, 3, 1)).astype(BF16),
        "wuv": jnp.transpose(w_kv_up[..., QK_NOPE:], (0, 2, 1, 3)).astype(BF16),
        "wukv": jnp.concatenate([w_kv_up[..., :QK_NOPE].reshape(n_odd, KV_LORA, N_HEADS * QK_NOPE),
                                 w_kv_up[..., QK_NOPE:].reshape(n_odd, KV_LORA, C_MIX)], axis=-1).astype(BF16),
        "wo": w_o.astype(BF16),
    }


def _tile_sizes(t_prompt):
    tt_even = min(t_prompt, 512)
    tt_proj = min(t_prompt, 512)
    tile = min(t_prompt, 1024)
    tt_out = min(t_prompt, 512)
    return tt_even, tt_proj, tile, tt_out


def kernel(x_prompt, x_sample, cache_pool, cache_ckv, cache_krope, norm_pre, norm_post, w_in_even, w_pool, pool_scale, sgu_ln_g, sgu_ln_b, w_spatial, b_spatial, w_out_even, w_in_odd, q_norm, kv_norm, w_q_up, w_kv_up, w_o):
    depth = norm_pre.shape[0]
    b, t, _ = x_prompt.shape
    db, s, _ = x_sample.shape
    past = cache_ckv.shape[2]
    p = _prepare_params(norm_pre, norm_post, w_in_even, w_pool, pool_scale, sgu_ln_g, sgu_ln_b, w_spatial,
                        b_spatial, w_out_even, w_in_odd, q_norm, kv_norm, w_q_up, w_kv_up, w_o)
    tt_even, tt_proj, tile, tt_out = _tile_sizes(t)

    cos_p, sin_p, cs_p = _rope_tables(0, t)
    cos_s, sin_s, _ = _rope_tables(past, s, copies=db)
    cache_krope_t = jnp.swapaxes(cache_krope, 2, 3)

    xp, xs = x_prompt, x_sample
    pool_p, pool_s, sgu_s = [], [], []
    ckv_p, kr_p, ckv_s, kr_s = [], [], [], []
    zero_hist = jnp.zeros((b, HIST_ROWS, W_A), F32)
    for lyr in range(depth):
        if lyr % 2 == 0:
            e = lyr // 2
            xp, hp = _even_layer(xp, zero_hist, lyr, e, p, nb=1, tt=tt_even, pos0=0, emit_v=False,
                                 name=f"even{e}_prompt")
            hist_s = jnp.pad(cache_pool[e], ((0, 0), (HIST_ROWS - POOL_HIST, 0), (0, 0)))
            xs, hs, vs = _even_layer(xs, hist_s, lyr, e, p, nb=db, tt=s, pos0=past, emit_v=True,
                                     name=f"even{e}_sample")
            pool_p.append(hp[:, HIST_ROWS - POOL_HIST:])
            pool_s.append(hs[:, HIST_ROWS - POOL_HIST:])
            sgu_s.append(vs)
        else:
            o = lyr // 2
            qh, kh, vh, ckv, kr, gate = _odd_project_heads(xp, cos_p, sin_p, cs_p, lyr, o, p, tt=tt_proj, tile=tile,
                                                           name=f"proj{o}_prompt")
            o_heads = _flat_attend(qh, kh, vh, tile=tile, name=f"attn{o}_prompt")
            xp = _odd_output(o_heads, gate, xp, lyr, o, p, tt=tt_out, name=f"out{o}_prompt")
            ckv_p.append(ckv)
            kr_p.append(kr)

            qa, qr, ckv, kr, ckvb, krb, gate = _odd_project(xs, cos_s, sin_s, lyr, o, p, nb=db, tt=s,
                                                            name=f"proj{o}_sample")
            o_lat = _cached_attend(qa, qr, cache_ckv, cache_krope_t, ckvb, krb, o, name=f"attn{o}_sample")
            xs = _latent_output(o_lat.reshape(N_HEADS, db * s, KV_LORA), gate.reshape(db * s, C_MIX),
                                xs.reshape(db * s, D_MODEL), lyr, o, p, name=f"out{o}_sample").reshape(db, s, D_MODEL)
            ckv_s.append(ckv)
            kr_s.append(kr)
    return (xp, xs, jnp.stack(pool_p), jnp.stack(pool_s), jnp.stack(sgu_s),
            jnp.stack(ckv_p), jnp.stack(kr_p), jnp.stack(ckv_s), jnp.stack(kr_s))
```

```python
import functools

import jax
import jax.numpy as jnp
import numpy as np
from jax import lax
from jax.experimental import pallas as pl
from jax.experimental.pallas import tpu as pltpu

D_MODEL = 1024
CHUNK = 64
EPS = 1e-6
POOL_WINDOWS = (2, 4, 8, 16)
POOL_HIST = max(POOL_WINDOWS) - 1
W_A = D_MODEL
POOL_GDIM = W_A // len(POOL_WINDOWS)
W_B = D_MODEL
SGU_HEADS = 4
SGU_HDIM = W_B // SGU_HEADS
SGU_CHUNK = 128
EVEN_MIX = W_A + W_B
EVEN_IN = W_A + 2 * W_B + EVEN_MIX
N_HEADS = D_MODEL // 128
QK_NOPE = 128
QK_ROPE = 64
V_DIM = 128
KV_LORA = D_MODEL // 4
Q_LORA = 3 * D_MODEL // 8
C_MIX = N_HEADS * V_DIM
ROPE_BASE = 10000.0
ATTN_SCALE = (QK_NOPE + QK_ROPE) ** -0.5
LOG2E = 1.4426950408889634

LANES = 128
HIST_ROWS = 16
VMEM_LIMIT_BYTES = 56 * 1024 * 1024
FLAT_BLOCK_ROWS = 512
HEAD_LANES = 2 * LANES
BIAS_LANE0 = LANES + QK_ROPE
MASK_BIAS = -1e30

ODD_Q0, ODD_KV0, ODD_G0, ODD_KR0 = 0, Q_LORA, Q_LORA + KV_LORA, Q_LORA + KV_LORA + C_MIX
ODD_COLS = ODD_KR0 + 2 * QK_ROPE
QUP_R0 = N_HEADS * QK_NOPE
QUP_S0 = QUP_R0 + N_HEADS * QK_ROPE
QUP_COLS = QUP_S0 + N_HEADS * QK_ROPE

F32 = jnp.float32
BF16 = jnp.bfloat16


def _dot(a, b):
    return jnp.dot(a, b, preferred_element_type=F32)


def _dot_nt(a, b):
    return lax.dot_general(a, b, (((1,), (1,)), ((), ())), preferred_element_type=F32)


def _rms(x, g):
    return x * lax.rsqrt(jnp.mean(x * x, axis=-1, keepdims=True) + EPS) * g


def _silu(x):
    return x * (1.0 / (1.0 + jnp.exp(-x)))


def _gelu(x):
    return 0.5 * x * (1.0 + lax.erf(x * (2.0 ** -0.5)))


def _even_body(x_ref, hist_ref, gpre_ref, gpost_ref, win_ref, wpool_ref, pscale_ref, lng_ref, lnb_ref,
               ws_ref, bst_ref, wout_ref, *rest, nb, tt, seg, pos0, emit_v, nsub, fused_in):
    rest = list(rest)
    if fused_in:
        oin_ref, gin_ref, wo_ref, gprev_ref = rest[:4]
        rest = rest[4:]
    xo_ref, histo_ref = rest[:2]
    if emit_v:
        vo_ref, aext_ref, mix_ref = rest[2:]
    else:
        aext_ref, mix_ref = rest[2:]
    t = pl.program_id(1)
    nt = pl.num_programs(1)
    m = nb * tt
    rsz = m // nsub
    rps = tt // nsub if nb == 1 else tt
    st = [dict() for _ in range(nsub)]

    @pl.when(t == 0)
    def _():
        aext_ref[:, 0:HIST_ROWS, :] = hist_ref[...]

    def rows(i):
        return slice(i * rsz, (i + 1) * rsz)

    def stage_x(i):
        pending = oin_ref[0, rows(i), :].astype(F32) * _silu(gin_ref[0, rows(i), :].astype(F32))
        xo_ref[0, rows(i), :] = x_ref[0, rows(i), :] + _rms(_dot(pending.astype(BF16), wo_ref[...]), gprev_ref[...])

    def stage_a(i):
        if fused_in:
            x = xo_ref[0, rows(i), :]
        else:
            x = x_ref[0, rows(i), :] if nb == 1 else x_ref[...].reshape(m, D_MODEL)
        h = _rms(x, gpre_ref[...]).astype(BF16)
        a = _dot(h, win_ref[:, 0:W_A])
        if nb == 1:
            aext_ref[0, HIST_ROWS + i * rsz:HIST_ROWS + (i + 1) * rsz, :] = a
        else:
            aext_ref[:, HIST_ROWS:HIST_ROWS + tt, :] = a.reshape(nb, tt, W_A)
        st[i]["h"] = h

    def stage_uv(i):
        st[i]["uv"] = _dot(st[i]["h"], win_ref[:, W_A:W_A + 2 * W_B])

    def stage_gate(i):
        st[i]["gate"] = _dot(st[i].pop("h"), win_ref[:, W_A + 2 * W_B:])

    def stage_pool(i):
        r0 = HIST_ROWS + (i * rsz if nb == 1 else 0)
        pos = pos0 + t * tt + (r0 - HIST_ROWS) + lax.broadcasted_iota(jnp.int32, (rps, 1), 0)
        for g, w in enumerate(POOL_WINDOWS):
            cs = slice(g * POOL_GDIM, (g + 1) * POOL_GDIM)
            inv_cnt = 1.0 / jnp.minimum(pos + 1, w).astype(F32)
            for s in range(nb):
                wsum = aext_ref[s, r0 - HIST_ROWS:r0 + rps, cs]
                k = 1
                while k < w:
                    wsum = wsum + pltpu.roll(wsum, k, axis=0)
                    k *= 2
                cur = aext_ref[s, r0:r0 + rps, cs]
                mix_ref[i * rsz + s * rps:i * rsz + (s + 1) * rps, cs] = wsum[HIST_ROWS:] * inv_cnt - cur
            y_a = _dot(mix_ref[rows(i), cs].astype(BF16), wpool_ref[g]) * pscale_ref[:, cs]
            mix_ref[rows(i), cs] = y_a

    def stage_sgu(i):
        uv = _gelu(st[i].pop("uv"))
        u = uv[:, :W_B]
        v = uv[:, W_B:]
        vc = v - jnp.mean(v, axis=-1, keepdims=True)
        vn = vc * lax.rsqrt(jnp.mean(vc * vc, axis=-1, keepdims=True) + EPS) * lng_ref[...] + lnb_ref[...]
        if emit_v:
            vo_ref[...] = vn.reshape(nb, tt, W_B)
        vb = vn.astype(BF16)
        ii = lax.broadcasted_iota(jnp.int32, (seg, seg), 0)
        jj = lax.broadcasted_iota(jnp.int32, (seg, seg), 1)
        causal = (jj // CHUNK) <= (ii // CHUNK)
        for g in range(SGU_HEADS):
            hs = slice(g * SGU_HDIM, (g + 1) * SGU_HDIM)
            wsg = jnp.where(causal, ws_ref[g, 0:seg, 0:seg], 0.0).astype(BF16)
            bcol = bst_ref[0:seg, g:g + 1]
            for c in range(rsz // seg):
                rs = slice(c * seg, (c + 1) * seg)
                mixed = _dot(wsg, vb[rs, hs]) + bcol
                mix_ref[i * rsz + c * seg:i * rsz + (c + 1) * seg, W_A + g * SGU_HDIM:W_A + (g + 1) * SGU_HDIM] = (
                    u[rs, hs] * mixed)

    def stage_out(i):
        mixv = (mix_ref[rows(i)] * _silu(st[i].pop("gate"))).astype(BF16)
        y = _rms(_dot(mixv, wout_ref[...]), gpost_ref[...])
        if nb == 1:
            x_in = xo_ref if fused_in else x_ref
            xo_ref[0, rows(i), :] = x_in[0, rows(i), :] + y
        else:
            xo_ref[...] = x_ref[...] + y.reshape(nb, tt, D_MODEL)

    if nsub == 1:
        order = [(stage_a, 0), (stage_uv, 0), (stage_pool, 0), (stage_gate, 0), (stage_sgu, 0), (stage_out, 0)]
    else:
        order = [(stage_a, 0), (stage_uv, 0), (stage_a, 1), (stage_sgu, 0), (stage_uv, 1), (stage_pool, 0),
                 (stage_gate, 0), (stage_sgu, 1), (stage_pool, 1), (stage_out, 0), (stage_gate, 1), (stage_out, 1)]
    if fused_in:
        order = [(stage_x, i) for i in range(nsub)] + order
    for stage, i in order:
        stage(i)

    tail = aext_ref[:, tt:tt + HIST_ROWS, :]

    @pl.when(t == nt - 1)
    def _():
        histo_ref[...] = tail

    aext_ref[:, 0:HIST_ROWS, :] = tail


def _resident(block_shape, index_map):
    return pl.BlockSpec(block_shape, index_map, pipeline_mode=pl.Buffered(1))


def _even_layer(x, hist, lyr, e, p, *, nb, tt, pos0, emit_v, name, pending=None):
    bx, tx, _ = x.shape
    seg = min(tx, SGU_CHUNK)
    assert bx % nb == 0 and tx % tt == 0 and tt % seg == 0 and tt >= HIST_ROWS and tt % 8 == 0
    assert pending is None or nb == 1
    grid = (bx // nb, tx // tt)
    row = lambda b, t: (b, t, 0)
    first = lambda b, t: (b, 0, 0)
    in_specs = [
        pl.BlockSpec((nb, tt, D_MODEL), row),
        pl.BlockSpec((nb, HIST_ROWS, W_A), first),
        _resident((None, 1, D_MODEL), lambda b, t: (lyr, 0, 0)),
        _resident((None, 1, D_MODEL), lambda b, t: (lyr, 0, 0)),
        _resident((None, D_MODEL, EVEN_IN), lambda b, t: (e, 0, 0)),
        _resident((None, len(POOL_WINDOWS), POOL_GDIM, POOL_GDIM), lambda b, t: (e, 0, 0, 0)),
        _resident((None, 1, W_A), lambda b, t: (e, 0, 0)),
        _resident((None, 1, W_B), lambda b, t: (e, 0, 0)),
        _resident((None, 1, W_B), lambda b, t: (e, 0, 0)),
        _resident((None, SGU_HEADS, SGU_CHUNK, SGU_CHUNK), lambda b, t: (e, 0, 0, 0)),
        _resident((None, SGU_CHUNK, SGU_HEADS), lambda b, t: (e, 0, 0)),
        _resident((None, EVEN_MIX, D_MODEL), lambda b, t: (e, 0, 0)),
    ]
    operands = [x, hist, p["norm_pre"], p["norm_post"], p["win_e"], p["wpool"], p["pscale"], p["lng"], p["lnb"],
                p["ws"], p["bst"], p["wout_e"]]
    if pending is not None:
        o_heads, gate, o = pending
        in_specs += [pl.BlockSpec((1, tt, C_MIX), row), pl.BlockSpec((1, tt, C_MIX), row),
                     _resident((None, C_MIX, D_MODEL), lambda b, t: (o, 0, 0)),
                     _resident((None, 1, D_MODEL), lambda b, t: (lyr - 1, 0, 0))]
        operands += [o_heads, gate, p["wo"], p["norm_post"]]
    out_shape = [jax.ShapeDtypeStruct((bx, tx, D_MODEL), F32), jax.ShapeDtypeStruct((bx, HIST_ROWS, W_A), F32)]
    out_specs = [pl.BlockSpec((nb, tt, D_MODEL), row), pl.BlockSpec((nb, HIST_ROWS, W_A), first)]
    if emit_v:
        out_shape.append(jax.ShapeDtypeStruct((bx, tx, W_B), F32))
        out_specs.append(pl.BlockSpec((nb, tt, W_B), row))
    nsub = 2 if nb == 1 and tt % (2 * seg) == 0 else 1
    body = functools.partial(_even_body, nb=nb, tt=tt, seg=seg, pos0=pos0, emit_v=emit_v, nsub=nsub,
                             fused_in=pending is not None)
    return pl.pallas_call(
        body,
        grid=grid,
        in_specs=in_specs,
        out_specs=out_specs,
        out_shape=out_shape,
        scratch_shapes=[pltpu.VMEM((nb, HIST_ROWS + tt, W_A), F32), pltpu.VMEM((nb * tt, EVEN_MIX), F32)],
        compiler_params=pltpu.CompilerParams(
            dimension_semantics=("arbitrary", "arbitrary"), vmem_limit_bytes=VMEM_LIMIT_BYTES),
        name=name,
    )(*operands)


def _proj_body(x_ref, gpre_ref, win_ref, qn_ref, kvn_ref, wq_ref, wuk_ref, cos_ref, sin_ref,
               qa_ref, qr_ref, ckv_ref, kr_ref, ckvb_ref, krb_ref, gate_ref, *, nb, tt):
    m = nb * tt
    x = x_ref[...].reshape(m, D_MODEL)
    h = _rms(x, gpre_ref[...]).astype(BF16)
    z = _dot(h, win_ref[...])
    gate_ref[...] = z[:, ODD_G0:ODD_KR0].reshape(nb, tt, C_MIX)

    ckv = _rms(z[:, ODD_KV0:ODD_G0], kvn_ref[...])
    ckv_ref[...] = ckv.reshape(nb, tt, KV_LORA)
    ckvb_ref[...] = ckv.astype(BF16).reshape(nb, tt, KV_LORA)

    cos = cos_ref[...]
    sin = sin_ref[...]
    kr = (z[:, ODD_KR0:ODD_KR0 + QK_ROPE] * cos[:, :QK_ROPE]
          + z[:, ODD_KR0 + QK_ROPE:ODD_COLS] * sin[:, :QK_ROPE])
    kr_ref[...] = kr.reshape(nb, tt, QK_ROPE)
    krb_ref[...] = kr.astype(BF16).reshape(nb, tt, QK_ROPE)

    qcn = _rms(z[:, ODD_Q0:ODD_KV0], qn_ref[...]).astype(BF16)
    q = _dot(qcn, wq_ref[...])
    c_exp = ATTN_SCALE * LOG2E
    heads_per_vreg = LANES // QK_ROPE
    for c in range(N_HEADS // heads_per_vreg):
        r = ((q[:, QUP_R0 + c * LANES:QUP_R0 + (c + 1) * LANES] * cos
              + q[:, QUP_S0 + c * LANES:QUP_S0 + (c + 1) * LANES] * sin) * c_exp).astype(BF16)
        for i in range(heads_per_vreg):
            qr_ref[:, heads_per_vreg * c + i, :, :] = r[:, i * QK_ROPE:(i + 1) * QK_ROPE].reshape(nb, tt, QK_ROPE)
    for hh in range(N_HEADS):
        q_nope = q[:, hh * QK_NOPE:(hh + 1) * QK_NOPE].astype(BF16)
        qa_ref[:, hh, :, :] = (_dot(q_nope, wuk_ref[hh]) * c_exp).astype(BF16).reshape(nb, tt, KV_LORA)


def _proj_heads_body(x_ref, gpre_ref, win_ref, qn_ref, kvn_ref, wq_ref, wukv_ref, cos_ref, sin_ref, cs_ref,
                     q_ref, k_ref, v_ref, ckv_ref, kr_ref, gate_ref, *, tt, tile):
    x = x_ref[0]
    h = _rms(x, gpre_ref[...]).astype(BF16)
    z = _dot(h, win_ref[...])
    gate_ref[0] = z[:, ODD_G0:ODD_KR0].astype(BF16)
    ckv = _rms(z[:, ODD_KV0:ODD_G0], kvn_ref[...])
    ckv_ref[0] = ckv

    lane = lax.broadcasted_iota(jnp.int32, (tt, LANES), 1)
    low = lane < QK_ROPE
    u = z[:, ODD_KR0:ODD_COLS] * cs_ref[...]
    kr_full = u + pltpu.roll(u, QK_ROPE, axis=1)
    kr_ref[0] = kr_full[:, :QK_ROPE]
    pos = pl.program_id(1) * tt + lax.broadcasted_iota(jnp.int32, (tt, LANES), 0)
    onehot = (lane - QK_ROPE == (pos % tile) // CHUNK).astype(F32)
    k_hi = jnp.where(low, kr_full, onehot).astype(BF16)
    v_hi = jnp.ones((tt, LANES), BF16)

    kv = _dot(ckv.astype(BF16), wukv_ref[...])
    qcn = _rms(z[:, ODD_Q0:ODD_KV0], qn_ref[...]).astype(BF16)
    q = _dot(qcn, wq_ref[...])
    c_exp = ATTN_SCALE * LOG2E
    heads_per_vreg = LANES // QK_ROPE
    for hh in range(N_HEADS):
        c, i = divmod(hh, heads_per_vreg)
        r = (q[:, QUP_R0 + c * LANES:QUP_R0 + (c + 1) * LANES] * cos_ref[...]
             + q[:, QUP_S0 + c * LANES:QUP_S0 + (c + 1) * LANES] * sin_ref[...]) * c_exp
        if i:
            r = pltpu.roll(r, LANES - i * QK_ROPE, axis=1)
        q_ref[0, hh, :, 0:LANES] = (q[:, hh * QK_NOPE:(hh + 1) * QK_NOPE] * c_exp).astype(BF16)
        q_ref[0, hh, :, LANES:2 * LANES] = jnp.where(low, r, 0.0).astype(BF16)
        k_ref[0, hh, :, 0:LANES] = kv[:, hh * QK_NOPE:(hh + 1) * QK_NOPE].astype(BF16)
        k_ref[0, hh, :, LANES:2 * LANES] = k_hi
        v_ref[0, hh, :, 0:LANES] = kv[:, C_MIX + hh * V_DIM:C_MIX + (hh + 1) * V_DIM].astype(BF16)
        v_ref[0, hh, :, LANES:2 * LANES] = v_hi


def _odd_project_heads(x, cos, sin, cs, lyr, o, p, *, tt, tile, name):
    bx, tx, _ = x.shape
    assert tx % tt == 0 and tt % 16 == 0 and tile % CHUNK == 0 and tile // CHUNK <= LANES - QK_ROPE
    grid = (bx, tx // tt)
    row = lambda b, t: (b, t, 0)
    hrow = lambda b, t: (b, 0, t, 0)
    tab = pl.BlockSpec((tt, LANES), lambda b, t: (t, 0))
    in_specs = [
        pl.BlockSpec((1, tt, D_MODEL), row),
        _resident((None, 1, D_MODEL), lambda b, t: (lyr, 0, 0)),
        _resident((None, D_MODEL, ODD_COLS), lambda b, t: (o, 0, 0)),
        _resident((None, 1, Q_LORA), lambda b, t: (o, 0, 0)),
        _resident((None, 1, KV_LORA), lambda b, t: (o, 0, 0)),
        _resident((None, Q_LORA, QUP_COLS), lambda b, t: (o, 0, 0)),
        _resident((None, KV_LORA, 2 * C_MIX), lambda b, t: (o, 0, 0)),
        tab, tab, tab,
    ]
    head_arr = jax.ShapeDtypeStruct((bx, N_HEADS, tx, HEAD_LANES), BF16)
    head_spec = pl.BlockSpec((1, N_HEADS, tt, HEAD_LANES), hrow)
    out_shape = [head_arr, head_arr, head_arr,
                 jax.ShapeDtypeStruct((bx, tx, KV_LORA), F32),
                 jax.ShapeDtypeStruct((bx, tx, QK_ROPE), F32),
                 jax.ShapeDtypeStruct((bx, tx, C_MIX), BF16)]
    out_specs = [head_spec, head_spec, head_spec,
                 pl.BlockSpec((1, tt, KV_LORA), row),
                 pl.BlockSpec((1, tt, QK_ROPE), row),
                 pl.BlockSpec((1, tt, C_MIX), row)]
    return pl.pallas_call(
        functools.partial(_proj_heads_body, tt=tt, tile=tile),
        grid=grid,
        in_specs=in_specs,
        out_specs=out_specs,
        out_shape=out_shape,
        compiler_params=pltpu.CompilerParams(
            dimension_semantics=("arbitrary", "arbitrary"), vmem_limit_bytes=VMEM_LIMIT_BYTES),
        name=name,
    )(x, p["norm_pre"], p["win_o"], p["qn"], p["kvn"], p["wq"], p["wukv"], cos, sin, cs)


def _flat_attn_body(q_ref, k_ref, v_ref, o_ref, m_ref, acc_ref, s0_ref, s1_ref, dbias_ref, *, tile, nq, rb):
    nblk = tile // rb
    row = lax.broadcasted_iota(jnp.int32, (tile, HEAD_LANES), 0)
    idx = lax.broadcasted_iota(jnp.int32, (tile, HEAD_LANES), 1) - BIAS_LANE0
    hidden = jnp.logical_and(jnp.logical_and(idx >= 0, idx < tile // CHUNK), idx > row // CHUNK)
    dbias_ref[...] = jnp.where(hidden, MASK_BIAS, 0.0).astype(BF16)
    acc_ref[...] = jnp.zeros(acc_ref.shape, F32)

    def scores(b, qi, j, s_ref):
        k = k_ref[0, 0, pl.ds(pl.multiple_of(j * tile, tile), tile), :]
        q = q_ref[0, 0, pl.ds(pl.multiple_of(qi * tile + b * rb, rb), rb), :]
        q = q + jnp.where(qi == j, dbias_ref[b * rb:(b + 1) * rb], jnp.zeros((rb, HEAD_LANES), BF16))
        s_ref[b * rb:(b + 1) * rb] = _dot_nt(q, k)

    def softmax_pv(b, qi, j, s_ref):
        v = v_ref[0, 0, pl.ds(pl.multiple_of(j * tile, tile), tile), :]
        rs = slice(b * rb, (b + 1) * rb)
        s = s_ref[rs]
        m_old = jnp.where(j == 0, -1e30, m_ref[rs])
        m_new = jnp.maximum(m_old, jnp.max(s, axis=-1, keepdims=True))
        alpha = jnp.exp2(m_old - m_new)
        p = jnp.exp2(s - jnp.concatenate([m_new] * (tile // LANES), axis=1))
        acc = (acc_ref[rs] * jnp.concatenate([alpha] * (HEAD_LANES // LANES), axis=1)
               + _dot(p.astype(BF16), v))
        acc_ref[rs] = acc
        m_ref[rs] = m_new
        o_ref[0, pl.ds(pl.multiple_of(qi * tile + b * rb, rb), rb), :] = (
            acc[:, :V_DIM] * (1.0 / acc[:, V_DIM:])).astype(BF16)

    def step(qi, j, s_cur, s_next):
        wrap = j == qi
        qn = jnp.minimum(jnp.where(wrap, qi + 1, qi), nq - 1)
        jn = jnp.where(wrap, 0, j + 1)
        for b in range(nblk):
            scores(b, qn, jn, s_next)
        for b in range(nblk):
            softmax_pv(b, qi, j, s_cur)
        return qn, jn

    def two_steps(_, carry):
        return step(*step(*carry, s0_ref, s1_ref), s1_ref, s0_ref)

    for b in range(nblk):
        scores(b, 0, 0, s0_ref)
    n_steps = nq * (nq + 1) // 2
    n_pairs = n_steps // 2
    unroll = max(u for u in (1, 2, 3) if n_pairs % u == 0)
    carry = lax.fori_loop(0, n_pairs, two_steps, (jnp.int32(0), jnp.int32(0)), unroll=unroll)
    if n_steps % 2:
        step(*carry, s0_ref, s1_ref)


def _flat_attend(q, k, v, *, tile, name):
    bx, nh, tx, _ = q.shape
    assert tx % tile == 0 and tile % FLAT_BLOCK_ROWS == 0
    head = pl.BlockSpec((1, 1, tx, HEAD_LANES), lambda b, hh: (b, hh, 0, 0))
    return pl.pallas_call(
        functools.partial(_flat_attn_body, tile=tile, nq=tx // tile, rb=FLAT_BLOCK_ROWS),
        grid=(bx, nh),
        in_specs=[head, head, head],
        out_specs=pl.BlockSpec((1, tx, V_DIM), lambda b, hh: (b, 0, hh)),
        out_shape=jax.ShapeDtypeStruct((bx, tx, nh * V_DIM), BF16),
        scratch_shapes=[pltpu.VMEM((tile, LANES), F32), pltpu.VMEM((tile, HEAD_LANES), F32),
                        pltpu.VMEM((tile, tile), F32), pltpu.VMEM((tile, tile), F32),
                        pltpu.VMEM((tile, HEAD_LANES), BF16)],
        compiler_params=pltpu.CompilerParams(
            dimension_semantics=("arbitrary", "arbitrary"), vmem_limit_bytes=VMEM_LIMIT_BYTES),
        name=name,
    )(q, k, v)


def _out_body(o_ref, gate_ref, x_ref, wo_ref, gpost_ref, xo_ref):
    y = _dot((o_ref[0].astype(F32) * _silu(gate_ref[0].astype(F32))).astype(BF16), wo_ref[...])
    xo_ref[0] = x_ref[0] + _rms(y, gpost_ref[...])


def _odd_output(o_heads, gate, x, lyr, o, p, *, tt, name):
    bx, tx, _ = x.shape
    assert tx % tt == 0
    row = lambda b, t: (b, t, 0)
    return pl.pallas_call(
        _out_body,
        grid=(bx, tx // tt),
        in_specs=[pl.BlockSpec((1, tt, C_MIX), row), pl.BlockSpec((1, tt, C_MIX), row),
                  pl.BlockSpec((1, tt, D_MODEL), row),
                  _resident((None, C_MIX, D_MODEL), lambda b, t: (o, 0, 0)),
                  _resident((None, 1, D_MODEL), lambda b, t: (lyr, 0, 0))],
        out_specs=pl.BlockSpec((1, tt, D_MODEL), row),
        out_shape=jax.ShapeDtypeStruct((bx, tx, D_MODEL), F32),
        compiler_params=pltpu.CompilerParams(
            dimension_semantics=("arbitrary", "arbitrary"), vmem_limit_bytes=VMEM_LIMIT_BYTES),
        name=name,
    )(o_heads, gate, x, p["wo"], p["norm_post"])


def _odd_project(x, cos, sin, lyr, o, p, *, nb, tt, name):
    bx, tx, _ = x.shape
    assert bx % nb == 0 and tx % tt == 0 and tt % 16 == 0
    grid = (bx // nb, tx // tt)
    row = lambda b, t: (b, t, 0)
    hrow = lambda b, t: (b, 0, t, 0)
    in_specs = [
        pl.BlockSpec((nb, tt, D_MODEL), row),
        _resident((None, 1, D_MODEL), lambda b, t: (lyr, 0, 0)),
        _resident((None, D_MODEL, ODD_COLS), lambda b, t: (o, 0, 0)),
        _resident((None, 1, Q_LORA), lambda b, t: (o, 0, 0)),
        _resident((None, 1, KV_LORA), lambda b, t: (o, 0, 0)),
        _resident((None, Q_LORA, QUP_COLS), lambda b, t: (o, 0, 0)),
        _resident((None, N_HEADS, QK_NOPE, KV_LORA), lambda b, t: (o, 0, 0, 0)),
        pl.BlockSpec((nb * tt, LANES), lambda b, t: (t, 0)),
        pl.BlockSpec((nb * tt, LANES), lambda b, t: (t, 0)),
    ]
    out_shape = [
        jax.ShapeDtypeStruct((bx, N_HEADS, tx, KV_LORA), BF16),
        jax.ShapeDtypeStruct((bx, N_HEADS, tx, QK_ROPE), BF16),
        jax.ShapeDtypeStruct((bx, tx, KV_LORA), F32),
        jax.ShapeDtypeStruct((bx, tx, QK_ROPE), F32),
        jax.ShapeDtypeStruct((bx, tx, KV_LORA), BF16),
        jax.ShapeDtypeStruct((bx, tx, QK_ROPE), BF16),
        jax.ShapeDtypeStruct((bx, tx, C_MIX), F32),
    ]
    out_specs = [
        pl.BlockSpec((nb, N_HEADS, tt, KV_LORA), hrow),
        pl.BlockSpec((nb, N_HEADS, tt, QK_ROPE), hrow),
        pl.BlockSpec((nb, tt, KV_LORA), row),
        pl.BlockSpec((nb, tt, QK_ROPE), row),
        pl.BlockSpec((nb, tt, KV_LORA), row),
        pl.BlockSpec((nb, tt, QK_ROPE), row),
        pl.BlockSpec((nb, tt, C_MIX), row),
    ]
    return pl.pallas_call(
        functools.partial(_proj_body, nb=nb, tt=tt),
        grid=grid,
        in_specs=in_specs,
        out_specs=out_specs,
        out_shape=out_shape,
        compiler_params=pltpu.CompilerParams(
            dimension_semantics=("arbitrary", "arbitrary"), vmem_limit_bytes=VMEM_LIMIT_BYTES),
        name=name,
    )(x, p["norm_pre"], p["win_o"], p["qn"], p["kvn"], p["wq"], p["wuk"], cos, sin)


def _cached_attn_body(qa_ref, qr_ref, cckv_ref, ckrt_ref, nckv_ref, nkr_ref, o_ref, *, tq, past):
    rows = N_HEADS * tq
    qa = qa_ref[0].reshape(rows, KV_LORA)
    qr = qr_ref[0].reshape(rows, QK_ROPE)
    kc = cckv_ref[0].astype(BF16)
    kn = nckv_ref[0]
    s_old = _dot_nt(qa, kc) + _dot(qr, ckrt_ref[0].astype(BF16))
    s_new = _dot_nt(qa, kn) + _dot_nt(qr, nkr_ref[0])
    kpos = past + lax.broadcasted_iota(jnp.int32, (1, tq, tq), 2)
    qpos = past + lax.broadcasted_iota(jnp.int32, (1, tq, tq), 1)
    s_new = jnp.where(kpos // CHUNK <= qpos // CHUNK, s_new.reshape(N_HEADS, tq, tq), -jnp.inf).reshape(rows, tq)
    m = jnp.maximum(jnp.max(s_old, axis=-1, keepdims=True), jnp.max(s_new, axis=-1, keepdims=True))
    p_old = jnp.exp2(s_old - m)
    p_new = jnp.exp2(s_new - m)
    l = jnp.sum(p_old, axis=-1, keepdims=True) + jnp.sum(p_new, axis=-1, keepdims=True)
    o_lat = (_dot(p_old.astype(BF16), kc) + _dot(p_new.astype(BF16), kn)) * (1.0 / l)
    o_ref[:, 0] = o_lat.astype(BF16).reshape(N_HEADS, tq, KV_LORA)


def _cached_attend(qa, qr, cache_ckv, cache_krope_t, ckvb, krb, o, *, name):
    bx, _, tq, _ = qa.shape
    past = cache_ckv.shape[2]
    assert tq % 16 == 0
    in_specs = [
        pl.BlockSpec((1, N_HEADS, tq, KV_LORA), lambda b: (b, 0, 0, 0)),
        pl.BlockSpec((1, N_HEADS, tq, QK_ROPE), lambda b: (b, 0, 0, 0)),
        pl.BlockSpec((None, 1, past, KV_LORA), lambda b: (o, b, 0, 0)),
        pl.BlockSpec((None, 1, QK_ROPE, past), lambda b: (o, b, 0, 0)),
        pl.BlockSpec((1, tq, KV_LORA), lambda b: (b, 0, 0)),
        pl.BlockSpec((1, tq, QK_ROPE), lambda b: (b, 0, 0)),
    ]
    return pl.pallas_call(
        functools.partial(_cached_attn_body, tq=tq, past=past),
        grid=(bx,),
        in_specs=in_specs,
        out_specs=pl.BlockSpec((N_HEADS, 1, tq, KV_LORA), lambda b: (0, b, 0, 0)),
        out_shape=jax.ShapeDtypeStruct((N_HEADS, bx, tq, KV_LORA), BF16),
        compiler_params=pltpu.CompilerParams(
            dimension_semantics=("arbitrary",), vmem_limit_bytes=VMEM_LIMIT_BYTES),
        name=name,
    )(qa, qr, cache_ckv, cache_krope_t, ckvb, krb)


def _latent_out_body(olat_ref, gate_ref, x_ref, wuv_ref, wo_ref, gpost_ref, xo_ref):
    o = jnp.concatenate([_dot(olat_ref[hh], wuv_ref[hh]) for hh in range(N_HEADS)], axis=1)
    y = _dot((o * _silu(gate_ref[...])).astype(BF16), wo_ref[...])
    xo_ref[...] = x_ref[...] + _rms(y, gpost_ref[...])


def _latent_output(o_lat, gate, x, lyr, o, p, *, name):
    m = x.shape[0]
    whole = lambda shape: pl.BlockSpec(shape, lambda i: (0,) * len(shape))
    return pl.pallas_call(
        _latent_out_body,
        grid=(1,),
        in_specs=[whole((N_HEADS, m, KV_LORA)), whole((m, C_MIX)), whole((m, D_MODEL)),
                  pl.BlockSpec((None, N_HEADS, KV_LORA, V_DIM), lambda i: (o, 0, 0, 0)),
                  pl.BlockSpec((None, C_MIX, D_MODEL), lambda i: (o, 0, 0)),
                  pl.BlockSpec((None, 1, D_MODEL), lambda i: (lyr, 0, 0))],
        out_specs=whole((m, D_MODEL)),
        out_shape=jax.ShapeDtypeStruct((m, D_MODEL), F32),
        compiler_params=pltpu.CompilerParams(
            dimension_semantics=("arbitrary",), vmem_limit_bytes=VMEM_LIMIT_BYTES),
        name=name,
    )(o_lat, gate, x, p["wuv"], p["wo"], p["norm_post"])


def _rope_tables(pos0, n, copies=1):
    half = QK_ROPE // 2
    freqs = ROPE_BASE ** (-np.arange(half, dtype=np.float64) / half)
    ang = (pos0 + np.arange(n, dtype=np.float64))[:, None] * freqs[None, :]
    c, s = np.cos(ang), np.sin(ang)
    cos64, sin64 = np.concatenate([c, c], axis=1), np.concatenate([-s, s], axis=1)
    tables = (np.tile(cos64, (copies, LANES // QK_ROPE)), np.tile(sin64, (copies, LANES // QK_ROPE)),
              np.tile(np.concatenate([cos64, sin64], axis=1), (copies, 1)))
    return tuple(jnp.asarray(tab, F32) for tab in tables)


def _half_swap(w):
    half = QK_ROPE // 2
    return jnp.concatenate([w[..., half:], w[..., :half]], axis=-1)


def _prepare_params(norm_pre, norm_post, w_in_even, w_pool, pool_scale, sgu_ln_g, sgu_ln_b, w_spatial, b_spatial,
                    w_out_even, w_in_odd, q_norm, kv_norm, w_q_up, w_kv_up, w_o):
    n_odd = w_in_odd.shape[0]
    kr_w = w_in_odd[..., Q_LORA + KV_LORA:Q_LORA + KV_LORA + QK_ROPE]
    win_o = jnp.concatenate([w_in_odd[..., :Q_LORA + KV_LORA], w_in_odd[..., Q_LORA + KV_LORA + QK_ROPE:],
                             kr_w, _half_swap(kr_w)], axis=-1)
    wq_r = w_q_up[..., QK_NOPE:]
    wq = jnp.concatenate([w_q_up[..., :QK_NOPE].reshape(n_odd, Q_LORA, N_HEADS * QK_NOPE),
                          wq_r.reshape(n_odd, Q_LORA, N_HEADS * QK_ROPE),
                          _half_swap(wq_r).reshape(n_odd, Q_LORA, N_HEADS * QK_ROPE)], axis=-1)
    return {
        "norm_pre": norm_pre[:, None, :],
        "norm_post": norm_post[:, None, :],
        "win_e": w_in_even.astype(BF16),
        "wpool": w_pool.astype(BF16),
        "pscale": pool_scale[:, None, :],
        "lng": sgu_ln_g[:, None, :],
        "lnb": sgu_ln_b[:, None, :],
        "ws": w_spatial,
        "bst": jnp.swapaxes(b_spatial, 1, 2),
        "wout_e": w_out_even.astype(BF16),
        "win_o": win_o.astype(BF16),
        "qn": q_norm[:, None, :],
        "kvn": kv_norm[:, None, :],
        "wq": wq.astype(BF16),
        "wuk": jnp.transpose(w_kv_up[..., :QK_NOPE], (0, 2, 3, 1)).astype(BF16),
        "wuv": jnp.transpose(w_kv_up[..., QK_NOPE:], (0, 2, 1, 3)).astype(BF16),
        "wukv": jnp.concatenate([w_kv_up[..., :QK_NOPE].reshape(n_odd, KV_LORA, N_HEADS * QK_NOPE),
                                 w_kv_up[..., QK_NOPE:].reshape(n_odd, KV_LORA, C_MIX)], axis=-1).astype(BF16),
        "wo": w_o.astype(BF16),
    }


def _tile_sizes(t_prompt):
    tt_even = min(t_prompt, 512)
    tt_proj = min(t_prompt, 512)
    tile = min(t_prompt, 1024)
    tt_out = min(t_prompt, 1024)
    return tt_even, tt_proj, tile, tt_out


def kernel(x_prompt, x_sample, cache_pool, cache_ckv, cache_krope, norm_pre, norm_post, w_in_even, w_pool, pool_scale, sgu_ln_g, sgu_ln_b, w_spatial, b_spatial, w_out_even, w_in_odd, q_norm, kv_norm, w_q_up, w_kv_up, w_o):
    depth = norm_pre.shape[0]
    b, t, _ = x_prompt.shape
    db, s, _ = x_sample.shape
    past = cache_ckv.shape[2]
    p = _prepare_params(norm_pre, norm_post, w_in_even, w_pool, pool_scale, sgu_ln_g, sgu_ln_b, w_spatial,
                        b_spatial, w_out_even, w_in_odd, q_norm, kv_norm, w_q_up, w_kv_up, w_o)
    tt_even, tt_proj, tile, tt_out = _tile_sizes(t)

    cos_p, sin_p, cs_p = _rope_tables(0, t)
    cos_s, sin_s, _ = _rope_tables(past, s, copies=db)
    cache_krope_t = jnp.swapaxes(cache_krope, 2, 3)

    xp, xs = x_prompt, x_sample
    pool_p, pool_s, sgu_s = [], [], []
    ckv_p, kr_p, ckv_s, kr_s = [], [], [], []
    zero_hist = jnp.zeros((b, HIST_ROWS, W_A), F32)
    pending = None
    for lyr in range(depth):
        if lyr % 2 == 0:
            e = lyr // 2
            xp, hp = _even_layer(xp, zero_hist, lyr, e, p, nb=1, tt=tt_even, pos0=0, emit_v=False,
                                 name=f"even{e}_prompt", pending=pending)
            pending = None
            hist_s = jnp.pad(cache_pool[e], ((0, 0), (HIST_ROWS - POOL_HIST, 0), (0, 0)))
            xs, hs, vs = _even_layer(xs, hist_s, lyr, e, p, nb=db, tt=s, pos0=past, emit_v=True,
                                     name=f"even{e}_sample")
            pool_p.append(hp[:, HIST_ROWS - POOL_HIST:])
            pool_s.append(hs[:, HIST_ROWS - POOL_HIST:])
            sgu_s.append(vs)
        else:
            o = lyr // 2
            qh, kh, vh, ckv, kr, gate = _odd_project_heads(xp, cos_p, sin_p, cs_p, lyr, o, p, tt=tt_proj, tile=tile,
                                                           name=f"proj{o}_prompt")
            o_heads = _flat_attend(qh, kh, vh, tile=tile, name=f"attn{o}_prompt")
            if lyr + 1 < depth:
                pending = (o_heads, gate, o)
            else:
                xp = _odd_output(o_heads, gate, xp, lyr, o, p, tt=tt_out, name=f"out{o}_prompt")
            ckv_p.append(ckv)
            kr_p.append(kr)

            qa, qr, ckv, kr, ckvb, krb, gate = _odd_project(xs, cos_s, sin_s, lyr, o, p, nb=db, tt=s,
                                                            name=f"proj{o}_sample")
            o_lat = _cached_attend(qa, qr, cache_ckv, cache_krope_t, ckvb, krb, o, name=f"attn{o}_sample")
            xs = _latent_output(o_lat.reshape(N_HEADS, db * s, KV_LORA), gate.reshape(db * s, C_MIX),
                                xs.reshape(db * s, D_MODEL), lyr, o, p, name=f"out{o}_sample").reshape(db, s, D_MODEL)
            ckv_s.append(ckv)
            kr_s.append(kr)
    return (xp, xs, jnp.stack(pool_p), jnp.stack(pool_s), jnp.stack(sgu_s),
            jnp.stack(ckv_p), jnp.stack(kr_p), jnp.stack(ckv_s), jnp.stack(kr_s))
```

```python
import functools

import jax
import jax.numpy as jnp
import numpy as np
from jax import lax
from jax.experimental import pallas as pl
from jax.experimental.pallas import tpu as pltpu

D_MODEL = 1024
CHUNK = 64
EPS = 1e-6
POOL_WINDOWS = (2, 4, 8, 16)
POOL_HIST = max(POOL_WINDOWS) - 1
W_A = D_MODEL
POOL_GDIM = W_A // len(POOL_WINDOWS)
W_B = D_MODEL
SGU_HEADS = 4
SGU_HDIM = W_B // SGU_HEADS
SGU_CHUNK = 128
EVEN_MIX = W_A + W_B
EVEN_IN = W_A + 2 * W_B + EVEN_MIX
N_HEADS = D_MODEL // 128
QK_NOPE = 128
QK_ROPE = 64
V_DIM = 128
KV_LORA = D_MODEL // 4
Q_LORA = 3 * D_MODEL // 8
C_MIX = N_HEADS * V_DIM
ROPE_BASE = 10000.0
ATTN_SCALE = (QK_NOPE + QK_ROPE) ** -0.5
LOG2E = 1.4426950408889634

LANES = 128
HIST_ROWS = 16
VMEM_LIMIT_BYTES = 56 * 1024 * 1024
FLAT_BLOCK_ROWS = 512
HEAD_LANES = 2 * LANES
BIAS_LANE0 = LANES + QK_ROPE
MASK_BIAS = -1e30

ODD_Q0, ODD_KV0, ODD_G0, ODD_KR0 = 0, Q_LORA, Q_LORA + KV_LORA, Q_LORA + KV_LORA + C_MIX
ODD_COLS = ODD_KR0 + 2 * QK_ROPE
QUP_R0 = N_HEADS * QK_NOPE
QUP_S0 = QUP_R0 + N_HEADS * QK_ROPE
QUP_COLS = QUP_S0 + N_HEADS * QK_ROPE

F32 = jnp.float32
BF16 = jnp.bfloat16


def _dot(a, b):
    return jnp.dot(a, b, preferred_element_type=F32)


def _dot_nt(a, b):
    return lax.dot_general(a, b, (((1,), (1,)), ((), ())), preferred_element_type=F32)


def _rms(x, g):
    return x * lax.rsqrt(jnp.mean(x * x, axis=-1, keepdims=True) + EPS) * g


def _silu(x):
    return x * (1.0 / (1.0 + jnp.exp(-x)))


def _gelu(x):
    return 0.5 * x * (1.0 + lax.erf(x * (2.0 ** -0.5)))


def _even_body(x_ref, hist_ref, gpre_ref, gpost_ref, win_ref, wpool_ref, pscale_ref, lng_ref, lnb_ref,
               ws_ref, bst_ref, wout_ref, *rest, nb, tt, seg, pos0, emit_v, nsub, fused_in):
    rest = list(rest)
    if fused_in:
        oin_ref, gin_ref, wo_ref, gprev_ref = rest[:4]
        rest = rest[4:]
    xo_ref, histo_ref = rest[:2]
    if emit_v:
        vo_ref, aext_ref, mix_ref = rest[2:]
    else:
        aext_ref, mix_ref = rest[2:]
    t = pl.program_id(1)
    nt = pl.num_programs(1)
    m = nb * tt
    rsz = m // nsub
    rps = tt // nsub if nb == 1 else tt
    st = [dict() for _ in range(nsub)]

    @pl.when(t == 0)
    def _():
        aext_ref[:, 0:HIST_ROWS, :] = hist_ref[...]

    def rows(i):
        return slice(i * rsz, (i + 1) * rsz)

    def stage_x(i):
        pending = oin_ref[0, rows(i), :].astype(F32) * _silu(gin_ref[0, rows(i), :].astype(F32))
        xo_ref[0, rows(i), :] = x_ref[0, rows(i), :] + _rms(_dot(pending.astype(BF16), wo_ref[...]), gprev_ref[...])

    def stage_a(i):
        if fused_in:
            x = xo_ref[0, rows(i), :]
        else:
            x = x_ref[0, rows(i), :] if nb == 1 else x_ref[...].reshape(m, D_MODEL)
        h = _rms(x, gpre_ref[...]).astype(BF16)
        a = _dot(h, win_ref[:, 0:W_A])
        if nb == 1:
            aext_ref[0, HIST_ROWS + i * rsz:HIST_ROWS + (i + 1) * rsz, :] = a
        else:
            aext_ref[:, HIST_ROWS:HIST_ROWS + tt, :] = a.reshape(nb, tt, W_A)
        st[i]["h"] = h

    def stage_uv(i):
        st[i]["uv"] = _dot(st[i]["h"], win_ref[:, W_A:W_A + 2 * W_B])

    def stage_gate(i):
        st[i]["gate"] = _dot(st[i].pop("h"), win_ref[:, W_A + 2 * W_B:])

    def stage_pool(i):
        r0 = HIST_ROWS + (i * rsz if nb == 1 else 0)
        pos = pos0 + t * tt + (r0 - HIST_ROWS) + lax.broadcasted_iota(jnp.int32, (rps, 1), 0)
        for g, w in enumerate(POOL_WINDOWS):
            cs = slice(g * POOL_GDIM, (g + 1) * POOL_GDIM)
            inv_cnt = 1.0 / jnp.minimum(pos + 1, w).astype(F32)
            for s in range(nb):
                wsum = aext_ref[s, r0 - HIST_ROWS:r0 + rps, cs]
                k = 1
                while k < w:
                    wsum = wsum + pltpu.roll(wsum, k, axis=0)
                    k *= 2
                cur = aext_ref[s, r0:r0 + rps, cs]
                mix_ref[i * rsz + s * rps:i * rsz + (s + 1) * rps, cs] = wsum[HIST_ROWS:] * inv_cnt - cur
            y_a = _dot(mix_ref[rows(i), cs].astype(BF16), wpool_ref[g]) * pscale_ref[:, cs]
            mix_ref[rows(i), cs] = y_a

    def stage_sgu(i):
        uv = _gelu(st[i].pop("uv"))
        u = uv[:, :W_B]
        v = uv[:, W_B:]
        vc = v - jnp.mean(v, axis=-1, keepdims=True)
        vn = vc * lax.rsqrt(jnp.mean(vc * vc, axis=-1, keepdims=True) + EPS) * lng_ref[...] + lnb_ref[...]
        if emit_v:
            vo_ref[...] = vn.reshape(nb, tt, W_B)
        vb = vn.astype(BF16)
        ii = lax.broadcasted_iota(jnp.int32, (seg, seg), 0)
        jj = lax.broadcasted_iota(jnp.int32, (seg, seg), 1)
        causal = (jj // CHUNK) <= (ii // CHUNK)
        for g in range(SGU_HEADS):
            hs = slice(g * SGU_HDIM, (g + 1) * SGU_HDIM)
            wsg = jnp.where(causal, ws_ref[g, 0:seg, 0:seg], 0.0).astype(BF16)
            bcol = bst_ref[0:seg, g:g + 1]
            for c in range(rsz // seg):
                rs = slice(c * seg, (c + 1) * seg)
                mixed = _dot(wsg, vb[rs, hs]) + bcol
                mix_ref[i * rsz + c * seg:i * rsz + (c + 1) * seg, W_A + g * SGU_HDIM:W_A + (g + 1) * SGU_HDIM] = (
                    u[rs, hs] * mixed)

    def stage_out(i):
        mixv = (mix_ref[rows(i)] * _silu(st[i].pop("gate"))).astype(BF16)
        y = _rms(_dot(mixv, wout_ref[...]), gpost_ref[...])
        if nb == 1:
            x_in = xo_ref if fused_in else x_ref
            xo_ref[0, rows(i), :] = x_in[0, rows(i), :] + y
        else:
            xo_ref[...] = x_ref[...] + y.reshape(nb, tt, D_MODEL)

    if nsub == 1:
        order = [(stage_a, 0), (stage_uv, 0), (stage_pool, 0), (stage_gate, 0), (stage_sgu, 0), (stage_out, 0)]
    else:
        order = [(stage_a, 0), (stage_uv, 0), (stage_a, 1), (stage_sgu, 0), (stage_uv, 1), (stage_pool, 0),
                 (stage_gate, 0), (stage_sgu, 1), (stage_pool, 1), (stage_out, 0), (stage_gate, 1), (stage_out, 1)]
    if fused_in:
        order = [(stage_x, i) for i in range(nsub)] + order
    for stage, i in order:
        stage(i)

    tail = aext_ref[:, tt:tt + HIST_ROWS, :]

    @pl.when(t == nt - 1)
    def _():
        histo_ref[...] = tail

    aext_ref[:, 0:HIST_ROWS, :] = tail


def _resident(block_shape, index_map):
    return pl.BlockSpec(block_shape, index_map, pipeline_mode=pl.Buffered(1))


def _even_layer(x, hist, lyr, e, p, *, nb, tt, pos0, emit_v, name, pending=None):
    bx, tx, _ = x.shape
    seg = min(tx, SGU_CHUNK)
    assert bx % nb == 0 and tx % tt == 0 and tt % seg == 0 and tt >= HIST_ROWS and tt % 8 == 0
    assert pending is None or nb == 1
    grid = (bx // nb, tx // tt)
    row = lambda b, t: (b, t, 0)
    first = lambda b, t: (b, 0, 0)
    in_specs = [
        pl.BlockSpec((nb, tt, D_MODEL), row),
        pl.BlockSpec((nb, HIST_ROWS, W_A), first),
        _resident((None, 1, D_MODEL), lambda b, t: (lyr, 0, 0)),
        _resident((None, 1, D_MODEL), lambda b, t: (lyr, 0, 0)),
        _resident((None, D_MODEL, EVEN_IN), lambda b, t: (e, 0, 0)),
        _resident((None, len(POOL_WINDOWS), POOL_GDIM, POOL_GDIM), lambda b, t: (e, 0, 0, 0)),
        _resident((None, 1, W_A), lambda b, t: (e, 0, 0)),
        _resident((None, 1, W_B), lambda b, t: (e, 0, 0)),
        _resident((None, 1, W_B), lambda b, t: (e, 0, 0)),
        _resident((None, SGU_HEADS, SGU_CHUNK, SGU_CHUNK), lambda b, t: (e, 0, 0, 0)),
        _resident((None, SGU_CHUNK, SGU_HEADS), lambda b, t: (e, 0, 0)),
        _resident((None, EVEN_MIX, D_MODEL), lambda b, t: (e, 0, 0)),
    ]
    operands = [x, hist, p["norm_pre"], p["norm_post"], p["win_e"], p["wpool"], p["pscale"], p["lng"], p["lnb"],
                p["ws"], p["bst"], p["wout_e"]]
    if pending is not None:
        o_heads, gate, o = pending
        in_specs += [pl.BlockSpec((1, tt, C_MIX), row), pl.BlockSpec((1, tt, C_MIX), row),
                     _resident((None, C_MIX, D_MODEL), lambda b, t: (o, 0, 0)),
                     _resident((None, 1, D_MODEL), lambda b, t: (lyr - 1, 0, 0))]
        operands += [o_heads, gate, p["wo"], p["norm_post"]]
    out_shape = [jax.ShapeDtypeStruct((bx, tx, D_MODEL), F32), jax.ShapeDtypeStruct((bx, HIST_ROWS, W_A), F32)]
    out_specs = [pl.BlockSpec((nb, tt, D_MODEL), row), pl.BlockSpec((nb, HIST_ROWS, W_A), first)]
    if emit_v:
        out_shape.append(jax.ShapeDtypeStruct((bx, tx, W_B), F32))
        out_specs.append(pl.BlockSpec((nb, tt, W_B), row))
    nsub = 2 if nb == 1 and tt % (2 * seg) == 0 else 1
    body = functools.partial(_even_body, nb=nb, tt=tt, seg=seg, pos0=pos0, emit_v=emit_v, nsub=nsub,
                             fused_in=pending is not None)
    return pl.pallas_call(
        body,
        grid=grid,
        in_specs=in_specs,
        out_specs=out_specs,
        out_shape=out_shape,
        scratch_shapes=[pltpu.VMEM((nb, HIST_ROWS + tt, W_A), F32), pltpu.VMEM((nb * tt, EVEN_MIX), F32)],
        compiler_params=pltpu.CompilerParams(
            dimension_semantics=("arbitrary", "arbitrary"), vmem_limit_bytes=VMEM_LIMIT_BYTES),
        name=name,
    )(*operands)


def _proj_body(x_ref, gpre_ref, win_ref, qn_ref, kvn_ref, wq_ref, wuk_ref, cos_ref, sin_ref,
               qa_ref, qr_ref, ckv_ref, kr_ref, ckvb_ref, krb_ref, gate_ref, *, nb, tt):
    m = nb * tt
    x = x_ref[...].reshape(m, D_MODEL)
    h = _rms(x, gpre_ref[...]).astype(BF16)
    z = _dot(h, win_ref[...])
    gate_ref[...] = z[:, ODD_G0:ODD_KR0].reshape(nb, tt, C_MIX)

    ckv = _rms(z[:, ODD_KV0:ODD_G0], kvn_ref[...])
    ckv_ref[...] = ckv.reshape(nb, tt, KV_LORA)
    ckvb_ref[...] = ckv.astype(BF16).reshape(nb, tt, KV_LORA)

    cos = cos_ref[...]
    sin = sin_ref[...]
    kr = (z[:, ODD_KR0:ODD_KR0 + QK_ROPE] * cos[:, :QK_ROPE]
          + z[:, ODD_KR0 + QK_ROPE:ODD_COLS] * sin[:, :QK_ROPE])
    kr_ref[...] = kr.reshape(nb, tt, QK_ROPE)
    krb_ref[...] = kr.astype(BF16).reshape(nb, tt, QK_ROPE)

    qcn = _rms(z[:, ODD_Q0:ODD_KV0], qn_ref[...]).astype(BF16)
    q = _dot(qcn, wq_ref[...])
    c_exp = ATTN_SCALE * LOG2E
    heads_per_vreg = LANES // QK_ROPE
    for c in range(N_HEADS // heads_per_vreg):
        r = ((q[:, QUP_R0 + c * LANES:QUP_R0 + (c + 1) * LANES] * cos
              + q[:, QUP_S0 + c * LANES:QUP_S0 + (c + 1) * LANES] * sin) * c_exp).astype(BF16)
        for i in range(heads_per_vreg):
            qr_ref[:, heads_per_vreg * c + i, :, :] = r[:, i * QK_ROPE:(i + 1) * QK_ROPE].reshape(nb, tt, QK_ROPE)
    for hh in range(N_HEADS):
        q_nope = q[:, hh * QK_NOPE:(hh + 1) * QK_NOPE].astype(BF16)
        qa_ref[:, hh, :, :] = (_dot(q_nope, wuk_ref[hh]) * c_exp).astype(BF16).reshape(nb, tt, KV_LORA)


def _proj_heads_body(x_ref, gpre_ref, win_ref, qn_ref, kvn_ref, wq_ref, wukv_ref, cos_ref, sin_ref, cs_ref,
                     *rest, tt, tile, n_prev):
    x = x_ref[0]
    h = _rms(x, gpre_ref[...]).astype(BF16)
    z = _dot(h, win_ref[...])
    rest = list(rest)
    if n_prev:
        prev_ckv_ref, prev_kr_ref = rest[:2]
        rest = rest[2:]
    q_ref, k_ref, v_ref, ckv_ref, kr_ref, gate_ref = rest
    if n_prev:
        ckv_ref[0:n_prev] = prev_ckv_ref[...]
        kr_ref[0:n_prev] = prev_kr_ref[...]
    gate_ref[0] = z[:, ODD_G0:ODD_KR0].astype(BF16)
    ckv = _rms(z[:, ODD_KV0:ODD_G0], kvn_ref[...])
    ckv_ref[n_prev, 0] = ckv

    lane = lax.broadcasted_iota(jnp.int32, (tt, LANES), 1)
    low = lane < QK_ROPE
    u = z[:, ODD_KR0:ODD_COLS] * cs_ref[...]
    kr_full = u + pltpu.roll(u, QK_ROPE, axis=1)
    kr_ref[n_prev, 0] = kr_full[:, :QK_ROPE]
    pos = pl.program_id(1) * tt + lax.broadcasted_iota(jnp.int32, (tt, LANES), 0)
    onehot = (lane - QK_ROPE == (pos % tile) // CHUNK).astype(F32)
    k_hi = jnp.where(low, kr_full, onehot).astype(BF16)
    v_hi = jnp.ones((tt, LANES), BF16)

    kv = _dot(ckv.astype(BF16), wukv_ref[...])
    qcn = _rms(z[:, ODD_Q0:ODD_KV0], qn_ref[...]).astype(BF16)
    q = _dot(qcn, wq_ref[...])
    c_exp = ATTN_SCALE * LOG2E
    heads_per_vreg = LANES // QK_ROPE
    for hh in range(N_HEADS):
        c, i = divmod(hh, heads_per_vreg)
        r = (q[:, QUP_R0 + c * LANES:QUP_R0 + (c + 1) * LANES] * cos_ref[...]
             + q[:, QUP_S0 + c * LANES:QUP_S0 + (c + 1) * LANES] * sin_ref[...]) * c_exp
        if i:
            r = pltpu.roll(r, LANES - i * QK_ROPE, axis=1)
        q_ref[0, hh, :, 0:LANES] = (q[:, hh * QK_NOPE:(hh + 1) * QK_NOPE] * c_exp).astype(BF16)
        q_ref[0, hh, :, LANES:2 * LANES] = jnp.where(low, r, 0.0).astype(BF16)
        k_ref[0, hh, :, 0:LANES] = kv[:, hh * QK_NOPE:(hh + 1) * QK_NOPE].astype(BF16)
        k_ref[0, hh, :, LANES:2 * LANES] = k_hi
        v_ref[0, hh, :, 0:LANES] = kv[:, C_MIX + hh * V_DIM:C_MIX + (hh + 1) * V_DIM].astype(BF16)
        v_ref[0, hh, :, LANES:2 * LANES] = v_hi


def _odd_project_heads(x, cos, sin, cs, lyr, o, p, *, tt, tile, name, prev=None):
    bx, tx, _ = x.shape
    assert tx % tt == 0 and tt % 16 == 0 and tile % CHUNK == 0 and tile // CHUNK <= LANES - QK_ROPE
    n_prev = 0 if prev is None else prev[0].shape[0]
    grid = (bx, tx // tt)
    row = lambda b, t: (b, t, 0)
    hrow = lambda b, t: (b, 0, t, 0)
    srow = lambda b, t: (0, b, t, 0)
    tab = pl.BlockSpec((tt, LANES), lambda b, t: (t, 0))
    in_specs = [
        pl.BlockSpec((1, tt, D_MODEL), row),
        _resident((None, 1, D_MODEL), lambda b, t: (lyr, 0, 0)),
        _resident((None, D_MODEL, ODD_COLS), lambda b, t: (o, 0, 0)),
        _resident((None, 1, Q_LORA), lambda b, t: (o, 0, 0)),
        _resident((None, 1, KV_LORA), lambda b, t: (o, 0, 0)),
        _resident((None, Q_LORA, QUP_COLS), lambda b, t: (o, 0, 0)),
        _resident((None, KV_LORA, 2 * C_MIX), lambda b, t: (o, 0, 0)),
        tab, tab, tab,
    ]
    operands = [x, p["norm_pre"], p["win_o"], p["qn"], p["kvn"], p["wq"], p["wukv"], cos, sin, cs]
    if n_prev:
        in_specs += [pl.BlockSpec((n_prev, 1, tt, KV_LORA), srow), pl.BlockSpec((n_prev, 1, tt, QK_ROPE), srow)]
        operands += list(prev)
    head_arr = jax.ShapeDtypeStruct((bx, N_HEADS, tx, HEAD_LANES), BF16)
    head_spec = pl.BlockSpec((1, N_HEADS, tt, HEAD_LANES), hrow)
    out_shape = [head_arr, head_arr, head_arr,
                 jax.ShapeDtypeStruct((n_prev + 1, bx, tx, KV_LORA), F32),
                 jax.ShapeDtypeStruct((n_prev + 1, bx, tx, QK_ROPE), F32),
                 jax.ShapeDtypeStruct((bx, tx, C_MIX), BF16)]
    out_specs = [head_spec, head_spec, head_spec,
                 pl.BlockSpec((n_prev + 1, 1, tt, KV_LORA), srow),
                 pl.BlockSpec((n_prev + 1, 1, tt, QK_ROPE), srow),
                 pl.BlockSpec((1, tt, C_MIX), row)]
    return pl.pallas_call(
        functools.partial(_proj_heads_body, tt=tt, tile=tile, n_prev=n_prev),
        grid=grid,
        in_specs=in_specs,
        out_specs=out_specs,
        out_shape=out_shape,
        compiler_params=pltpu.CompilerParams(
            dimension_semantics=("arbitrary", "arbitrary"), vmem_limit_bytes=VMEM_LIMIT_BYTES),
        name=name,
    )(*operands)


def _flat_attn_body(q_ref, k_ref, v_ref, o_ref, m_ref, acc_ref, s0_ref, s1_ref, dbias_ref, *, tile, nq, rb):
    nblk = tile // rb
    row = lax.broadcasted_iota(jnp.int32, (tile, HEAD_LANES), 0)
    idx = lax.broadcasted_iota(jnp.int32, (tile, HEAD_LANES), 1) - BIAS_LANE0
    hidden = jnp.logical_and(jnp.logical_and(idx >= 0, idx < tile // CHUNK), idx > row // CHUNK)
    dbias_ref[...] = jnp.where(hidden, MASK_BIAS, 0.0).astype(BF16)
    acc_ref[...] = jnp.zeros(acc_ref.shape, F32)

    def scores(b, qi, j, s_ref):
        k = k_ref[0, 0, pl.ds(pl.multiple_of(j * tile, tile), tile), :]
        q = q_ref[0, 0, pl.ds(pl.multiple_of(qi * tile + b * rb, rb), rb), :]
        q = q + jnp.where(qi == j, dbias_ref[b * rb:(b + 1) * rb], jnp.zeros((rb, HEAD_LANES), BF16))
        s_ref[b * rb:(b + 1) * rb] = _dot_nt(q, k)

    def softmax_pv(b, qi, j, s_ref):
        v = v_ref[0, 0, pl.ds(pl.multiple_of(j * tile, tile), tile), :]
        rs = slice(b * rb, (b + 1) * rb)
        s = s_ref[rs]
        m_old = jnp.where(j == 0, -1e30, m_ref[rs])
        m_new = jnp.maximum(m_old, jnp.max(s, axis=-1, keepdims=True))
        alpha = jnp.exp2(m_old - m_new)
        p = jnp.exp2(s - jnp.concatenate([m_new] * (tile // LANES), axis=1))
        acc = (acc_ref[rs] * jnp.concatenate([alpha] * (HEAD_LANES // LANES), axis=1)
               + _dot(p.astype(BF16), v))
        acc_ref[rs] = acc
        m_ref[rs] = m_new
        o_ref[0, pl.ds(pl.multiple_of(qi * tile + b * rb, rb), rb), :] = (
            acc[:, :V_DIM] * (1.0 / acc[:, V_DIM:])).astype(BF16)

    def step(qi, j, s_cur, s_next):
        wrap = j == qi
        qn = jnp.minimum(jnp.where(wrap, qi + 1, qi), nq - 1)
        jn = jnp.where(wrap, 0, j + 1)
        for b in range(nblk):
            scores(b, qn, jn, s_next)
        for b in range(nblk):
            softmax_pv(b, qi, j, s_cur)
        return qn, jn

    def two_steps(_, carry):
        return step(*step(*carry, s0_ref, s1_ref), s1_ref, s0_ref)

    for b in range(nblk):
        scores(b, 0, 0, s0_ref)
    n_steps = nq * (nq + 1) // 2
    n_pairs = n_steps // 2
    unroll = max(u for u in (1, 2, 3) if n_pairs % u == 0)
    carry = lax.fori_loop(0, n_pairs, two_steps, (jnp.int32(0), jnp.int32(0)), unroll=unroll)
    if n_steps % 2:
        step(*carry, s0_ref, s1_ref)


def _flat_attend(q, k, v, *, tile, name):
    bx, nh, tx, _ = q.shape
    assert tx % tile == 0 and tile % FLAT_BLOCK_ROWS == 0
    head = pl.BlockSpec((1, 1, tx, HEAD_LANES), lambda b, hh: (b, hh, 0, 0))
    return pl.pallas_call(
        functools.partial(_flat_attn_body, tile=tile, nq=tx // tile, rb=FLAT_BLOCK_ROWS),
        grid=(bx, nh),
        in_specs=[head, head, head],
        out_specs=pl.BlockSpec((1, tx, V_DIM), lambda b, hh: (b, 0, hh)),
        out_shape=jax.ShapeDtypeStruct((bx, tx, nh * V_DIM), BF16),
        scratch_shapes=[pltpu.VMEM((tile, LANES), F32), pltpu.VMEM((tile, HEAD_LANES), F32),
                        pltpu.VMEM((tile, tile), F32), pltpu.VMEM((tile, tile), F32),
                        pltpu.VMEM((tile, HEAD_LANES), BF16)],
        compiler_params=pltpu.CompilerParams(
            dimension_semantics=("arbitrary", "arbitrary"), vmem_limit_bytes=VMEM_LIMIT_BYTES),
        name=name,
    )(q, k, v)


def _out_body(o_ref, gate_ref, x_ref, wo_ref, gpost_ref, xo_ref):
    y = _dot((o_ref[0].astype(F32) * _silu(gate_ref[0].astype(F32))).astype(BF16), wo_ref[...])
    xo_ref[0] = x_ref[0] + _rms(y, gpost_ref[...])


def _odd_output(o_heads, gate, x, lyr, o, p, *, tt, name):
    bx, tx, _ = x.shape
    assert tx % tt == 0
    row = lambda b, t: (b, t, 0)
    return pl.pallas_call(
        _out_body,
        grid=(bx, tx // tt),
        in_specs=[pl.BlockSpec((1, tt, C_MIX), row), pl.BlockSpec((1, tt, C_MIX), row),
                  pl.BlockSpec((1, tt, D_MODEL), row),
                  _resident((None, C_MIX, D_MODEL), lambda b, t: (o, 0, 0)),
                  _resident((None, 1, D_MODEL), lambda b, t: (lyr, 0, 0))],
        out_specs=pl.BlockSpec((1, tt, D_MODEL), row),
        out_shape=jax.ShapeDtypeStruct((bx, tx, D_MODEL), F32),
        compiler_params=pltpu.CompilerParams(
            dimension_semantics=("arbitrary", "arbitrary"), vmem_limit_bytes=VMEM_LIMIT_BYTES),
        name=name,
    )(o_heads, gate, x, p["wo"], p["norm_post"])


def _odd_project(x, cos, sin, lyr, o, p, *, nb, tt, name):
    bx, tx, _ = x.shape
    assert bx % nb == 0 and tx % tt == 0 and tt % 16 == 0
    grid = (bx // nb, tx // tt)
    row = lambda b, t: (b, t, 0)
    hrow = lambda b, t: (b, 0, t, 0)
    in_specs = [
        pl.BlockSpec((nb, tt, D_MODEL), row),
        _resident((None, 1, D_MODEL), lambda b, t: (lyr, 0, 0)),
        _resident((None, D_MODEL, ODD_COLS), lambda b, t: (o, 0, 0)),
        _resident((None, 1, Q_LORA), lambda b, t: (o, 0, 0)),
        _resident((None, 1, KV_LORA), lambda b, t: (o, 0, 0)),
        _resident((None, Q_LORA, QUP_COLS), lambda b, t: (o, 0, 0)),
        _resident((None, N_HEADS, QK_NOPE, KV_LORA), lambda b, t: (o, 0, 0, 0)),
        pl.BlockSpec((nb * tt, LANES), lambda b, t: (t, 0)),
        pl.BlockSpec((nb * tt, LANES), lambda b, t: (t, 0)),
    ]
    out_shape = [
        jax.ShapeDtypeStruct((bx, N_HEADS, tx, KV_LORA), BF16),
        jax.ShapeDtypeStruct((bx, N_HEADS, tx, QK_ROPE), BF16),
        jax.ShapeDtypeStruct((bx, tx, KV_LORA), F32),
        jax.ShapeDtypeStruct((bx, tx, QK_ROPE), F32),
        jax.ShapeDtypeStruct((bx, tx, KV_LORA), BF16),
        jax.ShapeDtypeStruct((bx, tx, QK_ROPE), BF16),
        jax.ShapeDtypeStruct((bx, tx, C_MIX), F32),
    ]
    out_specs = [
        pl.BlockSpec((nb, N_HEADS, tt, KV_LORA), hrow),
        pl.BlockSpec((nb, N_HEADS, tt, QK_ROPE), hrow),
        pl.BlockSpec((nb, tt, KV_LORA), row),
        pl.BlockSpec((nb, tt, QK_ROPE), row),
        pl.BlockSpec((nb, tt, KV_LORA), row),
        pl.BlockSpec((nb, tt, QK_ROPE), row),
        pl.BlockSpec((nb, tt, C_MIX), row),
    ]
    return pl.pallas_call(
        functools.partial(_proj_body, nb=nb, tt=tt),
        grid=grid,
        in_specs=in_specs,
        out_specs=out_specs,
        out_shape=out_shape,
        compiler_params=pltpu.CompilerParams(
            dimension_semantics=("arbitrary", "arbitrary"), vmem_limit_bytes=VMEM_LIMIT_BYTES),
        name=name,
    )(x, p["norm_pre"], p["win_o"], p["qn"], p["kvn"], p["wq"], p["wuk"], cos, sin)


def _cached_attn_body(qa_ref, qr_ref, cckv_ref, ckrt_ref, nckv_ref, nkr_ref, o_ref, *, tq, past):
    rows = N_HEADS * tq
    qa = qa_ref[0].reshape(rows, KV_LORA)
    qr = qr_ref[0].reshape(rows, QK_ROPE)
    kc = cckv_ref[0].astype(BF16)
    kn = nckv_ref[0]
    s_old = _dot_nt(qa, kc) + _dot(qr, ckrt_ref[0].astype(BF16))
    s_new = _dot_nt(qa, kn) + _dot_nt(qr, nkr_ref[0])
    kpos = past + lax.broadcasted_iota(jnp.int32, (1, tq, tq), 2)
    qpos = past + lax.broadcasted_iota(jnp.int32, (1, tq, tq), 1)
    s_new = jnp.where(kpos // CHUNK <= qpos // CHUNK, s_new.reshape(N_HEADS, tq, tq), -jnp.inf).reshape(rows, tq)
    m = jnp.maximum(jnp.max(s_old, axis=-1, keepdims=True), jnp.max(s_new, axis=-1, keepdims=True))
    p_old = jnp.exp2(s_old - m)
    p_new = jnp.exp2(s_new - m)
    l = jnp.sum(p_old, axis=-1, keepdims=True) + jnp.sum(p_new, axis=-1, keepdims=True)
    o_lat = (_dot(p_old.astype(BF16), kc) + _dot(p_new.astype(BF16), kn)) * (1.0 / l)
    o_ref[:, 0] = o_lat.astype(BF16).reshape(N_HEADS, tq, KV_LORA)


def _cached_attend(qa, qr, cache_ckv, cache_krope_t, ckvb, krb, o, *, name):
    bx, _, tq, _ = qa.shape
    past = cache_ckv.shape[2]
    assert tq % 16 == 0
    in_specs = [
        pl.BlockSpec((1, N_HEADS, tq, KV_LORA), lambda b: (b, 0, 0, 0)),
        pl.BlockSpec((1, N_HEADS, tq, QK_ROPE), lambda b: (b, 0, 0, 0)),
        pl.BlockSpec((None, 1, past, KV_LORA), lambda b: (o, b, 0, 0)),
        pl.BlockSpec((None, 1, QK_ROPE, past), lambda b: (o, b, 0, 0)),
        pl.BlockSpec((1, tq, KV_LORA), lambda b: (b, 0, 0)),
        pl.BlockSpec((1, tq, QK_ROPE), lambda b: (b, 0, 0)),
    ]
    return pl.pallas_call(
        functools.partial(_cached_attn_body, tq=tq, past=past),
        grid=(bx,),
        in_specs=in_specs,
        out_specs=pl.BlockSpec((N_HEADS, 1, tq, KV_LORA), lambda b: (0, b, 0, 0)),
        out_shape=jax.ShapeDtypeStruct((N_HEADS, bx, tq, KV_LORA), BF16),
        compiler_params=pltpu.CompilerParams(
            dimension_semantics=("arbitrary",), vmem_limit_bytes=VMEM_LIMIT_BYTES),
        name=name,
    )(qa, qr, cache_ckv, cache_krope_t, ckvb, krb)


def _latent_out_body(olat_ref, gate_ref, x_ref, wuv_ref, wo_ref, gpost_ref, xo_ref):
    o = jnp.concatenate([_dot(olat_ref[hh], wuv_ref[hh]) for hh in range(N_HEADS)], axis=1)
    y = _dot((o * _silu(gate_ref[...])).astype(BF16), wo_ref[...])
    xo_ref[...] = x_ref[...] + _rms(y, gpost_ref[...])


def _latent_output(o_lat, gate, x, lyr, o, p, *, name):
    m = x.shape[0]
    whole = lambda shape: pl.BlockSpec(shape, lambda i: (0,) * len(shape))
    return pl.pallas_call(
        _latent_out_body,
        grid=(1,),
        in_specs=[whole((N_HEADS, m, KV_LORA)), whole((m, C_MIX)), whole((m, D_MODEL)),
                  pl.BlockSpec((None, N_HEADS, KV_LORA, V_DIM), lambda i: (o, 0, 0, 0)),
                  pl.BlockSpec((None, C_MIX, D_MODEL), lambda i: (o, 0, 0)),
                  pl.BlockSpec((None, 1, D_MODEL), lambda i: (lyr, 0, 0))],
        out_specs=whole((m, D_MODEL)),
        out_shape=jax.ShapeDtypeStruct((m, D_MODEL), F32),
        compiler_params=pltpu.CompilerParams(
            dimension_semantics=("arbitrary",), vmem_limit_bytes=VMEM_LIMIT_BYTES),
        name=name,
    )(o_lat, gate, x, p["wuv"], p["wo"], p["norm_post"])


def _rope_tables(pos0, n, copies=1):
    half = QK_ROPE // 2
    freqs = ROPE_BASE ** (-np.arange(half, dtype=np.float64) / half)
    ang = (pos0 + np.arange(n, dtype=np.float64))[:, None] * freqs[None, :]
    c, s = np.cos(ang), np.sin(ang)
    cos64, sin64 = np.concatenate([c, c], axis=1), np.concatenate([-s, s], axis=1)
    tables = (np.tile(cos64, (copies, LANES // QK_ROPE)), np.tile(sin64, (copies, LANES // QK_ROPE)),
              np.tile(np.concatenate([cos64, sin64], axis=1), (copies, 1)))
    return tuple(jnp.asarray(tab, F32) for tab in tables)


def _half_swap(w):
    half = QK_ROPE // 2
    return jnp.concatenate([w[..., half:], w[..., :half]], axis=-1)


def _prepare_params(norm_pre, norm_post, w_in_even, w_pool, pool_scale, sgu_ln_g, sgu_ln_b, w_spatial, b_spatial,
                    w_out_even, w_in_odd, q_norm, kv_norm, w_q_up, w_kv_up, w_o):
    n_odd = w_in_odd.shape[0]
    kr_w = w_in_odd[..., Q_LORA + KV_LORA:Q_LORA + KV_LORA + QK_ROPE]
    win_o = jnp.concatenate([w_in_odd[..., :Q_LORA + KV_LORA], w_in_odd[..., Q_LORA + KV_LORA + QK_ROPE:],
                             kr_w, _half_swap(kr_w)], axis=-1)
    wq_r = w_q_up[..., QK_NOPE:]
    wq = jnp.concatenate([w_q_up[..., :QK_NOPE].reshape(n_odd, Q_LORA, N_HEADS * QK_NOPE),
                          wq_r.reshape(n_odd, Q_LORA, N_HEADS * QK_ROPE),
                          _half_swap(wq_r).reshape(n_odd, Q_LORA, N_HEADS * QK_ROPE)], axis=-1)
    return {
        "norm_pre": norm_pre[:, None, :],
        "norm_post": norm_post[:, None, :],
        "win_e": w_in_even.astype(BF16),
        "wpool": w_pool.astype(BF16),
        "pscale": pool_scale[:, None, :],
        "lng": sgu_ln_g[:, None, :],
        "lnb": sgu_ln_b[:, None, :],
        "ws": w_spatial,
        "bst": jnp.swapaxes(b_spatial, 1, 2),
        "wout_e": w_out_even.astype(BF16),
        "win_o": win_o.astype(BF16),
        "qn": q_norm[:, None, :],
        "kvn": kv_norm[:, None, :],
        "wq": wq.astype(BF16),
        "wuk": jnp.transpose(w_kv_up[..., :QK_NOPE], (0, 2, 3, 1)).astype(BF16),
        "wuv": jnp.transpose(w_kv_up[..., QK_NOPE:], (0, 2, 1, 3)).astype(BF16),
        "wukv": jnp.concatenate([w_kv_up[..., :QK_NOPE].reshape(n_odd, KV_LORA, N_HEADS * QK_NOPE),
                                 w_kv_up[..., QK_NOPE:].reshape(n_odd, KV_LORA, C_MIX)], axis=-1).astype(BF16),
        "wo": w_o.astype(BF16),
    }


def _tile_sizes(t_prompt):
    tt_even = min(t_prompt, 512)
    tt_proj = min(t_prompt, 512)
    tile = min(t_prompt, 1024)
    tt_out = min(t_prompt, 1024)
    return tt_even, tt_proj, tile, tt_out


def kernel(x_prompt, x_sample, cache_pool, cache_ckv, cache_krope, norm_pre, norm_post, w_in_even, w_pool, pool_scale, sgu_ln_g, sgu_ln_b, w_spatial, b_spatial, w_out_even, w_in_odd, q_norm, kv_norm, w_q_up, w_kv_up, w_o):
    depth = norm_pre.shape[0]
    b, t, _ = x_prompt.shape
    db, s, _ = x_sample.shape
    past = cache_ckv.shape[2]
    p = _prepare_params(norm_pre, norm_post, w_in_even, w_pool, pool_scale, sgu_ln_g, sgu_ln_b, w_spatial,
                        b_spatial, w_out_even, w_in_odd, q_norm, kv_norm, w_q_up, w_kv_up, w_o)
    tt_even, tt_proj, tile, tt_out = _tile_sizes(t)

    cos_p, sin_p, cs_p = _rope_tables(0, t)
    cos_s, sin_s, _ = _rope_tables(past, s, copies=db)
    cache_krope_t = jnp.swapaxes(cache_krope, 2, 3)

    xp, xs = x_prompt, x_sample
    pool_p, pool_s, sgu_s = [], [], []
    ckv_s, kr_s = [], []
    zero_hist = jnp.zeros((b, HIST_ROWS, W_A), F32)
    pending = None
    latent_p = None
    for lyr in range(depth):
        if lyr % 2 == 0:
            e = lyr // 2
            xp, hp = _even_layer(xp, zero_hist, lyr, e, p, nb=1, tt=tt_even, pos0=0, emit_v=False,
                                 name=f"even{e}_prompt", pending=pending)
            pending = None
            hist_s = jnp.pad(cache_pool[e], ((0, 0), (HIST_ROWS - POOL_HIST, 0), (0, 0)))
            xs, hs, vs = _even_layer(xs, hist_s, lyr, e, p, nb=db, tt=s, pos0=past, emit_v=True,
                                     name=f"even{e}_sample")
            pool_p.append(hp[:, HIST_ROWS - POOL_HIST:])
            pool_s.append(hs[:, HIST_ROWS - POOL_HIST:])
            sgu_s.append(vs)
        else:
            o = lyr // 2
            qh, kh, vh, ckv_stack, kr_stack, gate = _odd_project_heads(
                xp, cos_p, sin_p, cs_p, lyr, o, p, tt=tt_proj, tile=tile, name=f"proj{o}_prompt", prev=latent_p)
            latent_p = (ckv_stack, kr_stack)
            o_heads = _flat_attend(qh, kh, vh, tile=tile, name=f"attn{o}_prompt")
            if lyr + 1 < depth:
                pending = (o_heads, gate, o)
            else:
                xp = _odd_output(o_heads, gate, xp, lyr, o, p, tt=tt_out, name=f"out{o}_prompt")

            qa, qr, ckv, kr, ckvb, krb, gate = _odd_project(xs, cos_s, sin_s, lyr, o, p, nb=db, tt=s,
                                                            name=f"proj{o}_sample")
            o_lat = _cached_attend(qa, qr, cache_ckv, cache_krope_t, ckvb, krb, o, name=f"attn{o}_sample")
            xs = _latent_output(o_lat.reshape(N_HEADS, db * s, KV_LORA), gate.reshape(db * s, C_MIX),
                                xs.reshape(db * s, D_MODEL), lyr, o, p, name=f"out{o}_sample").reshape(db, s, D_MODEL)
            ckv_s.append(ckv)
            kr_s.append(kr)
    return (xp, xs, jnp.stack(pool_p), jnp.stack(pool_s), jnp.stack(sgu_s),
            latent_p[0], latent_p[1], jnp.stack(ckv_s), jnp.stack(kr_s))
```

```python
import functools

import jax
import jax.numpy as jnp
import numpy as np
from jax import lax
from jax.experimental import pallas as pl
from jax.experimental.pallas import tpu as pltpu

D_MODEL = 1024
CHUNK = 64
EPS = 1e-6
POOL_WINDOWS = (2, 4, 8, 16)
POOL_HIST = max(POOL_WINDOWS) - 1
W_A = D_MODEL
POOL_GDIM = W_A // len(POOL_WINDOWS)
W_B = D_MODEL
SGU_HEADS = 4
SGU_HDIM = W_B // SGU_HEADS
SGU_CHUNK = 128
EVEN_MIX = W_A + W_B
EVEN_IN = W_A + 2 * W_B + EVEN_MIX
N_HEADS = D_MODEL // 128
QK_NOPE = 128
QK_ROPE = 64
V_DIM = 128
KV_LORA = D_MODEL // 4
Q_LORA = 3 * D_MODEL // 8
C_MIX = N_HEADS * V_DIM
ROPE_BASE = 10000.0
ATTN_SCALE = (QK_NOPE + QK_ROPE) ** -0.5
LOG2E = 1.4426950408889634

LANES = 128
HIST_ROWS = 16
VMEM_LIMIT_BYTES = 56 * 1024 * 1024
FLAT_BLOCK_ROWS = 512
HEAD_LANES = 2 * LANES
BIAS_LANE0 = LANES + QK_ROPE
MASK_BIAS = -1e30

ODD_Q0, ODD_KV0, ODD_G0, ODD_KR0 = 0, Q_LORA, Q_LORA + KV_LORA, Q_LORA + KV_LORA + C_MIX
ODD_COLS = ODD_KR0 + 2 * QK_ROPE
QUP_R0 = N_HEADS * QK_NOPE
QUP_S0 = QUP_R0 + N_HEADS * QK_ROPE
QUP_COLS = QUP_S0 + N_HEADS * QK_ROPE

F32 = jnp.float32
BF16 = jnp.bfloat16


def _dot(a, b):
    return jnp.dot(a, b, preferred_element_type=F32)


def _dot_nt(a, b):
    return lax.dot_general(a, b, (((1,), (1,)), ((), ())), preferred_element_type=F32)


def _rms(x, g):
    return x * lax.rsqrt(jnp.mean(x * x, axis=-1, keepdims=True) + EPS) * g


def _silu(x):
    return x * (1.0 / (1.0 + jnp.exp(-x)))


def _gelu(x):
    return 0.5 * x * (1.0 + lax.erf(x * (2.0 ** -0.5)))


def _even_body(x_ref, hist_ref, gpre_ref, gpost_ref, win_ref, wpool_ref, pscale_ref, lng_ref, lnb_ref,
               ws_ref, bst_ref, wout_ref, *rest, nb, tt, seg, pos0, emit_v, nsub, fused_in):
    rest = list(rest)
    if fused_in:
        oin_ref, gin_ref, wo_ref, gprev_ref = rest[:4]
        rest = rest[4:]
    xo_ref, histo_ref = rest[:2]
    if emit_v:
        vo_ref, aext_ref, mix_ref = rest[2:]
    else:
        aext_ref, mix_ref = rest[2:]
    t = pl.program_id(1)
    nt = pl.num_programs(1)
    m = nb * tt
    rsz = m // nsub
    rps = tt // nsub if nb == 1 else tt
    st = [dict() for _ in range(nsub)]

    @pl.when(t == 0)
    def _():
        aext_ref[:, 0:HIST_ROWS, :] = hist_ref[...]

    def rows(i):
        return slice(i * rsz, (i + 1) * rsz)

    def stage_x(i):
        pending = oin_ref[0, rows(i), :].astype(F32) * _silu(gin_ref[0, rows(i), :].astype(F32))
        xo_ref[0, rows(i), :] = x_ref[0, rows(i), :] + _rms(_dot(pending.astype(BF16), wo_ref[...]), gprev_ref[...])

    def stage_a(i):
        if fused_in:
            x = xo_ref[0, rows(i), :]
        else:
            x = x_ref[0, rows(i), :] if nb == 1 else x_ref[...].reshape(m, D_MODEL)
        h = _rms(x, gpre_ref[...]).astype(BF16)
        a = _dot(h, win_ref[:, 0:W_A])
        if nb == 1:
            aext_ref[0, HIST_ROWS + i * rsz:HIST_ROWS + (i + 1) * rsz, :] = a
        else:
            aext_ref[:, HIST_ROWS:HIST_ROWS + tt, :] = a.reshape(nb, tt, W_A)
        st[i]["h"] = h

    def stage_uv(i):
        st[i]["uv"] = _dot(st[i]["h"], win_ref[:, W_A:W_A + 2 * W_B])

    def stage_gate(i):
        st[i]["gate"] = _dot(st[i].pop("h"), win_ref[:, W_A + 2 * W_B:])

    def stage_pool(i):
        r0 = HIST_ROWS + (i * rsz if nb == 1 else 0)
        pos = pos0 + t * tt + (r0 - HIST_ROWS) + lax.broadcasted_iota(jnp.int32, (rps, 1), 0)
        for g, w in enumerate(POOL_WINDOWS):
            cs = slice(g * POOL_GDIM, (g + 1) * POOL_GDIM)
            inv_cnt = 1.0 / jnp.minimum(pos + 1, w).astype(F32)
            for s in range(nb):
                wsum = aext_ref[s, r0 - HIST_ROWS:r0 + rps, cs]
                k = 1
                while k < w:
                    wsum = wsum + pltpu.roll(wsum, k, axis=0)
                    k *= 2
                cur = aext_ref[s, r0:r0 + rps, cs]
                mix_ref[i * rsz + s * rps:i * rsz + (s + 1) * rps, cs] = wsum[HIST_ROWS:] * inv_cnt - cur
            y_a = _dot(mix_ref[rows(i), cs].astype(BF16), wpool_ref[g]) * pscale_ref[:, cs]
            mix_ref[rows(i), cs] = y_a

    def stage_sgu(i):
        uv = _gelu(st[i].pop("uv"))
        u = uv[:, :W_B]
        v = uv[:, W_B:]
        vc = v - jnp.mean(v, axis=-1, keepdims=True)
        vn = vc * lax.rsqrt(jnp.mean(vc * vc, axis=-1, keepdims=True) + EPS) * lng_ref[...] + lnb_ref[...]
        if emit_v:
            vo_ref[...] = vn.reshape(nb, tt, W_B)
        vb = vn.astype(BF16)
        ii = lax.broadcasted_iota(jnp.int32, (seg, seg), 0)
        jj = lax.broadcasted_iota(jnp.int32, (seg, seg), 1)
        causal = (jj // CHUNK) <= (ii // CHUNK)
        for g in range(SGU_HEADS):
            hs = slice(g * SGU_HDIM, (g + 1) * SGU_HDIM)
            wsg = jnp.where(causal, ws_ref[g, 0:seg, 0:seg], 0.0).astype(BF16)
            bcol = bst_ref[0:seg, g:g + 1]
            for c in range(rsz // seg):
                rs = slice(c * seg, (c + 1) * seg)
                mixed = _dot(wsg, vb[rs, hs]) + bcol
                mix_ref[i * rsz + c * seg:i * rsz + (c + 1) * seg, W_A + g * SGU_HDIM:W_A + (g + 1) * SGU_HDIM] = (
                    u[rs, hs] * mixed)

    def stage_out(i):
        mixv = (mix_ref[rows(i)] * _silu(st[i].pop("gate"))).astype(BF16)
        y = _rms(_dot(mixv, wout_ref[...]), gpost_ref[...])
        if nb == 1:
            x_in = xo_ref if fused_in else x_ref
            xo_ref[0, rows(i), :] = x_in[0, rows(i), :] + y
        else:
            xo_ref[...] = x_ref[...] + y.reshape(nb, tt, D_MODEL)

    if nsub == 1:
        order = [(stage_a, 0), (stage_uv, 0), (stage_pool, 0), (stage_gate, 0), (stage_sgu, 0), (stage_out, 0)]
    else:
        order = [(stage_a, 0), (stage_uv, 0), (stage_a, 1), (stage_sgu, 0), (stage_uv, 1), (stage_pool, 0),
                 (stage_gate, 0), (stage_sgu, 1), (stage_pool, 1), (stage_out, 0), (stage_gate, 1), (stage_out, 1)]
    if fused_in:
        order = [(stage_x, i) for i in range(nsub)] + order
    for stage, i in order:
        stage(i)

    tail = aext_ref[:, tt:tt + HIST_ROWS, :]

    @pl.when(t == nt - 1)
    def _():
        histo_ref[...] = tail

    aext_ref[:, 0:HIST_ROWS, :] = tail


def _resident(block_shape, index_map):
    return pl.BlockSpec(block_shape, index_map, pipeline_mode=pl.Buffered(1))


def _even_layer(x, hist, lyr, e, p, *, nb, tt, pos0, emit_v, name, pending=None):
    bx, tx, _ = x.shape
    seg = min(tx, SGU_CHUNK)
    assert bx % nb == 0 and tx % tt == 0 and tt % seg == 0 and tt >= HIST_ROWS and tt % 8 == 0
    assert pending is None or nb == 1
    grid = (bx // nb, tx // tt)
    row = lambda b, t: (b, t, 0)
    first = lambda b, t: (b, 0, 0)
    in_specs = [
        pl.BlockSpec((nb, tt, D_MODEL), row),
        pl.BlockSpec((nb, HIST_ROWS, W_A), first),
        _resident((None, 1, D_MODEL), lambda b, t: (lyr, 0, 0)),
        _resident((None, 1, D_MODEL), lambda b, t: (lyr, 0, 0)),
        _resident((None, D_MODEL, EVEN_IN), lambda b, t: (e, 0, 0)),
        _resident((None, len(POOL_WINDOWS), POOL_GDIM, POOL_GDIM), lambda b, t: (e, 0, 0, 0)),
        _resident((None, 1, W_A), lambda b, t: (e, 0, 0)),
        _resident((None, 1, W_B), lambda b, t: (e, 0, 0)),
        _resident((None, 1, W_B), lambda b, t: (e, 0, 0)),
        _resident((None, SGU_HEADS, SGU_CHUNK, SGU_CHUNK), lambda b, t: (e, 0, 0, 0)),
        _resident((None, SGU_CHUNK, SGU_HEADS), lambda b, t: (e, 0, 0)),
        _resident((None, EVEN_MIX, D_MODEL), lambda b, t: (e, 0, 0)),
    ]
    operands = [x, hist, p["norm_pre"], p["norm_post"], p["win_e"], p["wpool"], p["pscale"], p["lng"], p["lnb"],
                p["ws"], p["bst"], p["wout_e"]]
    if pending is not None:
        o_heads, gate, o = pending
        in_specs += [pl.BlockSpec((1, tt, C_MIX), row), pl.BlockSpec((1, tt, C_MIX), row),
                     _resident((None, C_MIX, D_MODEL), lambda b, t: (o, 0, 0)),
                     _resident((None, 1, D_MODEL), lambda b, t: (lyr - 1, 0, 0))]
        operands += [o_heads, gate, p["wo"], p["norm_post"]]
    out_shape = [jax.ShapeDtypeStruct((bx, tx, D_MODEL), F32), jax.ShapeDtypeStruct((bx, HIST_ROWS, W_A), F32)]
    out_specs = [pl.BlockSpec((nb, tt, D_MODEL), row), pl.BlockSpec((nb, HIST_ROWS, W_A), first)]
    if emit_v:
        out_shape.append(jax.ShapeDtypeStruct((bx, tx, W_B), F32))
        out_specs.append(pl.BlockSpec((nb, tt, W_B), row))
    nsub = 2 if nb == 1 and tt % (2 * seg) == 0 else 1
    body = functools.partial(_even_body, nb=nb, tt=tt, seg=seg, pos0=pos0, emit_v=emit_v, nsub=nsub,
                             fused_in=pending is not None)
    return pl.pallas_call(
        body,
        grid=grid,
        in_specs=in_specs,
        out_specs=out_specs,
        out_shape=out_shape,
        scratch_shapes=[pltpu.VMEM((nb, HIST_ROWS + tt, W_A), F32), pltpu.VMEM((nb * tt, EVEN_MIX), F32)],
        compiler_params=pltpu.CompilerParams(
            dimension_semantics=("arbitrary", "arbitrary"), vmem_limit_bytes=VMEM_LIMIT_BYTES),
        name=name,
    )(*operands)


def _proj_body(x_ref, gpre_ref, win_ref, qn_ref, kvn_ref, wq_ref, wuk_ref, cos_ref, sin_ref,
               qa_ref, qr_ref, ckv_ref, kr_ref, ckvb_ref, krb_ref, gate_ref, *, nb, tt):
    m = nb * tt
    x = x_ref[...].reshape(m, D_MODEL)
    h = _rms(x, gpre_ref[...]).astype(BF16)
    z = _dot(h, win_ref[...])
    gate_ref[...] = z[:, ODD_G0:ODD_KR0].reshape(nb, tt, C_MIX)

    ckv = _rms(z[:, ODD_KV0:ODD_G0], kvn_ref[...])
    ckv_ref[...] = ckv.reshape(nb, tt, KV_LORA)
    ckvb_ref[...] = ckv.astype(BF16).reshape(nb, tt, KV_LORA)

    cos = cos_ref[...]
    sin = sin_ref[...]
    kr = (z[:, ODD_KR0:ODD_KR0 + QK_ROPE] * cos[:, :QK_ROPE]
          + z[:, ODD_KR0 + QK_ROPE:ODD_COLS] * sin[:, :QK_ROPE])
    kr_ref[...] = kr.reshape(nb, tt, QK_ROPE)
    krb_ref[...] = kr.astype(BF16).reshape(nb, tt, QK_ROPE)

    qcn = _rms(z[:, ODD_Q0:ODD_KV0], qn_ref[...]).astype(BF16)
    q = _dot(qcn, wq_ref[...])
    c_exp = ATTN_SCALE * LOG2E
    heads_per_vreg = LANES // QK_ROPE
    for c in range(N_HEADS // heads_per_vreg):
        r = ((q[:, QUP_R0 + c * LANES:QUP_R0 + (c + 1) * LANES] * cos
              + q[:, QUP_S0 + c * LANES:QUP_S0 + (c + 1) * LANES] * sin) * c_exp).astype(BF16)
        for i in range(heads_per_vreg):
            qr_ref[:, heads_per_vreg * c + i, :, :] = r[:, i * QK_ROPE:(i + 1) * QK_ROPE].reshape(nb, tt, QK_ROPE)
    for hh in range(N_HEADS):
        q_nope = q[:, hh * QK_NOPE:(hh + 1) * QK_NOPE].astype(BF16)
        qa_ref[:, hh, :, :] = (_dot(q_nope, wuk_ref[hh]) * c_exp).astype(BF16).reshape(nb, tt, KV_LORA)


def _proj_heads_body(x_ref, gpre_ref, win_ref, qn_ref, kvn_ref, wq_ref, wukv_ref, cos_ref, sin_ref, cs_ref,
                     *rest, tt, tile, n_prev):
    x = x_ref[0]
    h = _rms(x, gpre_ref[...]).astype(BF16)
    z = _dot(h, win_ref[...])
    rest = list(rest)
    if n_prev:
        prev_ckv_ref, prev_kr_ref = rest[:2]
        rest = rest[2:]
    q_ref, k_ref, v_ref, ckv_ref, kr_ref, gate_ref = rest
    if n_prev:
        ckv_ref[0:n_prev] = prev_ckv_ref[...]
        kr_ref[0:n_prev] = prev_kr_ref[...]
    gate_ref[0] = z[:, ODD_G0:ODD_KR0].astype(BF16)
    ckv = _rms(z[:, ODD_KV0:ODD_G0], kvn_ref[...])
    ckv_ref[n_prev, 0] = ckv

    lane = lax.broadcasted_iota(jnp.int32, (tt, LANES), 1)
    low = lane < QK_ROPE
    u = z[:, ODD_KR0:ODD_COLS] * cs_ref[...]
    kr_full = u + pltpu.roll(u, QK_ROPE, axis=1)
    kr_ref[n_prev, 0] = kr_full[:, :QK_ROPE]
    pos = pl.program_id(1) * tt + lax.broadcasted_iota(jnp.int32, (tt, LANES), 0)
    onehot = (lane - QK_ROPE == (pos % tile) // CHUNK).astype(F32)
    k_hi = jnp.where(low, kr_full, onehot).astype(BF16)
    v_hi = jnp.ones((tt, LANES), BF16)

    kv = _dot(ckv.astype(BF16), wukv_ref[...])
    qcn = _rms(z[:, ODD_Q0:ODD_KV0], qn_ref[...]).astype(BF16)
    q = _dot(qcn, wq_ref[...])
    c_exp = ATTN_SCALE * LOG2E
    heads_per_vreg = LANES // QK_ROPE
    for hh in range(N_HEADS):
        c, i = divmod(hh, heads_per_vreg)
        r = (q[:, QUP_R0 + c * LANES:QUP_R0 + (c + 1) * LANES] * cos_ref[...]
             + q[:, QUP_S0 + c * LANES:QUP_S0 + (c + 1) * LANES] * sin_ref[...]) * c_exp
        if i:
            r = pltpu.roll(r, LANES - i * QK_ROPE, axis=1)
        q_ref[0, hh, :, 0:LANES] = (q[:, hh * QK_NOPE:(hh + 1) * QK_NOPE] * c_exp).astype(BF16)
        q_ref[0, hh, :, LANES:2 * LANES] = jnp.where(low, r, 0.0).astype(BF16)
        k_ref[0, hh, :, 0:LANES] = kv[:, hh * QK_NOPE:(hh + 1) * QK_NOPE].astype(BF16)
        k_ref[0, hh, :, LANES:2 * LANES] = k_hi
        v_ref[0, hh, :, 0:LANES] = kv[:, C_MIX + hh * V_DIM:C_MIX + (hh + 1) * V_DIM].astype(BF16)
        v_ref[0, hh, :, LANES:2 * LANES] = v_hi


def _odd_project_heads(x, cos, sin, cs, lyr, o, p, *, tt, tile, name, prev=None):
    bx, tx, _ = x.shape
    assert tx % tt == 0 and tt % 16 == 0 and tile % CHUNK == 0 and tile // CHUNK <= LANES - QK_ROPE
    n_prev = 0 if prev is None else prev[0].shape[0]
    grid = (bx, tx // tt)
    row = lambda b, t: (b, t, 0)
    hrow = lambda b, t: (b, 0, t, 0)
    srow = lambda b, t: (0, b, t, 0)
    tab = pl.BlockSpec((tt, LANES), lambda b, t: (t, 0))
    in_specs = [
        pl.BlockSpec((1, tt, D_MODEL), row),
        _resident((None, 1, D_MODEL), lambda b, t: (lyr, 0, 0)),
        _resident((None, D_MODEL, ODD_COLS), lambda b, t: (o, 0, 0)),
        _resident((None, 1, Q_LORA), lambda b, t: (o, 0, 0)),
        _resident((None, 1, KV_LORA), lambda b, t: (o, 0, 0)),
        _resident((None, Q_LORA, QUP_COLS), lambda b, t: (o, 0, 0)),
        _resident((None, KV_LORA, 2 * C_MIX), lambda b, t: (o, 0, 0)),
        tab, tab, tab,
    ]
    operands = [x, p["norm_pre"], p["win_o"], p["qn"], p["kvn"], p["wq"], p["wukv"], cos, sin, cs]
    if n_prev:
        in_specs += [pl.BlockSpec((n_prev, 1, tt, KV_LORA), srow), pl.BlockSpec((n_prev, 1, tt, QK_ROPE), srow)]
        operands += list(prev)
    head_arr = jax.ShapeDtypeStruct((bx, N_HEADS, tx, HEAD_LANES), BF16)
    head_spec = pl.BlockSpec((1, N_HEADS, tt, HEAD_LANES), hrow)
    out_shape = [head_arr, head_arr, head_arr,
                 jax.ShapeDtypeStruct((n_prev + 1, bx, tx, KV_LORA), F32),
                 jax.ShapeDtypeStruct((n_prev + 1, bx, tx, QK_ROPE), F32),
                 jax.ShapeDtypeStruct((bx, tx, C_MIX), BF16)]
    out_specs = [head_spec, head_spec, head_spec,
                 pl.BlockSpec((n_prev + 1, 1, tt, KV_LORA), srow),
                 pl.BlockSpec((n_prev + 1, 1, tt, QK_ROPE), srow),
                 pl.BlockSpec((1, tt, C_MIX), row)]
    return pl.pallas_call(
        functools.partial(_proj_heads_body, tt=tt, tile=tile, n_prev=n_prev),
        grid=grid,
        in_specs=in_specs,
        out_specs=out_specs,
        out_shape=out_shape,
        compiler_params=pltpu.CompilerParams(
            dimension_semantics=("arbitrary", "arbitrary"), vmem_limit_bytes=VMEM_LIMIT_BYTES),
        name=name,
    )(*operands)


def _flat_attn_body(q_ref, k_ref, v_ref, o_ref, m_ref, acc_ref, s0_ref, s1_ref, dbias_ref, *, tile, nq, rb):
    nblk = tile // rb
    row = lax.broadcasted_iota(jnp.int32, (tile, HEAD_LANES), 0)
    idx = lax.broadcasted_iota(jnp.int32, (tile, HEAD_LANES), 1) - BIAS_LANE0
    hidden = jnp.logical_and(jnp.logical_and(idx >= 0, idx < tile // CHUNK), idx > row // CHUNK)
    dbias_ref[...] = jnp.where(hidden, MASK_BIAS, 0.0).astype(BF16)
    acc_ref[...] = jnp.zeros(acc_ref.shape, F32)

    def scores(b, qi, j, s_ref):
        k = k_ref[0, 0, pl.ds(pl.multiple_of(j * tile, tile), tile), :]
        q = q_ref[0, 0, pl.ds(pl.multiple_of(qi * tile + b * rb, rb), rb), :]
        q = q + jnp.where(qi == j, dbias_ref[b * rb:(b + 1) * rb], jnp.zeros((rb, HEAD_LANES), BF16))
        s_ref[b * rb:(b + 1) * rb] = _dot_nt(q, k)

    def softmax_pv(b, qi, j, s_ref):
        v = v_ref[0, 0, pl.ds(pl.multiple_of(j * tile, tile), tile), :]
        rs = slice(b * rb, (b + 1) * rb)
        s = s_ref[rs]
        m_old = jnp.where(j == 0, -1e30, m_ref[rs])
        m_new = jnp.maximum(m_old, jnp.max(s, axis=-1, keepdims=True))
        alpha = jnp.exp2(m_old - m_new)
        p = jnp.exp2(s - jnp.concatenate([m_new] * (tile // LANES), axis=1))
        acc = (acc_ref[rs] * jnp.concatenate([alpha] * (HEAD_LANES // LANES), axis=1)
               + _dot(p.astype(BF16), v))
        acc_ref[rs] = acc
        m_ref[rs] = m_new
        o_ref[0, pl.ds(pl.multiple_of(qi * tile + b * rb, rb), rb), :] = (
            acc[:, :V_DIM] * (1.0 / acc[:, V_DIM:])).astype(BF16)

    def step(qi, j, s_cur, s_next):
        wrap = j == qi
        qn = jnp.minimum(jnp.where(wrap, qi + 1, qi), nq - 1)
        jn = jnp.where(wrap, 0, j + 1)
        for b in range(nblk):
            scores(b, qn, jn, s_next)
        for b in range(nblk):
            softmax_pv(b, qi, j, s_cur)
        return qn, jn

    def two_steps(_, carry):
        return step(*step(*carry, s0_ref, s1_ref), s1_ref, s0_ref)

    for b in range(nblk):
        scores(b, 0, 0, s0_ref)
    n_steps = nq * (nq + 1) // 2
    n_pairs = n_steps // 2
    unroll = max(u for u in (1, 2, 3) if n_pairs % u == 0)
    carry = lax.fori_loop(0, n_pairs, two_steps, (jnp.int32(0), jnp.int32(0)), unroll=unroll)
    if n_steps % 2:
        step(*carry, s0_ref, s1_ref)


def _flat_attend(q, k, v, *, tile, name):
    bx, nh, tx, _ = q.shape
    assert tx % tile == 0 and tile % FLAT_BLOCK_ROWS == 0
    head = pl.BlockSpec((1, 1, tx, HEAD_LANES), lambda b, hh: (b, hh, 0, 0))
    return pl.pallas_call(
        functools.partial(_flat_attn_body, tile=tile, nq=tx // tile, rb=FLAT_BLOCK_ROWS),
        grid=(bx, nh),
        in_specs=[head, head, head],
        out_specs=pl.BlockSpec((1, tx, V_DIM), lambda b, hh: (b, 0, hh)),
        out_shape=jax.ShapeDtypeStruct((bx, tx, nh * V_DIM), BF16),
        scratch_shapes=[pltpu.VMEM((tile, LANES), F32), pltpu.VMEM((tile, HEAD_LANES), F32),
                        pltpu.VMEM((tile, tile), F32), pltpu.VMEM((tile, tile), F32),
                        pltpu.VMEM((tile, HEAD_LANES), BF16)],
        compiler_params=pltpu.CompilerParams(
            dimension_semantics=("arbitrary", "arbitrary"), vmem_limit_bytes=VMEM_LIMIT_BYTES),
        name=name,
    )(q, k, v)


def _out_body(o_ref, gate_ref, x_ref, wo_ref, gpost_ref, xo_ref):
    y = _dot((o_ref[0].astype(F32) * _silu(gate_ref[0].astype(F32))).astype(BF16), wo_ref[...])
    xo_ref[0] = x_ref[0] + _rms(y, gpost_ref[...])


def _odd_output(o_heads, gate, x, lyr, o, p, *, tt, name):
    bx, tx, _ = x.shape
    assert tx % tt == 0
    row = lambda b, t: (b, t, 0)
    return pl.pallas_call(
        _out_body,
        grid=(bx, tx // tt),
        in_specs=[pl.BlockSpec((1, tt, C_MIX), row), pl.BlockSpec((1, tt, C_MIX), row),
                  pl.BlockSpec((1, tt, D_MODEL), row),
                  _resident((None, C_MIX, D_MODEL), lambda b, t: (o, 0, 0)),
                  _resident((None, 1, D_MODEL), lambda b, t: (lyr, 0, 0))],
        out_specs=pl.BlockSpec((1, tt, D_MODEL), row),
        out_shape=jax.ShapeDtypeStruct((bx, tx, D_MODEL), F32),
        compiler_params=pltpu.CompilerParams(
            dimension_semantics=("arbitrary", "arbitrary"), vmem_limit_bytes=VMEM_LIMIT_BYTES),
        name=name,
    )(o_heads, gate, x, p["wo"], p["norm_post"])


def _odd_project(x, cos, sin, lyr, o, p, *, nb, tt, name):
    bx, tx, _ = x.shape
    assert bx % nb == 0 and tx % tt == 0 and tt % 16 == 0
    grid = (bx // nb, tx // tt)
    row = lambda b, t: (b, t, 0)
    hrow = lambda b, t: (b, 0, t, 0)
    in_specs = [
        pl.BlockSpec((nb, tt, D_MODEL), row),
        _resident((None, 1, D_MODEL), lambda b, t: (lyr, 0, 0)),
        _resident((None, D_MODEL, ODD_COLS), lambda b, t: (o, 0, 0)),
        _resident((None, 1, Q_LORA), lambda b, t: (o, 0, 0)),
        _resident((None, 1, KV_LORA), lambda b, t: (o, 0, 0)),
        _resident((None, Q_LORA, QUP_COLS), lambda b, t: (o, 0, 0)),
        _resident((None, N_HEADS, QK_NOPE, KV_LORA), lambda b, t: (o, 0, 0, 0)),
        pl.BlockSpec((nb * tt, LANES), lambda b, t: (t, 0)),
        pl.BlockSpec((nb * tt, LANES), lambda b, t: (t, 0)),
    ]
    out_shape = [
        jax.ShapeDtypeStruct((bx, N_HEADS, tx, KV_LORA), BF16),
        jax.ShapeDtypeStruct((bx, N_HEADS, tx, QK_ROPE), BF16),
        jax.ShapeDtypeStruct((bx, tx, KV_LORA), F32),
        jax.ShapeDtypeStruct((bx, tx, QK_ROPE), F32),
        jax.ShapeDtypeStruct((bx, tx, KV_LORA), BF16),
        jax.ShapeDtypeStruct((bx, tx, QK_ROPE), BF16),
        jax.ShapeDtypeStruct((bx, tx, C_MIX), F32),
    ]
    out_specs = [
        pl.BlockSpec((nb, N_HEADS, tt, KV_LORA), hrow),
        pl.BlockSpec((nb, N_HEADS, tt, QK_ROPE), hrow),
        pl.BlockSpec((nb, tt, KV_LORA), row),
        pl.BlockSpec((nb, tt, QK_ROPE), row),
        pl.BlockSpec((nb, tt, KV_LORA), row),
        pl.BlockSpec((nb, tt, QK_ROPE), row),
        pl.BlockSpec((nb, tt, C_MIX), row),
    ]
    return pl.pallas_call(
        functools.partial(_proj_body, nb=nb, tt=tt),
        grid=grid,
        in_specs=in_specs,
        out_specs=out_specs,
        out_shape=out_shape,
        compiler_params=pltpu.CompilerParams(
            dimension_semantics=("arbitrary", "arbitrary"), vmem_limit_bytes=VMEM_LIMIT_BYTES),
        name=name,
    )(x, p["norm_pre"], p["win_o"], p["qn"], p["kvn"], p["wq"], p["wuk"], cos, sin)


def _cached_attn_body(qa_ref, qr_ref, cckv_ref, ckrt_ref, nckv_ref, nkr_ref, o_ref, *, tq, past, ns):
    rows = N_HEADS * tq
    kpos = past + lax.broadcasted_iota(jnp.int32, (1, tq, tq), 2)
    qpos = past + lax.broadcasted_iota(jnp.int32, (1, tq, tq), 1)
    visible = kpos // CHUNK <= qpos // CHUNK
    staged = []
    for s in range(ns):
        qa = qa_ref[s].reshape(rows, KV_LORA)
        qr = qr_ref[s].reshape(rows, QK_ROPE)
        kc = cckv_ref[s].astype(BF16)
        kn = nckv_ref[s]
        s_old = _dot_nt(qa, kc) + _dot(qr, ckrt_ref[s].astype(BF16))
        s_new = _dot_nt(qa, kn) + _dot_nt(qr, nkr_ref[s])
        s_new = jnp.where(visible, s_new.reshape(N_HEADS, tq, tq), -jnp.inf).reshape(rows, tq)
        staged.append((s_old, s_new, kc, kn))
    for s, (s_old, s_new, kc, kn) in enumerate(staged):
        m = jnp.maximum(jnp.max(s_old, axis=-1, keepdims=True), jnp.max(s_new, axis=-1, keepdims=True))
        p_old = jnp.exp2(s_old - m)
        p_new = jnp.exp2(s_new - m)
        l = jnp.sum(p_old, axis=-1, keepdims=True) + jnp.sum(p_new, axis=-1, keepdims=True)
        o_lat = (_dot(p_old.astype(BF16), kc) + _dot(p_new.astype(BF16), kn)) * (1.0 / l)
        o_ref[:, s] = o_lat.astype(BF16).reshape(N_HEADS, tq, KV_LORA)


def _cached_attend(qa, qr, cache_ckv, cache_krope_t, ckvb, krb, o, *, name):
    bx, _, tq, _ = qa.shape
    past = cache_ckv.shape[2]
    assert tq % 16 == 0
    ns = max(n for n in (1, 2, 4) if bx % n == 0)
    in_specs = [
        pl.BlockSpec((ns, N_HEADS, tq, KV_LORA), lambda b: (b, 0, 0, 0)),
        pl.BlockSpec((ns, N_HEADS, tq, QK_ROPE), lambda b: (b, 0, 0, 0)),
        pl.BlockSpec((None, ns, past, KV_LORA), lambda b: (o, b, 0, 0)),
        pl.BlockSpec((None, ns, QK_ROPE, past), lambda b: (o, b, 0, 0)),
        pl.BlockSpec((ns, tq, KV_LORA), lambda b: (b, 0, 0)),
        pl.BlockSpec((ns, tq, QK_ROPE), lambda b: (b, 0, 0)),
    ]
    return pl.pallas_call(
        functools.partial(_cached_attn_body, tq=tq, past=past, ns=ns),
        grid=(bx // ns,),
        in_specs=in_specs,
        out_specs=pl.BlockSpec((N_HEADS, ns, tq, KV_LORA), lambda b: (0, b, 0, 0)),
        out_shape=jax.ShapeDtypeStruct((N_HEADS, bx, tq, KV_LORA), BF16),
        compiler_params=pltpu.CompilerParams(
            dimension_semantics=("arbitrary",), vmem_limit_bytes=VMEM_LIMIT_BYTES),
        name=name,
    )(qa, qr, cache_ckv, cache_krope_t, ckvb, krb)


def _latent_out_body(olat_ref, gate_ref, x_ref, wuv_ref, wo_ref, gpost_ref, xo_ref):
    o = jnp.concatenate([_dot(olat_ref[hh], wuv_ref[hh]) for hh in range(N_HEADS)], axis=1)
    y = _dot((o * _silu(gate_ref[...])).astype(BF16), wo_ref[...])
    xo_ref[...] = x_ref[...] + _rms(y, gpost_ref[...])


def _latent_output(o_lat, gate, x, lyr, o, p, *, name):
    m = x.shape[0]
    whole = lambda shape: pl.BlockSpec(shape, lambda i: (0,) * len(shape))
    return pl.pallas_call(
        _latent_out_body,
        grid=(1,),
        in_specs=[whole((N_HEADS, m, KV_LORA)), whole((m, C_MIX)), whole((m, D_MODEL)),
                  pl.BlockSpec((None, N_HEADS, KV_LORA, V_DIM), lambda i: (o, 0, 0, 0)),
                  pl.BlockSpec((None, C_MIX, D_MODEL), lambda i: (o, 0, 0)),
                  pl.BlockSpec((None, 1, D_MODEL), lambda i: (lyr, 0, 0))],
        out_specs=whole((m, D_MODEL)),
        out_shape=jax.ShapeDtypeStruct((m, D_MODEL), F32),
        compiler_params=pltpu.CompilerParams(
            dimension_semantics=("arbitrary",), vmem_limit_bytes=VMEM_LIMIT_BYTES),
        name=name,
    )(o_lat, gate, x, p["wuv"], p["wo"], p["norm_post"])


def _rope_tables(pos0, n, copies=1):
    half = QK_ROPE // 2
    freqs = ROPE_BASE ** (-np.arange(half, dtype=np.float64) / half)
    ang = (pos0 + np.arange(n, dtype=np.float64))[:, None] * freqs[None, :]
    c, s = np.cos(ang), np.sin(ang)
    cos64, sin64 = np.concatenate([c, c], axis=1), np.concatenate([-s, s], axis=1)
    tables = (np.tile(cos64, (copies, LANES // QK_ROPE)), np.tile(sin64, (copies, LANES // QK_ROPE)),
              np.tile(np.concatenate([cos64, sin64], axis=1), (copies, 1)))
    return tuple(jnp.asarray(tab, F32) for tab in tables)


def _half_swap(w):
    half = QK_ROPE // 2
    return jnp.concatenate([w[..., half:], w[..., :half]], axis=-1)


def _prepare_params(norm_pre, norm_post, w_in_even, w_pool, pool_scale, sgu_ln_g, sgu_ln_b, w_spatial, b_spatial,
                    w_out_even, w_in_odd, q_norm, kv_norm, w_q_up, w_kv_up, w_o):
    n_odd = w_in_odd.shape[0]
    kr_w = w_in_odd[..., Q_LORA + KV_LORA:Q_LORA + KV_LORA + QK_ROPE]
    win_o = jnp.concatenate([w_in_odd[..., :Q_LORA + KV_LORA], w_in_odd[..., Q_LORA + KV_LORA + QK_ROPE:],
                             kr_w, _half_swap(kr_w)], axis=-1)
    wq_r = w_q_up[..., QK_NOPE:]
    wq = jnp.concatenate([w_q_up[..., :QK_NOPE].reshape(n_odd, Q_LORA, N_HEADS * QK_NOPE),
                          wq_r.reshape(n_odd, Q_LORA, N_HEADS * QK_ROPE),
                          _half_swap(wq_r).reshape(n_odd, Q_LORA, N_HEADS * QK_ROPE)], axis=-1)
    return {
        "norm_pre": norm_pre[:, None, :],
        "norm_post": norm_post[:, None, :],
        "win_e": w_in_even.astype(BF16),
        "wpool": w_pool.astype(BF16),
        "pscale": pool_scale[:, None, :],
        "lng": sgu_ln_g[:, None, :],
        "lnb": sgu_ln_b[:, None, :],
        "ws": w_spatial,
        "bst": jnp.swapaxes(b_spatial, 1, 2),
        "wout_e": w_out_even.astype(BF16),
        "win_o": win_o.astype(BF16),
        "qn": q_norm[:, None, :],
        "kvn": kv_norm[:, None, :],
        "wq": wq.astype(BF16),
        "wuk": jnp.transpose(w_kv_up[..., :QK_NOPE], (0, 2, 3, 1)).astype(BF16),
        "wuv": jnp.transpose(w_kv_up[..., QK_NOPE:], (0, 2, 1, 3)).astype(BF16),
        "wukv": jnp.concatenate([w_kv_up[..., :QK_NOPE].reshape(n_odd, KV_LORA, N_HEADS * QK_NOPE),
                                 w_kv_up[..., QK_NOPE:].reshape(n_odd, KV_LORA, C_MIX)], axis=-1).astype(BF16),
        "wo": w_o.astype(BF16),
    }


def _tile_sizes(t_prompt):
    tt_even = min(t_prompt, 512)
    tt_proj = min(t_prompt, 512)
    tile = min(t_prompt, 1024)
    tt_out = min(t_prompt, 1024)
    return tt_even, tt_proj, tile, tt_out


def kernel(x_prompt, x_sample, cache_pool, cache_ckv, cache_krope, norm_pre, norm_post, w_in_even, w_pool, pool_scale, sgu_ln_g, sgu_ln_b, w_spatial, b_spatial, w_out_even, w_in_odd, q_norm, kv_norm, w_q_up, w_kv_up, w_o):
    depth = norm_pre.shape[0]
    b, t, _ = x_prompt.shape
    db, s, _ = x_sample.shape
    past = cache_ckv.shape[2]
    p = _prepare_params(norm_pre, norm_post, w_in_even, w_pool, pool_scale, sgu_ln_g, sgu_ln_b, w_spatial,
                        b_spatial, w_out_even, w_in_odd, q_norm, kv_norm, w_q_up, w_kv_up, w_o)
    tt_even, tt_proj, tile, tt_out = _tile_sizes(t)

    cos_p, sin_p, cs_p = _rope_tables(0, t)
    cos_s, sin_s, _ = _rope_tables(past, s, copies=db)
    cache_krope_t = jnp.swapaxes(cache_krope, 2, 3)

    xp, xs = x_prompt, x_sample
    pool_p, pool_s, sgu_s = [], [], []
    ckv_s, kr_s = [], []
    zero_hist = jnp.zeros((b, HIST_ROWS, W_A), F32)
    pending = None
    latent_p = None
    for lyr in range(depth):
        if lyr % 2 == 0:
            e = lyr // 2
            xp, hp = _even_layer(xp, zero_hist, lyr, e, p, nb=1, tt=tt_even, pos0=0, emit_v=False,
                                 name=f"even{e}_prompt", pending=pending)
            pending = None
            hist_s = jnp.pad(cache_pool[e], ((0, 0), (HIST_ROWS - POOL_HIST, 0), (0, 0)))
            xs, hs, vs = _even_layer(xs, hist_s, lyr, e, p, nb=db, tt=s, pos0=past, emit_v=True,
                                     name=f"even{e}_sample")
            pool_p.append(hp[:, HIST_ROWS - POOL_HIST:])
            pool_s.append(hs[:, HIST_ROWS - POOL_HIST:])
            sgu_s.append(vs)
        else:
            o = lyr // 2
            qh, kh, vh, ckv_stack, kr_stack, gate = _odd_project_heads(
                xp, cos_p, sin_p, cs_p, lyr, o, p, tt=tt_proj, tile=tile, name=f"proj{o}_prompt", prev=latent_p)
            latent_p = (ckv_stack, kr_stack)
            o_heads = _flat_attend(qh, kh, vh, tile=tile, name=f"attn{o}_prompt")
            if lyr + 1 < depth:
                pending = (o_heads, gate, o)
            else:
                xp = _odd_output(o_heads, gate, xp, lyr, o, p, tt=tt_out, name=f"out{o}_prompt")

            qa, qr, ckv, kr, ckvb, krb, gate = _odd_project(xs, cos_s, sin_s, lyr, o, p, nb=db, tt=s,
                                                            name=f"proj{o}_sample")
            o_lat = _cached_attend(qa, qr, cache_ckv, cache_krope_t, ckvb, krb, o, name=f"attn{o}_sample")
            xs = _latent_output(o_lat.reshape(N_HEADS, db * s, KV_LORA), gate.reshape(db * s, C_MIX),
                                xs.reshape(db * s, D_MODEL), lyr, o, p, name=f"out{o}_sample").reshape(db, s, D_MODEL)
            ckv_s.append(ckv)
            kr_s.append(kr)
    return (xp, xs, jnp.stack(pool_p), jnp.stack(pool_s), jnp.stack(sgu_s),
            latent_p[0], latent_p[1], jnp.stack(ckv_s), jnp.stack(kr_s))
```

```python
import functools

import jax
import jax.numpy as jnp
import numpy as np
from jax import lax
from jax.experimental import pallas as pl
from jax.experimental.pallas import tpu as pltpu

D_MODEL = 1024
CHUNK = 64
EPS = 1e-6
POOL_WINDOWS = (2, 4, 8, 16)
POOL_HIST = max(POOL_WINDOWS) - 1
W_A = D_MODEL
POOL_GDIM = W_A // len(POOL_WINDOWS)
W_B = D_MODEL
SGU_HEADS = 4
SGU_HDIM = W_B // SGU_HEADS
SGU_CHUNK = 128
EVEN_MIX = W_A + W_B
EVEN_IN = W_A + 2 * W_B + EVEN_MIX
N_HEADS = D_MODEL // 128
QK_NOPE = 128
QK_ROPE = 64
V_DIM = 128
KV_LORA = D_MODEL // 4
Q_LORA = 3 * D_MODEL // 8
C_MIX = N_HEADS * V_DIM
ROPE_BASE = 10000.0
ATTN_SCALE = (QK_NOPE + QK_ROPE) ** -0.5
LOG2E = 1.4426950408889634

LANES = 128
HIST_ROWS = 16
VMEM_LIMIT_BYTES = 56 * 1024 * 1024
FLAT_BLOCK_ROWS = 512
HEAD_LANES = 2 * LANES
BIAS_LANE0 = LANES + QK_ROPE
MASK_BIAS = -1e30

ODD_Q0, ODD_KV0, ODD_G0, ODD_KR0 = 0, Q_LORA, Q_LORA + KV_LORA, Q_LORA + KV_LORA + C_MIX
ODD_COLS = ODD_KR0 + 2 * QK_ROPE
QUP_R0 = N_HEADS * QK_NOPE
QUP_S0 = QUP_R0 + N_HEADS * QK_ROPE
QUP_COLS = QUP_S0 + N_HEADS * QK_ROPE

F32 = jnp.float32
BF16 = jnp.bfloat16


def _dot(a, b):
    return jnp.dot(a, b, preferred_element_type=F32)


def _dot_nt(a, b):
    return lax.dot_general(a, b, (((1,), (1,)), ((), ())), preferred_element_type=F32)


def _rms(x, g):
    return x * lax.rsqrt(jnp.mean(x * x, axis=-1, keepdims=True) + EPS) * g


def _silu(x):
    return x * (1.0 / (1.0 + jnp.exp(-x)))


def _gelu(x):
    return 0.5 * x * (1.0 + lax.erf(x * (2.0 ** -0.5)))


def _even_body(x_ref, hist_ref, gpre_ref, gpost_ref, win_ref, wpool_ref, pscale_ref, lng_ref, lnb_ref,
               ws_ref, bst_ref, wout_ref, *rest, nb, tt, seg, pos0, emit_v, nsub, fused_in):
    rest = list(rest)
    if fused_in:
        oin_ref, gin_ref, wo_ref, gprev_ref = rest[:4]
        rest = rest[4:]
    xo_ref, histo_ref = rest[:2]
    if emit_v:
        vo_ref, aext_ref, mix_ref = rest[2:]
    else:
        aext_ref, mix_ref = rest[2:]
    t = pl.program_id(1)
    nt = pl.num_programs(1)
    m = nb * tt
    rsz = m // nsub
    rps = tt // nsub if nb == 1 else tt
    st = [dict() for _ in range(nsub)]

    @pl.when(t == 0)
    def _():
        aext_ref[:, 0:HIST_ROWS, :] = hist_ref[...]

    def rows(i):
        return slice(i * rsz, (i + 1) * rsz)

    def stage_x(i):
        pending = oin_ref[0, rows(i), :].astype(F32) * _silu(gin_ref[0, rows(i), :].astype(F32))
        xo_ref[0, rows(i), :] = x_ref[0, rows(i), :] + _rms(_dot(pending.astype(BF16), wo_ref[...]), gprev_ref[...])

    def stage_a(i):
        if fused_in:
            x = xo_ref[0, rows(i), :]
        else:
            x = x_ref[0, rows(i), :] if nb == 1 else x_ref[...].reshape(m, D_MODEL)
        h = _rms(x, gpre_ref[...]).astype(BF16)
        a = _dot(h, win_ref[:, 0:W_A])
        if nb == 1:
            aext_ref[0, HIST_ROWS + i * rsz:HIST_ROWS + (i + 1) * rsz, :] = a
        else:
            aext_ref[:, HIST_ROWS:HIST_ROWS + tt, :] = a.reshape(nb, tt, W_A)
        st[i]["h"] = h

    def stage_uv(i):
        st[i]["uv"] = _dot(st[i]["h"], win_ref[:, W_A:W_A + 2 * W_B])

    def stage_gate(i):
        st[i]["gate"] = _dot(st[i].pop("h"), win_ref[:, W_A + 2 * W_B:])

    def stage_pool(i):
        r0 = HIST_ROWS + (i * rsz if nb == 1 else 0)
        pos = pos0 + t * tt + (r0 - HIST_ROWS) + lax.broadcasted_iota(jnp.int32, (rps, 1), 0)
        for g, w in enumerate(POOL_WINDOWS):
            cs = slice(g * POOL_GDIM, (g + 1) * POOL_GDIM)
            inv_cnt = 1.0 / jnp.minimum(pos + 1, w).astype(F32)
            for s in range(nb):
                wsum = aext_ref[s, r0 - HIST_ROWS:r0 + rps, cs]
                k = 1
                while k < w:
                    wsum = wsum + pltpu.roll(wsum, k, axis=0)
                    k *= 2
                cur = aext_ref[s, r0:r0 + rps, cs]
                mix_ref[i * rsz + s * rps:i * rsz + (s + 1) * rps, cs] = wsum[HIST_ROWS:] * inv_cnt - cur
            y_a = _dot(mix_ref[rows(i), cs].astype(BF16), wpool_ref[g]) * pscale_ref[:, cs]
            mix_ref[rows(i), cs] = y_a

    def stage_sgu(i):
        uv = _gelu(st[i].pop("uv"))
        u = uv[:, :W_B]
        v = uv[:, W_B:]
        vc = v - jnp.mean(v, axis=-1, keepdims=True)
        vn = vc * lax.rsqrt(jnp.mean(vc * vc, axis=-1, keepdims=True) + EPS) * lng_ref[...] + lnb_ref[...]
        if emit_v:
            vo_ref[...] = vn.reshape(nb, tt, W_B)
        vb = vn.astype(BF16)
        ii = lax.broadcasted_iota(jnp.int32, (seg, seg), 0)
        jj = lax.broadcasted_iota(jnp.int32, (seg, seg), 1)
        causal = (jj // CHUNK) <= (ii // CHUNK)
        for g in range(SGU_HEADS):
            hs = slice(g * SGU_HDIM, (g + 1) * SGU_HDIM)
            wsg = jnp.where(causal, ws_ref[g, 0:seg, 0:seg], 0.0).astype(BF16)
            bcol = bst_ref[0:seg, g:g + 1]
            for c in range(rsz // seg):
                rs = slice(c * seg, (c + 1) * seg)
                mixed = _dot(wsg, vb[rs, hs]) + bcol
                mix_ref[i * rsz + c * seg:i * rsz + (c + 1) * seg, W_A + g * SGU_HDIM:W_A + (g + 1) * SGU_HDIM] = (
                    u[rs, hs] * mixed)

    def stage_out(i):
        mixv = (mix_ref[rows(i)] * _silu(st[i].pop("gate"))).astype(BF16)
        y = _rms(_dot(mixv, wout_ref[...]), gpost_ref[...])
        if nb == 1:
            x_in = xo_ref if fused_in else x_ref
            xo_ref[0, rows(i), :] = x_in[0, rows(i), :] + y
        else:
            xo_ref[...] = x_ref[...] + y.reshape(nb, tt, D_MODEL)

    if nsub == 1:
        order = [(stage_a, 0), (stage_uv, 0), (stage_pool, 0), (stage_gate, 0), (stage_sgu, 0), (stage_out, 0)]
    else:
        order = [(stage_a, 0), (stage_uv, 0), (stage_a, 1), (stage_sgu, 0), (stage_uv, 1), (stage_pool, 0),
                 (stage_gate, 0), (stage_sgu, 1), (stage_pool, 1), (stage_out, 0), (stage_gate, 1), (stage_out, 1)]
    if fused_in:
        order = [(stage_x, i) for i in range(nsub)] + order
    for stage, i in order:
        stage(i)

    tail = aext_ref[:, tt:tt + HIST_ROWS, :]

    @pl.when(t == nt - 1)
    def _():
        histo_ref[...] = tail

    aext_ref[:, 0:HIST_ROWS, :] = tail


def _resident(block_shape, index_map):
    return pl.BlockSpec(block_shape, index_map, pipeline_mode=pl.Buffered(1))


def _even_layer(x, hist, lyr, e, p, *, nb, tt, pos0, emit_v, name, pending=None):
    bx, tx, _ = x.shape
    seg = min(tx, SGU_CHUNK)
    assert bx % nb == 0 and tx % tt == 0 and tt % seg == 0 and tt >= HIST_ROWS and tt % 8 == 0
    assert pending is None or nb == 1
    grid = (bx // nb, tx // tt)
    row = lambda b, t: (b, t, 0)
    first = lambda b, t: (b, 0, 0)
    in_specs = [
        pl.BlockSpec((nb, tt, D_MODEL), row),
        pl.BlockSpec((nb, HIST_ROWS, W_A), first),
        _resident((None, 1, D_MODEL), lambda b, t: (lyr, 0, 0)),
        _resident((None, 1, D_MODEL), lambda b, t: (lyr, 0, 0)),
        _resident((None, D_MODEL, EVEN_IN), lambda b, t: (e, 0, 0)),
        _resident((None, len(POOL_WINDOWS), POOL_GDIM, POOL_GDIM), lambda b, t: (e, 0, 0, 0)),
        _resident((None, 1, W_A), lambda b, t: (e, 0, 0)),
        _resident((None, 1, W_B), lambda b, t: (e, 0, 0)),
        _resident((None, 1, W_B), lambda b, t: (e, 0, 0)),
        _resident((None, SGU_HEADS, SGU_CHUNK, SGU_CHUNK), lambda b, t: (e, 0, 0, 0)),
        _resident((None, SGU_CHUNK, SGU_HEADS), lambda b, t: (e, 0, 0)),
        _resident((None, EVEN_MIX, D_MODEL), lambda b, t: (e, 0, 0)),
    ]
    operands = [x, hist, p["norm_pre"], p["norm_post"], p["win_e"], p["wpool"], p["pscale"], p["lng"], p["lnb"],
                p["ws"], p["bst"], p["wout_e"]]
    if pending is not None:
        o_heads, gate, o = pending
        in_specs += [pl.BlockSpec((1, tt, C_MIX), row), pl.BlockSpec((1, tt, C_MIX), row),
                     _resident((None, C_MIX, D_MODEL), lambda b, t: (o, 0, 0)),
                     _resident((None, 1, D_MODEL), lambda b, t: (lyr - 1, 0, 0))]
        operands += [o_heads, gate, p["wo"], p["norm_post"]]
    out_shape = [jax.ShapeDtypeStruct((bx, tx, D_MODEL), F32), jax.ShapeDtypeStruct((bx, HIST_ROWS, W_A), F32)]
    out_specs = [pl.BlockSpec((nb, tt, D_MODEL), row), pl.BlockSpec((nb, HIST_ROWS, W_A), first)]
    if emit_v:
        out_shape.append(jax.ShapeDtypeStruct((bx, tx, W_B), F32))
        out_specs.append(pl.BlockSpec((nb, tt, W_B), row))
    nsub = 2 if nb == 1 and tt % (2 * seg) == 0 else 1
    body = functools.partial(_even_body, nb=nb, tt=tt, seg=seg, pos0=pos0, emit_v=emit_v, nsub=nsub,
                             fused_in=pending is not None)
    return pl.pallas_call(
        body,
        grid=grid,
        in_specs=in_specs,
        out_specs=out_specs,
        out_shape=out_shape,
        scratch_shapes=[pltpu.VMEM((nb, HIST_ROWS + tt, W_A), F32), pltpu.VMEM((nb * tt, EVEN_MIX), F32)],
        compiler_params=pltpu.CompilerParams(
            dimension_semantics=("arbitrary", "arbitrary"), vmem_limit_bytes=VMEM_LIMIT_BYTES),
        name=name,
    )(*operands)


def _proj_body(x_ref, gpre_ref, win_ref, qn_ref, kvn_ref, wq_ref, wuk_ref, cos_ref, sin_ref,
               qa_ref, qr_ref, ckv_ref, kr_ref, ckvb_ref, krb_ref, gate_ref, *, nb, tt):
    m = nb * tt
    x = x_ref[...].reshape(m, D_MODEL)
    h = _rms(x, gpre_ref[...]).astype(BF16)
    z = _dot(h, win_ref[...])
    gate_ref[...] = z[:, ODD_G0:ODD_KR0].reshape(nb, tt, C_MIX)

    ckv = _rms(z[:, ODD_KV0:ODD_G0], kvn_ref[...])
    ckv_ref[...] = ckv.reshape(nb, tt, KV_LORA)
    ckvb_ref[...] = ckv.astype(BF16).reshape(nb, tt, KV_LORA)

    cos = cos_ref[...]
    sin = sin_ref[...]
    kr = (z[:, ODD_KR0:ODD_KR0 + QK_ROPE] * cos[:, :QK_ROPE]
          + z[:, ODD_KR0 + QK_ROPE:ODD_COLS] * sin[:, :QK_ROPE])
    kr_ref[...] = kr.reshape(nb, tt, QK_ROPE)
    krb_ref[...] = kr.astype(BF16).reshape(nb, tt, QK_ROPE)

    qcn = _rms(z[:, ODD_Q0:ODD_KV0], qn_ref[...]).astype(BF16)
    q = _dot(qcn, wq_ref[...])
    c_exp = ATTN_SCALE * LOG2E
    heads_per_vreg = LANES // QK_ROPE
    for c in range(N_HEADS // heads_per_vreg):
        r = ((q[:, QUP_R0 + c * LANES:QUP_R0 + (c + 1) * LANES] * cos
              + q[:, QUP_S0 + c * LANES:QUP_S0 + (c + 1) * LANES] * sin) * c_exp).astype(BF16)
        for i in range(heads_per_vreg):
            qr_ref[:, heads_per_vreg * c + i, :, :] = r[:, i * QK_ROPE:(i + 1) * QK_ROPE].reshape(nb, tt, QK_ROPE)
    for hh in range(N_HEADS):
        q_nope = q[:, hh * QK_NOPE:(hh + 1) * QK_NOPE].astype(BF16)
        qa_ref[:, hh, :, :] = (_dot(q_nope, wuk_ref[hh]) * c_exp).astype(BF16).reshape(nb, tt, KV_LORA)


def _proj_heads_body(x_ref, gpre_ref, win_ref, qn_ref, kvn_ref, wq_ref, wukv_ref, cos_ref, sin_ref, cs_ref,
                     *rest, tt, tile, n_prev, nsub):
    rest = list(rest)
    if n_prev:
        prev_ckv_ref, prev_kr_ref = rest[:2]
        rest = rest[2:]
    q_ref, k_ref, v_ref, ckv_ref, kr_ref, gate_ref = rest
    if n_prev:
        ckv_ref[0:n_prev] = prev_ckv_ref[...]
        kr_ref[0:n_prev] = prev_kr_ref[...]

    rsz = tt // nsub
    st = [dict() for _ in range(nsub)]
    lane = lax.broadcasted_iota(jnp.int32, (rsz, LANES), 1)
    low = lane < QK_ROPE
    v_hi = jnp.ones((rsz, LANES), BF16)
    c_exp = ATTN_SCALE * LOG2E

    def rows(i):
        return slice(i * rsz, (i + 1) * rsz)

    def stage_z(i):
        h = _rms(x_ref[0, rows(i), :], gpre_ref[...]).astype(BF16)
        st[i]["z"] = _dot(h, win_ref[...])

    def stage_latent(i):
        z = st[i]["z"]
        gate_ref[0, rows(i), :] = z[:, ODD_G0:ODD_KR0].astype(BF16)
        ckv = _rms(z[:, ODD_KV0:ODD_G0], kvn_ref[...])
        ckv_ref[n_prev, 0, rows(i), :] = ckv
        u = z[:, ODD_KR0:ODD_COLS] * cs_ref[rows(i), :]
        kr_full = u + pltpu.roll(u, QK_ROPE, axis=1)
        kr_ref[n_prev, 0, rows(i), :] = kr_full[:, :QK_ROPE]
        pos = pl.program_id(1) * tt + i * rsz + lax.broadcasted_iota(jnp.int32, (rsz, LANES), 0)
        onehot = (lane - QK_ROPE == (pos % tile) // CHUNK).astype(F32)
        st[i]["k_hi"] = jnp.where(low, kr_full, onehot).astype(BF16)
        st[i]["kv"] = _dot(ckv.astype(BF16), wukv_ref[...])

    def stage_q(i):
        qcn = _rms(st[i].pop("z")[:, ODD_Q0:ODD_KV0], qn_ref[...]).astype(BF16)
        st[i]["q"] = _dot(qcn, wq_ref[...])

    def stage_store(i):
        q, kv, k_hi = st[i].pop("q"), st[i].pop("kv"), st[i].pop("k_hi")
        cos, sin = cos_ref[rows(i), :], sin_ref[rows(i), :]
        heads_per_vreg = LANES // QK_ROPE
        for hh in range(N_HEADS):
            c, j = divmod(hh, heads_per_vreg)
            r = (q[:, QUP_R0 + c * LANES:QUP_R0 + (c + 1) * LANES] * cos
                 + q[:, QUP_S0 + c * LANES:QUP_S0 + (c + 1) * LANES] * sin) * c_exp
            if j:
                r = pltpu.roll(r, LANES - j * QK_ROPE, axis=1)
            q_ref[0, hh, rows(i), 0:LANES] = (q[:, hh * QK_NOPE:(hh + 1) * QK_NOPE] * c_exp).astype(BF16)
            q_ref[0, hh, rows(i), LANES:2 * LANES] = jnp.where(low, r, 0.0).astype(BF16)
            k_ref[0, hh, rows(i), 0:LANES] = kv[:, hh * QK_NOPE:(hh + 1) * QK_NOPE].astype(BF16)
            k_ref[0, hh, rows(i), LANES:2 * LANES] = k_hi
            v_ref[0, hh, rows(i), 0:LANES] = kv[:, C_MIX + hh * V_DIM:C_MIX + (hh + 1) * V_DIM].astype(BF16)
            v_ref[0, hh, rows(i), LANES:2 * LANES] = v_hi

    if nsub == 1:
        order = [(stage_z, 0), (stage_latent, 0), (stage_q, 0), (stage_store, 0)]
    else:
        order = [(stage_z, 0), (stage_z, 1), (stage_latent, 0), (stage_q, 0), (stage_latent, 1), (stage_store, 0),
                 (stage_q, 1), (stage_store, 1)]
    for stage, i in order:
        stage(i)


def _odd_project_heads(x, cos, sin, cs, lyr, o, p, *, tt, tile, name, prev=None):
    bx, tx, _ = x.shape
    assert tx % tt == 0 and tt % 16 == 0 and tile % CHUNK == 0 and tile // CHUNK <= LANES - QK_ROPE
    n_prev = 0 if prev is None else prev[0].shape[0]
    grid = (bx, tx // tt)
    row = lambda b, t: (b, t, 0)
    hrow = lambda b, t: (b, 0, t, 0)
    srow = lambda b, t: (0, b, t, 0)
    tab = pl.BlockSpec((tt, LANES), lambda b, t: (t, 0))
    in_specs = [
        pl.BlockSpec((1, tt, D_MODEL), row),
        _resident((None, 1, D_MODEL), lambda b, t: (lyr, 0, 0)),
        _resident((None, D_MODEL, ODD_COLS), lambda b, t: (o, 0, 0)),
        _resident((None, 1, Q_LORA), lambda b, t: (o, 0, 0)),
        _resident((None, 1, KV_LORA), lambda b, t: (o, 0, 0)),
        _resident((None, Q_LORA, QUP_COLS), lambda b, t: (o, 0, 0)),
        _resident((None, KV_LORA, 2 * C_MIX), lambda b, t: (o, 0, 0)),
        tab, tab, tab,
    ]
    operands = [x, p["norm_pre"], p["win_o"], p["qn"], p["kvn"], p["wq"], p["wukv"], cos, sin, cs]
    if n_prev:
        in_specs += [pl.BlockSpec((n_prev, 1, tt, KV_LORA), srow), pl.BlockSpec((n_prev, 1, tt, QK_ROPE), srow)]
        operands += list(prev)
    head_arr = jax.ShapeDtypeStruct((bx, N_HEADS, tx, HEAD_LANES), BF16)
    head_spec = pl.BlockSpec((1, N_HEADS, tt, HEAD_LANES), hrow)
    out_shape = [head_arr, head_arr, head_arr,
                 jax.ShapeDtypeStruct((n_prev + 1, bx, tx, KV_LORA), F32),
                 jax.ShapeDtypeStruct((n_prev + 1, bx, tx, QK_ROPE), F32),
                 jax.ShapeDtypeStruct((bx, tx, C_MIX), BF16)]
    out_specs = [head_spec, head_spec, head_spec,
                 pl.BlockSpec((n_prev + 1, 1, tt, KV_LORA), srow),
                 pl.BlockSpec((n_prev + 1, 1, tt, QK_ROPE), srow),
                 pl.BlockSpec((1, tt, C_MIX), row)]
    return pl.pallas_call(
        functools.partial(_proj_heads_body, tt=tt, tile=tile, n_prev=n_prev, nsub=2 if tt % 32 == 0 else 1),
        grid=grid,
        in_specs=in_specs,
        out_specs=out_specs,
        out_shape=out_shape,
        compiler_params=pltpu.CompilerParams(
            dimension_semantics=("arbitrary", "arbitrary"), vmem_limit_bytes=VMEM_LIMIT_BYTES),
        name=name,
    )(*operands)


def _flat_attn_body(q_ref, k_ref, v_ref, o_ref, m_ref, acc_ref, s0_ref, s1_ref, dbias_ref, *, tile, nq, rb):
    nblk = tile // rb
    row = lax.broadcasted_iota(jnp.int32, (tile, HEAD_LANES), 0)
    idx = lax.broadcasted_iota(jnp.int32, (tile, HEAD_LANES), 1) - BIAS_LANE0
    hidden = jnp.logical_and(jnp.logical_and(idx >= 0, idx < tile // CHUNK), idx > row // CHUNK)
    dbias_ref[...] = jnp.where(hidden, MASK_BIAS, 0.0).astype(BF16)
    acc_ref[...] = jnp.zeros(acc_ref.shape, F32)

    def scores(b, qi, j, s_ref):
        k = k_ref[0, 0, pl.ds(pl.multiple_of(j * tile, tile), tile), :]
        q = q_ref[0, 0, pl.ds(pl.multiple_of(qi * tile + b * rb, rb), rb), :]
        q = q + jnp.where(qi == j, dbias_ref[b * rb:(b + 1) * rb], jnp.zeros((rb, HEAD_LANES), BF16))
        s_ref[b * rb:(b + 1) * rb] = _dot_nt(q, k)

    def softmax_pv(b, qi, j, s_ref):
        v = v_ref[0, 0, pl.ds(pl.multiple_of(j * tile, tile), tile), :]
        rs = slice(b * rb, (b + 1) * rb)
        s = s_ref[rs]
        m_old = jnp.where(j == 0, -1e30, m_ref[rs])
        m_new = jnp.maximum(m_old, jnp.max(s, axis=-1, keepdims=True))
        alpha = jnp.exp2(m_old - m_new)
        p = jnp.exp2(s - jnp.concatenate([m_new] * (tile // LANES), axis=1))
        acc = (acc_ref[rs] * jnp.concatenate([alpha] * (HEAD_LANES // LANES), axis=1)
               + _dot(p.astype(BF16), v))
        acc_ref[rs] = acc
        m_ref[rs] = m_new
        o_ref[0, pl.ds(pl.multiple_of(qi * tile + b * rb, rb), rb), :] = (
            acc[:, :V_DIM] * (1.0 / acc[:, V_DIM:])).astype(BF16)

    def step(qi, j, s_cur, s_next):
        wrap = j == qi
        qn = jnp.minimum(jnp.where(wrap, qi + 1, qi), nq - 1)
        jn = jnp.where(wrap, 0, j + 1)
        for b in range(nblk):
            scores(b, qn, jn, s_next)
        for b in range(nblk):
            softmax_pv(b, qi, j, s_cur)
        return qn, jn

    def two_steps(_, carry):
        return step(*step(*carry, s0_ref, s1_ref), s1_ref, s0_ref)

    for b in range(nblk):
        scores(b, 0, 0, s0_ref)
    n_steps = nq * (nq + 1) // 2
    n_pairs = n_steps // 2
    unroll = max(u for u in (1, 2, 3) if n_pairs % u == 0)
    carry = lax.fori_loop(0, n_pairs, two_steps, (jnp.int32(0), jnp.int32(0)), unroll=unroll)
    if n_steps % 2:
        step(*carry, s0_ref, s1_ref)


def _flat_attend(q, k, v, *, tile, name):
    bx, nh, tx, _ = q.shape
    assert tx % tile == 0 and tile % FLAT_BLOCK_ROWS == 0
    head = pl.BlockSpec((1, 1, tx, HEAD_LANES), lambda b, hh: (b, hh, 0, 0))
    return pl.pallas_call(
        functools.partial(_flat_attn_body, tile=tile, nq=tx // tile, rb=FLAT_BLOCK_ROWS),
        grid=(bx, nh),
        in_specs=[head, head, head],
        out_specs=pl.BlockSpec((1, tx, V_DIM), lambda b, hh: (b, 0, hh)),
        out_shape=jax.ShapeDtypeStruct((bx, tx, nh * V_DIM), BF16),
        scratch_shapes=[pltpu.VMEM((tile, LANES), F32), pltpu.VMEM((tile, HEAD_LANES), F32),
                        pltpu.VMEM((tile, tile), F32), pltpu.VMEM((tile, tile), F32),
                        pltpu.VMEM((tile, HEAD_LANES), BF16)],
        compiler_params=pltpu.CompilerParams(
            dimension_semantics=("arbitrary", "arbitrary"), vmem_limit_bytes=VMEM_LIMIT_BYTES),
        name=name,
    )(q, k, v)


def _out_body(o_ref, gate_ref, x_ref, wo_ref, gpost_ref, xo_ref):
    y = _dot((o_ref[0].astype(F32) * _silu(gate_ref[0].astype(F32))).astype(BF16), wo_ref[...])
    xo_ref[0] = x_ref[0] + _rms(y, gpost_ref[...])


def _odd_output(o_heads, gate, x, lyr, o, p, *, tt, name):
    bx, tx, _ = x.shape
    assert tx % tt == 0
    row = lambda b, t: (b, t, 0)
    return pl.pallas_call(
        _out_body,
        grid=(bx, tx // tt),
        in_specs=[pl.BlockSpec((1, tt, C_MIX), row), pl.BlockSpec((1, tt, C_MIX), row),
                  pl.BlockSpec((1, tt, D_MODEL), row),
                  _resident((None, C_MIX, D_MODEL), lambda b, t: (o, 0, 0)),
                  _resident((None, 1, D_MODEL), lambda b, t: (lyr, 0, 0))],
        out_specs=pl.BlockSpec((1, tt, D_MODEL), row),
        out_shape=jax.ShapeDtypeStruct((bx, tx, D_MODEL), F32),
        compiler_params=pltpu.CompilerParams(
            dimension_semantics=("arbitrary", "arbitrary"), vmem_limit_bytes=VMEM_LIMIT_BYTES),
        name=name,
    )(o_heads, gate, x, p["wo"], p["norm_post"])


def _odd_project(x, cos, sin, lyr, o, p, *, nb, tt, name):
    bx, tx, _ = x.shape
    assert bx % nb == 0 and tx % tt == 0 and tt % 16 == 0
    grid = (bx // nb, tx // tt)
    row = lambda b, t: (b, t, 0)
    hrow = lambda b, t: (b, 0, t, 0)
    in_specs = [
        pl.BlockSpec((nb, tt, D_MODEL), row),
        _resident((None, 1, D_MODEL), lambda b, t: (lyr, 0, 0)),
        _resident((None, D_MODEL, ODD_COLS), lambda b, t: (o, 0, 0)),
        _resident((None, 1, Q_LORA), lambda b, t: (o, 0, 0)),
        _resident((None, 1, KV_LORA), lambda b, t: (o, 0, 0)),
        _resident((None, Q_LORA, QUP_COLS), lambda b, t: (o, 0, 0)),
        _resident((None, N_HEADS, QK_NOPE, KV_LORA), lambda b, t: (o, 0, 0, 0)),
        pl.BlockSpec((nb * tt, LANES), lambda b, t: (t, 0)),
        pl.BlockSpec((nb * tt, LANES), lambda b, t: (t, 0)),
    ]
    out_shape = [
        jax.ShapeDtypeStruct((bx, N_HEADS, tx, KV_LORA), BF16),
        jax.ShapeDtypeStruct((bx, N_HEADS, tx, QK_ROPE), BF16),
        jax.ShapeDtypeStruct((bx, tx, KV_LORA), F32),
        jax.ShapeDtypeStruct((bx, tx, QK_ROPE), F32),
        jax.ShapeDtypeStruct((bx, tx, KV_LORA), BF16),
        jax.ShapeDtypeStruct((bx, tx, QK_ROPE), BF16),
        jax.ShapeDtypeStruct((bx, tx, C_MIX), F32),
    ]
    out_specs = [
        pl.BlockSpec((nb, N_HEADS, tt, KV_LORA), hrow),
        pl.BlockSpec((nb, N_HEADS, tt, QK_ROPE), hrow),
        pl.BlockSpec((nb, tt, KV_LORA), row),
        pl.BlockSpec((nb, tt, QK_ROPE), row),
        pl.BlockSpec((nb, tt, KV_LORA), row),
        pl.BlockSpec((nb, tt, QK_ROPE), row),
        pl.BlockSpec((nb, tt, C_MIX), row),
    ]
    return pl.pallas_call(
        functools.partial(_proj_body, nb=nb, tt=tt),
        grid=grid,
        in_specs=in_specs,
        out_specs=out_specs,
        out_shape=out_shape,
        compiler_params=pltpu.CompilerParams(
            dimension_semantics=("arbitrary", "arbitrary"), vmem_limit_bytes=VMEM_LIMIT_BYTES),
        name=name,
    )(x, p["norm_pre"], p["win_o"], p["qn"], p["kvn"], p["wq"], p["wuk"], cos, sin)


def _cached_attn_body(qa_ref, qr_ref, cckv_ref, ckrt_ref, nckv_ref, nkr_ref, o_ref, *, tq, past, ns):
    rows = N_HEADS * tq
    kpos = past + lax.broadcasted_iota(jnp.int32, (1, tq, tq), 2)
    qpos = past + lax.broadcasted_iota(jnp.int32, (1, tq, tq), 1)
    visible = kpos // CHUNK <= qpos // CHUNK
    staged = []
    for s in range(ns):
        qa = qa_ref[s].reshape(rows, KV_LORA)
        qr = qr_ref[s].reshape(rows, QK_ROPE)
        kc = cckv_ref[s].astype(BF16)
        kn = nckv_ref[s]
        s_old = _dot_nt(qa, kc) + _dot(qr, ckrt_ref[s].astype(BF16))
        s_new = _dot_nt(qa, kn) + _dot_nt(qr, nkr_ref[s])
        s_new = jnp.where(visible, s_new.reshape(N_HEADS, tq, tq), -jnp.inf).reshape(rows, tq)
        staged.append((s_old, s_new, kc, kn))
    for s, (s_old, s_new, kc, kn) in enumerate(staged):
        m = jnp.maximum(jnp.max(s_old, axis=-1, keepdims=True), jnp.max(s_new, axis=-1, keepdims=True))
        p_old = jnp.exp2(s_old - m)
        p_new = jnp.exp2(s_new - m)
        l = jnp.sum(p_old, axis=-1, keepdims=True) + jnp.sum(p_new, axis=-1, keepdims=True)
        o_lat = (_dot(p_old.astype(BF16), kc) + _dot(p_new.astype(BF16), kn)) * (1.0 / l)
        o_ref[:, s] = o_lat.astype(BF16).reshape(N_HEADS, tq, KV_LORA)


def _cached_attend(qa, qr, cache_ckv, cache_krope_t, ckvb, krb, o, *, name):
    bx, _, tq, _ = qa.shape
    past = cache_ckv.shape[2]
    assert tq % 16 == 0
    ns = max(n for n in (1, 2, 4) if bx % n == 0)
    in_specs = [
        pl.BlockSpec((ns, N_HEADS, tq, KV_LORA), lambda b: (b, 0, 0, 0)),
        pl.BlockSpec((ns, N_HEADS, tq, QK_ROPE), lambda b: (b, 0, 0, 0)),
        pl.BlockSpec((None, ns, past, KV_LORA), lambda b: (o, b, 0, 0)),
        pl.BlockSpec((None, ns, QK_ROPE, past), lambda b: (o, b, 0, 0)),
        pl.BlockSpec((ns, tq, KV_LORA), lambda b: (b, 0, 0)),
        pl.BlockSpec((ns, tq, QK_ROPE), lambda b: (b, 0, 0)),
    ]
    return pl.pallas_call(
        functools.partial(_cached_attn_body, tq=tq, past=past, ns=ns),
        grid=(bx // ns,),
        in_specs=in_specs,
        out_specs=pl.BlockSpec((N_HEADS, ns, tq, KV_LORA), lambda b: (0, b, 0, 0)),
        out_shape=jax.ShapeDtypeStruct((N_HEADS, bx, tq, KV_LORA), BF16),
        compiler_params=pltpu.CompilerParams(
            dimension_semantics=("arbitrary",), vmem_limit_bytes=VMEM_LIMIT_BYTES),
        name=name,
    )(qa, qr, cache_ckv, cache_krope_t, ckvb, krb)


def _latent_out_body(olat_ref, gate_ref, x_ref, wuv_ref, wo_ref, gpost_ref, xo_ref):
    o = jnp.concatenate([_dot(olat_ref[hh], wuv_ref[hh]) for hh in range(N_HEADS)], axis=1)
    y = _dot((o * _silu(gate_ref[...])).astype(BF16), wo_ref[...])
    xo_ref[...] = x_ref[...] + _rms(y, gpost_ref[...])


def _latent_output(o_lat, gate, x, lyr, o, p, *, name):
    m = x.shape[0]
    whole = lambda shape: pl.BlockSpec(shape, lambda i: (0,) * len(shape))
    return pl.pallas_call(
        _latent_out_body,
        grid=(1,),
        in_specs=[whole((N_HEADS, m, KV_LORA)), whole((m, C_MIX)), whole((m, D_MODEL)),
                  pl.BlockSpec((None, N_HEADS, KV_LORA, V_DIM), lambda i: (o, 0, 0, 0)),
                  pl.BlockSpec((None, C_MIX, D_MODEL), lambda i: (o, 0, 0)),
                  pl.BlockSpec((None, 1, D_MODEL), lambda i: (lyr, 0, 0))],
        out_specs=whole((m, D_MODEL)),
        out_shape=jax.ShapeDtypeStruct((m, D_MODEL), F32),
        compiler_params=pltpu.CompilerParams(
            dimension_semantics=("arbitrary",), vmem_limit_bytes=VMEM_LIMIT_BYTES),
        name=name,
    )(o_lat, gate, x, p["wuv"], p["wo"], p["norm_post"])


def _rope_tables(pos0, n, copies=1):
    half = QK_ROPE // 2
    freqs = ROPE_BASE ** (-np.arange(half, dtype=np.float64) / half)
    ang = (pos0 + np.arange(n, dtype=np.float64))[:, None] * freqs[None, :]
    c, s = np.cos(ang), np.sin(ang)
    cos64, sin64 = np.concatenate([c, c], axis=1), np.concatenate([-s, s], axis=1)
    tables = (np.tile(cos64, (copies, LANES // QK_ROPE)), np.tile(sin64, (copies, LANES // QK_ROPE)),
              np.tile(np.concatenate([cos64, sin64], axis=1), (copies, 1)))
    return tuple(jnp.asarray(tab, F32) for tab in tables)


def _half_swap(w):
    half = QK_ROPE // 2
    return jnp.concatenate([w[..., half:], w[..., :half]], axis=-1)


def _prepare_params(norm_pre, norm_post, w_in_even, w_pool, pool_scale, sgu_ln_g, sgu_ln_b, w_spatial, b_spatial,
                    w_out_even, w_in_odd, q_norm, kv_norm, w_q_up, w_kv_up, w_o):
    n_odd = w_in_odd.shape[0]
    kr_w = w_in_odd[..., Q_LORA + KV_LORA:Q_LORA + KV_LORA + QK_ROPE]
    win_o = jnp.concatenate([w_in_odd[..., :Q_LORA + KV_LORA], w_in_odd[..., Q_LORA + KV_LORA + QK_ROPE:],
                             kr_w, _half_swap(kr_w)], axis=-1)
    wq_r = w_q_up[..., QK_NOPE:]
    wq = jnp.concatenate([w_q_up[..., :QK_NOPE].reshape(n_odd, Q_LORA, N_HEADS * QK_NOPE),
                          wq_r.reshape(n_odd, Q_LORA, N_HEADS * QK_ROPE),
                          _half_swap(wq_r).reshape(n_odd, Q_LORA, N_HEADS * QK_ROPE)], axis=-1)
    return {
        "norm_pre": norm_pre[:, None, :],
        "norm_post": norm_post[:, None, :],
        "win_e": w_in_even.astype(BF16),
        "wpool": w_pool.astype(BF16),
        "pscale": pool_scale[:, None, :],
        "lng": sgu_ln_g[:, None, :],
        "lnb": sgu_ln_b[:, None, :],
        "ws": w_spatial,
        "bst": jnp.swapaxes(b_spatial, 1, 2),
        "wout_e": w_out_even.astype(BF16),
        "win_o": win_o.astype(BF16),
        "qn": q_norm[:, None, :],
        "kvn": kv_norm[:, None, :],
        "wq": wq.astype(BF16),
        "wuk": jnp.transpose(w_kv_up[..., :QK_NOPE], (0, 2, 3, 1)).astype(BF16),
        "wuv": jnp.transpose(w_kv_up[..., QK_NOPE:], (0, 2, 1, 3)).astype(BF16),
        "wukv": jnp.concatenate([w_kv_up[..., :QK_NOPE].reshape(n_odd, KV_LORA, N_HEADS * QK_NOPE),
                                 w_kv_up[..., QK_NOPE:].reshape(n_odd, KV_LORA, C_MIX)], axis=-1).astype(BF16),
        "wo": w_o.astype(BF16),
    }


def _tile_sizes(t_prompt):
    tt_even = min(t_prompt, 512)
    tt_proj = min(t_prompt, 512)
    tile = min(t_prompt, 1024)
    tt_out = min(t_prompt, 1024)
    return tt_even, tt_proj, tile, tt_out


def kernel(x_prompt, x_sample, cache_pool, cache_ckv, cache_krope, norm_pre, norm_post, w_in_even, w_pool, pool_scale, sgu_ln_g, sgu_ln_b, w_spatial, b_spatial, w_out_even, w_in_odd, q_norm, kv_norm, w_q_up, w_kv_up, w_o):
    depth = norm_pre.shape[0]
    b, t, _ = x_prompt.shape
    db, s, _ = x_sample.shape
    past = cache_ckv.shape[2]
    p = _prepare_params(norm_pre, norm_post, w_in_even, w_pool, pool_scale, sgu_ln_g, sgu_ln_b, w_spatial,
                        b_spatial, w_out_even, w_in_odd, q_norm, kv_norm, w_q_up, w_kv_up, w_o)
    tt_even, tt_proj, tile, tt_out = _tile_sizes(t)

    cos_p, sin_p, cs_p = _rope_tables(0, t)
    cos_s, sin_s, _ = _rope_tables(past, s, copies=db)
    cache_krope_t = jnp.swapaxes(cache_krope, 2, 3)

    xp, xs = x_prompt, x_sample
    pool_p, pool_s, sgu_s = [], [], []
    ckv_s, kr_s = [], []
    zero_hist = jnp.zeros((b, HIST_ROWS, W_A), F32)
    pending = None
    latent_p = None
    for lyr in range(depth):
        if lyr % 2 == 0:
            e = lyr // 2
            xp, hp = _even_layer(xp, zero_hist, lyr, e, p, nb=1, tt=tt_even, pos0=0, emit_v=False,
                                 name=f"even{e}_prompt", pending=pending)
            pending = None
            hist_s = jnp.pad(cache_pool[e], ((0, 0), (HIST_ROWS - POOL_HIST, 0), (0, 0)))
            xs, hs, vs = _even_layer(xs, hist_s, lyr, e, p, nb=db, tt=s, pos0=past, emit_v=True,
                                     name=f"even{e}_sample")
            pool_p.append(hp[:, HIST_ROWS - POOL_HIST:])
            pool_s.append(hs[:, HIST_ROWS - POOL_HIST:])
            sgu_s.append(vs)
        else:
            o = lyr // 2
            qh, kh, vh, ckv_stack, kr_stack, gate = _odd_project_heads(
                xp, cos_p, sin_p, cs_p, lyr, o, p, tt=tt_proj, tile=tile, name=f"proj{o}_prompt", prev=latent_p)
            latent_p = (ckv_stack, kr_stack)
            o_heads = _flat_attend(qh, kh, vh, tile=tile, name=f"attn{o}_prompt")
            if lyr + 1 < depth:
                pending = (o_heads, gate, o)
            else:
                xp = _odd_output(o_heads, gate, xp, lyr, o, p, tt=tt_out, name=f"out{o}_prompt")

            qa, qr, ckv, kr, ckvb, krb, gate = _odd_project(xs, cos_s, sin_s, lyr, o, p, nb=db, tt=s,
                                                            name=f"proj{o}_sample")
            o_lat = _cached_attend(qa, qr, cache_ckv, cache_krope_t, ckvb, krb, o, name=f"attn{o}_sample")
            xs = _latent_output(o_lat.reshape(N_HEADS, db * s, KV_LORA), gate.reshape(db * s, C_MIX),
                                xs.reshape(db * s, D_MODEL), lyr, o, p, name=f"out{o}_sample").reshape(db, s, D_MODEL)
            ckv_s.append(ckv)
            kr_s.append(kr)
    return (xp, xs, jnp.stack(pool_p), jnp.stack(pool_s), jnp.stack(sgu_s),
            latent_p[0], latent_p[1], jnp.stack(ckv_s), jnp.stack(kr_s))
```

```python
import functools

import jax
import jax.numpy as jnp
import numpy as np
from jax import lax
from jax.experimental import pallas as pl
from jax.experimental.pallas import tpu as pltpu

D_MODEL = 1024
CHUNK = 64
EPS = 1e-6
POOL_WINDOWS = (2, 4, 8, 16)
POOL_HIST = max(POOL_WINDOWS) - 1
W_A = D_MODEL
POOL_GDIM = W_A // len(POOL_WINDOWS)
W_B = D_MODEL
SGU_HEADS = 4
SGU_HDIM = W_B // SGU_HEADS
SGU_CHUNK = 128
EVEN_MIX = W_A + W_B
EVEN_IN = W_A + 2 * W_B + EVEN_MIX
N_HEADS = D_MODEL // 128
QK_NOPE = 128
QK_ROPE = 64
V_DIM = 128
KV_LORA = D_MODEL // 4
Q_LORA = 3 * D_MODEL // 8
C_MIX = N_HEADS * V_DIM
ROPE_BASE = 10000.0
ATTN_SCALE = (QK_NOPE + QK_ROPE) ** -0.5
LOG2E = 1.4426950408889634

LANES = 128
HIST_ROWS = 16
VMEM_LIMIT_BYTES = 56 * 1024 * 1024
FLAT_BLOCK_ROWS = 1024
HEAD_LANES = 2 * LANES
BIAS_LANE0 = LANES + QK_ROPE
MASK_BIAS = -1e30

ODD_Q0, ODD_KV0, ODD_G0, ODD_KR0 = 0, Q_LORA, Q_LORA + KV_LORA, Q_LORA + KV_LORA + C_MIX
ODD_COLS = ODD_KR0 + 2 * QK_ROPE
QUP_R0 = N_HEADS * QK_NOPE
QUP_S0 = QUP_R0 + N_HEADS * QK_ROPE
QUP_COLS = QUP_S0 + N_HEADS * QK_ROPE

F32 = jnp.float32
BF16 = jnp.bfloat16


def _dot(a, b):
    return jnp.dot(a, b, preferred_element_type=F32)


def _dot_nt(a, b):
    return lax.dot_general(a, b, (((1,), (1,)), ((), ())), preferred_element_type=F32)


def _rms(x, g):
    return x * lax.rsqrt(jnp.mean(x * x, axis=-1, keepdims=True) + EPS) * g


def _silu(x):
    return x * (1.0 / (1.0 + jnp.exp(-x)))


def _gelu(x):
    return 0.5 * x * (1.0 + lax.erf(x * (2.0 ** -0.5)))


def _even_body(x_ref, hist_ref, gpre_ref, gpost_ref, win_ref, wpool_ref, pscale_ref, lng_ref, lnb_ref,
               ws_ref, bst_ref, wout_ref, *rest, nb, tt, seg, pos0, emit_v, nsub, fused_in):
    rest = list(rest)
    if fused_in:
        oin_ref, gin_ref, wo_ref, gprev_ref = rest[:4]
        rest = rest[4:]
    xo_ref, histo_ref = rest[:2]
    if emit_v:
        vo_ref, aext_ref, mix_ref = rest[2:]
    else:
        aext_ref, mix_ref = rest[2:]
    t = pl.program_id(1)
    nt = pl.num_programs(1)
    m = nb * tt
    rsz = m // nsub
    rps = tt // nsub if nb == 1 else tt
    st = [dict() for _ in range(nsub)]

    @pl.when(t == 0)
    def _():
        aext_ref[:, 0:HIST_ROWS, :] = hist_ref[...]

    def rows(i):
        return slice(i * rsz, (i + 1) * rsz)

    def stage_x(i):
        pending = oin_ref[0, rows(i), :].astype(F32) * _silu(gin_ref[0, rows(i), :].astype(F32))
        xo_ref[0, rows(i), :] = x_ref[0, rows(i), :] + _rms(_dot(pending.astype(BF16), wo_ref[...]), gprev_ref[...])

    def stage_a(i):
        if fused_in:
            x = xo_ref[0, rows(i), :]
        else:
            x = x_ref[0, rows(i), :] if nb == 1 else x_ref[...].reshape(m, D_MODEL)
        h = _rms(x, gpre_ref[...]).astype(BF16)
        a = _dot(h, win_ref[:, 0:W_A])
        if nb == 1:
            aext_ref[0, HIST_ROWS + i * rsz:HIST_ROWS + (i + 1) * rsz, :] = a
        else:
            aext_ref[:, HIST_ROWS:HIST_ROWS + tt, :] = a.reshape(nb, tt, W_A)
        st[i]["h"] = h

    def stage_uv(i):
        st[i]["uv"] = _dot(st[i]["h"], win_ref[:, W_A:W_A + 2 * W_B])

    def stage_gate(i):
        st[i]["gate"] = _dot(st[i].pop("h"), win_ref[:, W_A + 2 * W_B:])

    def stage_pool(i):
        r0 = HIST_ROWS + (i * rsz if nb == 1 else 0)
        pos = pos0 + t * tt + (r0 - HIST_ROWS) + lax.broadcasted_iota(jnp.int32, (rps, 1), 0)
        for g, w in enumerate(POOL_WINDOWS):
            cs = slice(g * POOL_GDIM, (g + 1) * POOL_GDIM)
            inv_cnt = 1.0 / jnp.minimum(pos + 1, w).astype(F32)
            for s in range(nb):
                wsum = aext_ref[s, r0 - HIST_ROWS:r0 + rps, cs]
                k = 1
                while k < w:
                    wsum = wsum + pltpu.roll(wsum, k, axis=0)
                    k *= 2
                cur = aext_ref[s, r0:r0 + rps, cs]
                mix_ref[i * rsz + s * rps:i * rsz + (s + 1) * rps, cs] = wsum[HIST_ROWS:] * inv_cnt - cur
            y_a = _dot(mix_ref[rows(i), cs].astype(BF16), wpool_ref[g]) * pscale_ref[:, cs]
            mix_ref[rows(i), cs] = y_a

    def stage_sgu(i):
        uv = _gelu(st[i].pop("uv"))
        u = uv[:, :W_B]
        v = uv[:, W_B:]
        vc = v - jnp.mean(v, axis=-1, keepdims=True)
        vn = vc * lax.rsqrt(jnp.mean(vc * vc, axis=-1, keepdims=True) + EPS) * lng_ref[...] + lnb_ref[...]
        if emit_v:
            vo_ref[...] = vn.reshape(nb, tt, W_B)
        vb = vn.astype(BF16)
        ii = lax.broadcasted_iota(jnp.int32, (seg, seg), 0)
        jj = lax.broadcasted_iota(jnp.int32, (seg, seg), 1)
        causal = (jj // CHUNK) <= (ii // CHUNK)
        for g in range(SGU_HEADS):
            hs = slice(g * SGU_HDIM, (g + 1) * SGU_HDIM)
            wsg = jnp.where(causal, ws_ref[g, 0:seg, 0:seg], 0.0).astype(BF16)
            bcol = bst_ref[0:seg, g:g + 1]
            for c in range(rsz // seg):
                rs = slice(c * seg, (c + 1) * seg)
                mixed = _dot(wsg, vb[rs, hs]) + bcol
                mix_ref[i * rsz + c * seg:i * rsz + (c + 1) * seg, W_A + g * SGU_HDIM:W_A + (g + 1) * SGU_HDIM] = (
                    u[rs, hs] * mixed)

    def stage_out(i):
        mixv = (mix_ref[rows(i)] * _silu(st[i].pop("gate"))).astype(BF16)
        y = _rms(_dot(mixv, wout_ref[...]), gpost_ref[...])
        if nb == 1:
            x_in = xo_ref if fused_in else x_ref
            xo_ref[0, rows(i), :] = x_in[0, rows(i), :] + y
        else:
            xo_ref[...] = x_ref[...] + y.reshape(nb, tt, D_MODEL)

    if nsub == 1:
        order = [(stage_a, 0), (stage_uv, 0), (stage_pool, 0), (stage_gate, 0), (stage_sgu, 0), (stage_out, 0)]
    else:
        order = [(stage_a, 0), (stage_uv, 0), (stage_a, 1), (stage_sgu, 0), (stage_uv, 1), (stage_pool, 0),
                 (stage_gate, 0), (stage_sgu, 1), (stage_pool, 1), (stage_out, 0), (stage_gate, 1), (stage_out, 1)]
    if fused_in:
        order = [(stage_x, i) for i in range(nsub)] + order
    for stage, i in order:
        stage(i)

    tail = aext_ref[:, tt:tt + HIST_ROWS, :]

    @pl.when(t == nt - 1)
    def _():
        histo_ref[...] = tail

    aext_ref[:, 0:HIST_ROWS, :] = tail


def _resident(block_shape, index_map):
    return pl.BlockSpec(block_shape, index_map, pipeline_mode=pl.Buffered(1))


def _even_layer(x, hist, lyr, e, p, *, nb, tt, pos0, emit_v, name, pending=None):
    bx, tx, _ = x.shape
    seg = min(tx, SGU_CHUNK)
    assert bx % nb == 0 and tx % tt == 0 and tt % seg == 0 and tt >= HIST_ROWS and tt % 8 == 0
    assert pending is None or nb == 1
    grid = (bx // nb, tx // tt)
    row = lambda b, t: (b, t, 0)
    first = lambda b, t: (b, 0, 0)
    in_specs = [
        pl.BlockSpec((nb, tt, D_MODEL), row),
        pl.BlockSpec((nb, HIST_ROWS, W_A), first),
        _resident((None, 1, D_MODEL), lambda b, t: (lyr, 0, 0)),
        _resident((None, 1, D_MODEL), lambda b, t: (lyr, 0, 0)),
        _resident((None, D_MODEL, EVEN_IN), lambda b, t: (e, 0, 0)),
        _resident((None, len(POOL_WINDOWS), POOL_GDIM, POOL_GDIM), lambda b, t: (e, 0, 0, 0)),
        _resident((None, 1, W_A), lambda b, t: (e, 0, 0)),
        _resident((None, 1, W_B), lambda b, t: (e, 0, 0)),
        _resident((None, 1, W_B), lambda b, t: (e, 0, 0)),
        _resident((None, SGU_HEADS, SGU_CHUNK, SGU_CHUNK), lambda b, t: (e, 0, 0, 0)),
        _resident((None, SGU_CHUNK, SGU_HEADS), lambda b, t: (e, 0, 0)),
        _resident((None, EVEN_MIX, D_MODEL), lambda b, t: (e, 0, 0)),
    ]
    operands = [x, hist, p["norm_pre"], p["norm_post"], p["win_e"], p["wpool"], p["pscale"], p["lng"], p["lnb"],
                p["ws"], p["bst"], p["wout_e"]]
    if pending is not None:
        o_heads, gate, o = pending
        in_specs += [pl.BlockSpec((1, tt, C_MIX), row), pl.BlockSpec((1, tt, C_MIX), row),
                     _resident((None, C_MIX, D_MODEL), lambda b, t: (o, 0, 0)),
                     _resident((None, 1, D_MODEL), lambda b, t: (lyr - 1, 0, 0))]
        operands += [o_heads, gate, p["wo"], p["norm_post"]]
    out_shape = [jax.ShapeDtypeStruct((bx, tx, D_MODEL), F32), jax.ShapeDtypeStruct((bx, HIST_ROWS, W_A), F32)]
    out_specs = [pl.BlockSpec((nb, tt, D_MODEL), row), pl.BlockSpec((nb, HIST_ROWS, W_A), first)]
    if emit_v:
        out_shape.append(jax.ShapeDtypeStruct((bx, tx, W_B), F32))
        out_specs.append(pl.BlockSpec((nb, tt, W_B), row))
    nsub = 2 if nb == 1 and tt % (2 * seg) == 0 else 1
    body = functools.partial(_even_body, nb=nb, tt=tt, seg=seg, pos0=pos0, emit_v=emit_v, nsub=nsub,
                             fused_in=pending is not None)
    return pl.pallas_call(
        body,
        grid=grid,
        in_specs=in_specs,
        out_specs=out_specs,
        out_shape=out_shape,
        scratch_shapes=[pltpu.VMEM((nb, HIST_ROWS + tt, W_A), F32), pltpu.VMEM((nb * tt, EVEN_MIX), F32)],
        compiler_params=pltpu.CompilerParams(
            dimension_semantics=("arbitrary", "arbitrary"), vmem_limit_bytes=VMEM_LIMIT_BYTES),
        name=name,
    )(*operands)


def _proj_body(x_ref, gpre_ref, win_ref, qn_ref, kvn_ref, wq_ref, wuk_ref, cos_ref, sin_ref,
               qa_ref, qr_ref, ckv_ref, kr_ref, ckvb_ref, krb_ref, gate_ref, *, nb, tt):
    m = nb * tt
    x = x_ref[...].reshape(m, D_MODEL)
    h = _rms(x, gpre_ref[...]).astype(BF16)
    z = _dot(h, win_ref[...])
    gate_ref[...] = z[:, ODD_G0:ODD_KR0].reshape(nb, tt, C_MIX)

    ckv = _rms(z[:, ODD_KV0:ODD_G0], kvn_ref[...])
    ckv_ref[...] = ckv.reshape(nb, tt, KV_LORA)
    ckvb_ref[...] = ckv.astype(BF16).reshape(nb, tt, KV_LORA)

    cos = cos_ref[...]
    sin = sin_ref[...]
    kr = (z[:, ODD_KR0:ODD_KR0 + QK_ROPE] * cos[:, :QK_ROPE]
          + z[:, ODD_KR0 + QK_ROPE:ODD_COLS] * sin[:, :QK_ROPE])
    kr_ref[...] = kr.reshape(nb, tt, QK_ROPE)
    krb_ref[...] = kr.astype(BF16).reshape(nb, tt, QK_ROPE)

    qcn = _rms(z[:, ODD_Q0:ODD_KV0], qn_ref[...]).astype(BF16)
    q = _dot(qcn, wq_ref[...])
    c_exp = ATTN_SCALE * LOG2E
    heads_per_vreg = LANES // QK_ROPE
    for c in range(N_HEADS // heads_per_vreg):
        r = ((q[:, QUP_R0 + c * LANES:QUP_R0 + (c + 1) * LANES] * cos
              + q[:, QUP_S0 + c * LANES:QUP_S0 + (c + 1) * LANES] * sin) * c_exp).astype(BF16)
        for i in range(heads_per_vreg):
            qr_ref[:, heads_per_vreg * c + i, :, :] = r[:, i * QK_ROPE:(i + 1) * QK_ROPE].reshape(nb, tt, QK_ROPE)
    for hh in range(N_HEADS):
        q_nope = q[:, hh * QK_NOPE:(hh + 1) * QK_NOPE].astype(BF16)
        qa_ref[:, hh, :, :] = (_dot(q_nope, wuk_ref[hh]) * c_exp).astype(BF16).reshape(nb, tt, KV_LORA)


def _proj_heads_body(x_ref, gpre_ref, win_ref, qn_ref, kvn_ref, wq_ref, wukv_ref, cos_ref, sin_ref, cs_ref,
                     *rest, tt, tile, n_prev):
    x = x_ref[0]
    h = _rms(x, gpre_ref[...]).astype(BF16)
    z = _dot(h, win_ref[...])
    rest = list(rest)
    if n_prev:
        prev_ckv_ref, prev_kr_ref = rest[:2]
        rest = rest[2:]
    q_ref, k_ref, v_ref, ckv_ref, kr_ref, gate_ref = rest
    if n_prev:
        ckv_ref[0:n_prev] = prev_ckv_ref[...]
        kr_ref[0:n_prev] = prev_kr_ref[...]
    gate_ref[0] = z[:, ODD_G0:ODD_KR0].astype(BF16)
    ckv = _rms(z[:, ODD_KV0:ODD_G0], kvn_ref[...])
    ckv_ref[n_prev, 0] = ckv

    lane = lax.broadcasted_iota(jnp.int32, (tt, LANES), 1)
    low = lane < QK_ROPE
    u = z[:, ODD_KR0:ODD_COLS] * cs_ref[...]
    kr_full = u + pltpu.roll(u, QK_ROPE, axis=1)
    kr_ref[n_prev, 0] = kr_full[:, :QK_ROPE]
    pos = pl.program_id(1) * tt + lax.broadcasted_iota(jnp.int32, (tt, LANES), 0)
    onehot = (lane - QK_ROPE == (pos % tile) // CHUNK).astype(F32)
    k_hi = jnp.where(low, kr_full, onehot).astype(BF16)
    v_hi = jnp.ones((tt, LANES), BF16)

    kv = _dot(ckv.astype(BF16), wukv_ref[...])
    qcn = _rms(z[:, ODD_Q0:ODD_KV0], qn_ref[...]).astype(BF16)
    q = _dot(qcn, wq_ref[...])
    c_exp = ATTN_SCALE * LOG2E
    heads_per_vreg = LANES // QK_ROPE
    for hh in range(N_HEADS):
        c, i = divmod(hh, heads_per_vreg)
        r = (q[:, QUP_R0 + c * LANES:QUP_R0 + (c + 1) * LANES] * cos_ref[...]
             + q[:, QUP_S0 + c * LANES:QUP_S0 + (c + 1) * LANES] * sin_ref[...]) * c_exp
        if i:
            r = pltpu.roll(r, LANES - i * QK_ROPE, axis=1)
        q_ref[0, hh, :, 0:LANES] = (q[:, hh * QK_NOPE:(hh + 1) * QK_NOPE] * c_exp).astype(BF16)
        q_ref[0, hh, :, LANES:2 * LANES] = jnp.where(low, r, 0.0).astype(BF16)
        k_ref[0, hh, :, 0:LANES] = kv[:, hh * QK_NOPE:(hh + 1) * QK_NOPE].astype(BF16)
        k_ref[0, hh, :, LANES:2 * LANES] = k_hi
        v_ref[0, hh, :, 0:LANES] = kv[:, C_MIX + hh * V_DIM:C_MIX + (hh + 1) * V_DIM].astype(BF16)
        v_ref[0, hh, :, LANES:2 * LANES] = v_hi


def _odd_project_heads(x, cos, sin, cs, lyr, o, p, *, tt, tile, name, prev=None):
    bx, tx, _ = x.shape
    assert tx % tt == 0 and tt % 16 == 0 and tile % CHUNK == 0 and tile // CHUNK <= LANES - QK_ROPE
    n_prev = 0 if prev is None else prev[0].shape[0]
    grid = (bx, tx // tt)
    row = lambda b, t: (b, t, 0)
    hrow = lambda b, t: (b, 0, t, 0)
    srow = lambda b, t: (0, b, t, 0)
    tab = pl.BlockSpec((tt, LANES), lambda b, t: (t, 0))
    in_specs = [
        pl.BlockSpec((1, tt, D_MODEL), row),
        _resident((None, 1, D_MODEL), lambda b, t: (lyr, 0, 0)),
        _resident((None, D_MODEL, ODD_COLS), lambda b, t: (o, 0, 0)),
        _resident((None, 1, Q_LORA), lambda b, t: (o, 0, 0)),
        _resident((None, 1, KV_LORA), lambda b, t: (o, 0, 0)),
        _resident((None, Q_LORA, QUP_COLS), lambda b, t: (o, 0, 0)),
        _resident((None, KV_LORA, 2 * C_MIX), lambda b, t: (o, 0, 0)),
        tab, tab, tab,
    ]
    operands = [x, p["norm_pre"], p["win_o"], p["qn"], p["kvn"], p["wq"], p["wukv"], cos, sin, cs]
    if n_prev:
        in_specs += [pl.BlockSpec((n_prev, 1, tt, KV_LORA), srow), pl.BlockSpec((n_prev, 1, tt, QK_ROPE), srow)]
        operands += list(prev)
    head_arr = jax.ShapeDtypeStruct((bx, N_HEADS, tx, HEAD_LANES), BF16)
    head_spec = pl.BlockSpec((1, N_HEADS, tt, HEAD_LANES), hrow)
    out_shape = [head_arr, head_arr, head_arr,
                 jax.ShapeDtypeStruct((n_prev + 1, bx, tx, KV_LORA), F32),
                 jax.ShapeDtypeStruct((n_prev + 1, bx, tx, QK_ROPE), F32),
                 jax.ShapeDtypeStruct((bx, tx, C_MIX), BF16)]
    out_specs = [head_spec, head_spec, head_spec,
                 pl.BlockSpec((n_prev + 1, 1, tt, KV_LORA), srow),
                 pl.BlockSpec((n_prev + 1, 1, tt, QK_ROPE), srow),
                 pl.BlockSpec((1, tt, C_MIX), row)]
    return pl.pallas_call(
        functools.partial(_proj_heads_body, tt=tt, tile=tile, n_prev=n_prev),
        grid=grid,
        in_specs=in_specs,
        out_specs=out_specs,
        out_shape=out_shape,
        compiler_params=pltpu.CompilerParams(
            dimension_semantics=("arbitrary", "arbitrary"), vmem_limit_bytes=VMEM_LIMIT_BYTES),
        name=name,
    )(*operands)


def _flat_attn_body(q_ref, k_ref, v_ref, o_ref, m_ref, acc_ref, s0_ref, s1_ref, dbias_ref, *, tile, nq, rb):
    nblk = tile // rb
    row = lax.broadcasted_iota(jnp.int32, (tile, HEAD_LANES), 0)
    idx = lax.broadcasted_iota(jnp.int32, (tile, HEAD_LANES), 1) - BIAS_LANE0
    hidden = jnp.logical_and(jnp.logical_and(idx >= 0, idx < tile // CHUNK), idx > row // CHUNK)
    dbias_ref[...] = jnp.where(hidden, MASK_BIAS, 0.0).astype(BF16)
    acc_ref[...] = jnp.zeros(acc_ref.shape, F32)

    def scores(b, qi, j, s_ref):
        k = k_ref[0, 0, pl.ds(pl.multiple_of(j * tile, tile), tile), :]
        q = q_ref[0, 0, pl.ds(pl.multiple_of(qi * tile + b * rb, rb), rb), :]
        q = q + jnp.where(qi == j, dbias_ref[b * rb:(b + 1) * rb], jnp.zeros((rb, HEAD_LANES), BF16))
        s_ref[b * rb:(b + 1) * rb] = _dot_nt(q, k)

    def softmax_pv(b, qi, j, s_ref):
        v = v_ref[0, 0, pl.ds(pl.multiple_of(j * tile, tile), tile), :]
        rs = slice(b * rb, (b + 1) * rb)
        s = s_ref[rs]
        m_old = jnp.where(j == 0, -1e30, m_ref[rs])
        m_new = jnp.maximum(m_old, jnp.max(s, axis=-1, keepdims=True))
        alpha = jnp.exp2(m_old - m_new)
        p = jnp.exp2(s - jnp.concatenate([m_new] * (tile // LANES), axis=1))
        acc = (acc_ref[rs] * jnp.concatenate([alpha] * (HEAD_LANES // LANES), axis=1)
               + _dot(p.astype(BF16), v))
        acc_ref[rs] = acc
        m_ref[rs] = m_new
        o_ref[0, pl.ds(pl.multiple_of(qi * tile + b * rb, rb), rb), :] = (
            acc[:, :V_DIM] * (1.0 / acc[:, V_DIM:])).astype(BF16)

    def step(qi, j, s_cur, s_next):
        wrap = j == qi
        qn = jnp.minimum(jnp.where(wrap, qi + 1, qi), nq - 1)
        jn = jnp.where(wrap, 0, j + 1)
        for b in range(nblk):
            scores(b, qn, jn, s_next)
        for b in range(nblk):
            softmax_pv(b, qi, j, s_cur)
        return qn, jn

    def two_steps(_, carry):
        return step(*step(*carry, s0_ref, s1_ref), s1_ref, s0_ref)

    for b in range(nblk):
        scores(b, 0, 0, s0_ref)
    n_steps = nq * (nq + 1) // 2
    n_pairs = n_steps // 2
    unroll = max(u for u in (1, 2, 3) if n_pairs % u == 0)
    carry = lax.fori_loop(0, n_pairs, two_steps, (jnp.int32(0), jnp.int32(0)), unroll=unroll)
    if n_steps % 2:
        step(*carry, s0_ref, s1_ref)


def _flat_attend(q, k, v, *, tile, name):
    bx, nh, tx, _ = q.shape
    assert tx % tile == 0 and tile % FLAT_BLOCK_ROWS == 0
    head = pl.BlockSpec((1, 1, tx, HEAD_LANES), lambda b, hh: (b, hh, 0, 0))
    return pl.pallas_call(
        functools.partial(_flat_attn_body, tile=tile, nq=tx // tile, rb=FLAT_BLOCK_ROWS),
        grid=(bx, nh),
        in_specs=[head, head, head],
        out_specs=pl.BlockSpec((1, tx, V_DIM), lambda b, hh: (b, 0, hh)),
        out_shape=jax.ShapeDtypeStruct((bx, tx, nh * V_DIM), BF16),
        scratch_shapes=[pltpu.VMEM((tile, LANES), F32), pltpu.VMEM((tile, HEAD_LANES), F32),
                        pltpu.VMEM((tile, tile), F32), pltpu.VMEM((tile, tile), F32),
                        pltpu.VMEM((tile, HEAD_LANES), BF16)],
        compiler_params=pltpu.CompilerParams(
            dimension_semantics=("arbitrary", "arbitrary"), vmem_limit_bytes=VMEM_LIMIT_BYTES),
        name=name,
    )(q, k, v)


def _out_body(o_ref, gate_ref, x_ref, wo_ref, gpost_ref, xo_ref):
    y = _dot((o_ref[0].astype(F32) * _silu(gate_ref[0].astype(F32))).astype(BF16), wo_ref[...])
    xo_ref[0] = x_ref[0] + _rms(y, gpost_ref[...])


def _odd_output(o_heads, gate, x, lyr, o, p, *, tt, name):
    bx, tx, _ = x.shape
    assert tx % tt == 0
    row = lambda b, t: (b, t, 0)
    return pl.pallas_call(
        _out_body,
        grid=(bx, tx // tt),
        in_specs=[pl.BlockSpec((1, tt, C_MIX), row), pl.BlockSpec((1, tt, C_MIX), row),
                  pl.BlockSpec((1, tt, D_MODEL), row),
                  _resident((None, C_MIX, D_MODEL), lambda b, t: (o, 0, 0)),
                  _resident((None, 1, D_MODEL), lambda b, t: (lyr, 0, 0))],
        out_specs=pl.BlockSpec((1, tt, D_MODEL), row),
        out_shape=jax.ShapeDtypeStruct((bx, tx, D_MODEL), F32),
        compiler_params=pltpu.CompilerParams(
            dimension_semantics=("arbitrary", "arbitrary"), vmem_limit_bytes=VMEM_LIMIT_BYTES),
        name=name,
    )(o_heads, gate, x, p["wo"], p["norm_post"])


def _odd_project(x, cos, sin, lyr, o, p, *, nb, tt, name):
    bx, tx, _ = x.shape
    assert bx % nb == 0 and tx % tt == 0 and tt % 16 == 0
    grid = (bx // nb, tx // tt)
    row = lambda b, t: (b, t, 0)
    hrow = lambda b, t: (b, 0, t, 0)
    in_specs = [
        pl.BlockSpec((nb, tt, D_MODEL), row),
        _resident((None, 1, D_MODEL), lambda b, t: (lyr, 0, 0)),
        _resident((None, D_MODEL, ODD_COLS), lambda b, t: (o, 0, 0)),
        _resident((None, 1, Q_LORA), lambda b, t: (o, 0, 0)),
        _resident((None, 1, KV_LORA), lambda b, t: (o, 0, 0)),
        _resident((None, Q_LORA, QUP_COLS), lambda b, t: (o, 0, 0)),
        _resident((None, N_HEADS, QK_NOPE, KV_LORA), lambda b, t: (o, 0, 0, 0)),
        pl.BlockSpec((nb * tt, LANES), lambda b, t: (t, 0)),
        pl.BlockSpec((nb * tt, LANES), lambda b, t: (t, 0)),
    ]
    out_shape = [
        jax.ShapeDtypeStruct((bx, N_HEADS, tx, KV_LORA), BF16),
        jax.ShapeDtypeStruct((bx, N_HEADS, tx, QK_ROPE), BF16),
        jax.ShapeDtypeStruct((bx, tx, KV_LORA), F32),
        jax.ShapeDtypeStruct((bx, tx, QK_ROPE), F32),
        jax.ShapeDtypeStruct((bx, tx, KV_LORA), BF16),
        jax.ShapeDtypeStruct((bx, tx, QK_ROPE), BF16),
        jax.ShapeDtypeStruct((bx, tx, C_MIX), F32),
    ]
    out_specs = [
        pl.BlockSpec((nb, N_HEADS, tt, KV_LORA), hrow),
        pl.BlockSpec((nb, N_HEADS, tt, QK_ROPE), hrow),
        pl.BlockSpec((nb, tt, KV_LORA), row),
        pl.BlockSpec((nb, tt, QK_ROPE), row),
        pl.BlockSpec((nb, tt, KV_LORA), row),
        pl.BlockSpec((nb, tt, QK_ROPE), row),
        pl.BlockSpec((nb, tt, C_MIX), row),
    ]
    return pl.pallas_call(
        functools.partial(_proj_body, nb=nb, tt=tt),
        grid=grid,
        in_specs=in_specs,
        out_specs=out_specs,
        out_shape=out_shape,
        compiler_params=pltpu.CompilerParams(
            dimension_semantics=("arbitrary", "arbitrary"), vmem_limit_bytes=VMEM_LIMIT_BYTES),
        name=name,
    )(x, p["norm_pre"], p["win_o"], p["qn"], p["kvn"], p["wq"], p["wuk"], cos, sin)


def _cached_attn_body(qa_ref, qr_ref, cckv_ref, ckrt_ref, nckv_ref, nkr_ref, o_ref, *, tq, past, ns):
    rows = N_HEADS * tq
    kpos = past + lax.broadcasted_iota(jnp.int32, (1, tq, tq), 2)
    qpos = past + lax.broadcasted_iota(jnp.int32, (1, tq, tq), 1)
    visible = kpos // CHUNK <= qpos // CHUNK
    staged = []
    for s in range(ns):
        qa = qa_ref[s].reshape(rows, KV_LORA)
        qr = qr_ref[s].reshape(rows, QK_ROPE)
        kc = cckv_ref[s].astype(BF16)
        kn = nckv_ref[s]
        s_old = _dot_nt(qa, kc) + _dot(qr, ckrt_ref[s].astype(BF16))
        s_new = _dot_nt(qa, kn) + _dot_nt(qr, nkr_ref[s])
        s_new = jnp.where(visible, s_new.reshape(N_HEADS, tq, tq), -jnp.inf).reshape(rows, tq)
        staged.append((s_old, s_new, kc, kn))
    for s, (s_old, s_new, kc, kn) in enumerate(staged):
        m = jnp.maximum(jnp.max(s_old, axis=-1, keepdims=True), jnp.max(s_new, axis=-1, keepdims=True))
        p_old = jnp.exp2(s_old - m)
        p_new = jnp.exp2(s_new - m)
        l = jnp.sum(p_old, axis=-1, keepdims=True) + jnp.sum(p_new, axis=-1, keepdims=True)
        o_lat = (_dot(p_old.astype(BF16), kc) + _dot(p_new.astype(BF16), kn)) * (1.0 / l)
        o_ref[:, s] = o_lat.astype(BF16).reshape(N_HEADS, tq, KV_LORA)


def _cached_attend(qa, qr, cache_ckv, cache_krope_t, ckvb, krb, o, *, name):
    bx, _, tq, _ = qa.shape
    past = cache_ckv.shape[2]
    assert tq % 16 == 0
    ns = max(n for n in (1, 2, 4) if bx % n == 0)
    in_specs = [
        pl.BlockSpec((ns, N_HEADS, tq, KV_LORA), lambda b: (b, 0, 0, 0)),
        pl.BlockSpec((ns, N_HEADS, tq, QK_ROPE), lambda b: (b, 0, 0, 0)),
        pl.BlockSpec((None, ns, past, KV_LORA), lambda b: (o, b, 0, 0)),
        pl.BlockSpec((None, ns, QK_ROPE, past), lambda b: (o, b, 0, 0)),
        pl.BlockSpec((ns, tq, KV_LORA), lambda b: (b, 0, 0)),
        pl.BlockSpec((ns, tq, QK_ROPE), lambda b: (b, 0, 0)),
    ]
    return pl.pallas_call(
        functools.partial(_cached_attn_body, tq=tq, past=past, ns=ns),
        grid=(bx // ns,),
        in_specs=in_specs,
        out_specs=pl.BlockSpec((N_HEADS, ns, tq, KV_LORA), lambda b: (0, b, 0, 0)),
        out_shape=jax.ShapeDtypeStruct((N_HEADS, bx, tq, KV_LORA), BF16),
        compiler_params=pltpu.CompilerParams(
            dimension_semantics=("arbitrary",), vmem_limit_bytes=VMEM_LIMIT_BYTES),
        name=name,
    )(qa, qr, cache_ckv, cache_krope_t, ckvb, krb)


def _latent_out_body(olat_ref, gate_ref, x_ref, wuv_ref, wo_ref, gpost_ref, xo_ref):
    o = jnp.concatenate([_dot(olat_ref[hh], wuv_ref[hh]) for hh in range(N_HEADS)], axis=1)
    y = _dot((o * _silu(gate_ref[...])).astype(BF16), wo_ref[...])
    xo_ref[...] = x_ref[...] + _rms(y, gpost_ref[...])


def _latent_output(o_lat, gate, x, lyr, o, p, *, name):
    m = x.shape[0]
    whole = lambda shape: pl.BlockSpec(shape, lambda i: (0,) * len(shape))
    return pl.pallas_call(
        _latent_out_body,
        grid=(1,),
        in_specs=[whole((N_HEADS, m, KV_LORA)), whole((m, C_MIX)), whole((m, D_MODEL)),
                  pl.BlockSpec((None, N_HEADS, KV_LORA, V_DIM), lambda i: (o, 0, 0, 0)),
                  pl.BlockSpec((None, C_MIX, D_MODEL), lambda i: (o, 0, 0)),
                  pl.BlockSpec((None, 1, D_MODEL), lambda i: (lyr, 0, 0))],
        out_specs=whole((m, D_MODEL)),
        out_shape=jax.ShapeDtypeStruct((m, D_MODEL), F32),
        compiler_params=pltpu.CompilerParams(
            dimension_semantics=("arbitrary",), vmem_limit_bytes=VMEM_LIMIT_BYTES),
        name=name,
    )(o_lat, gate, x, p["wuv"], p["wo"], p["norm_post"])


def _rope_tables(pos0, n, copies=1):
    half = QK_ROPE // 2
    freqs = ROPE_BASE ** (-np.arange(half, dtype=np.float64) / half)
    ang = (pos0 + np.arange(n, dtype=np.float64))[:, None] * freqs[None, :]
    c, s = np.cos(ang), np.sin(ang)
    cos64, sin64 = np.concatenate([c, c], axis=1), np.concatenate([-s, s], axis=1)
    tables = (np.tile(cos64, (copies, LANES // QK_ROPE)), np.tile(sin64, (copies, LANES // QK_ROPE)),
              np.tile(np.concatenate([cos64, sin64], axis=1), (copies, 1)))
    return tuple(jnp.asarray(tab, F32) for tab in tables)


def _half_swap(w):
    half = QK_ROPE // 2
    return jnp.concatenate([w[..., half:], w[..., :half]], axis=-1)


def _prepare_params(norm_pre, norm_post, w_in_even, w_pool, pool_scale, sgu_ln_g, sgu_ln_b, w_spatial, b_spatial,
                    w_out_even, w_in_odd, q_norm, kv_norm, w_q_up, w_kv_up, w_o):
    n_odd = w_in_odd.shape[0]
    kr_w = w_in_odd[..., Q_LORA + KV_LORA:Q_LORA + KV_LORA + QK_ROPE]
    win_o = jnp.concatenate([w_in_odd[..., :Q_LORA + KV_LORA], w_in_odd[..., Q_LORA + KV_LORA + QK_ROPE:],
                             kr_w, _half_swap(kr_w)], axis=-1)
    wq_r = w_q_up[..., QK_NOPE:]
    wq = jnp.concatenate([w_q_up[..., :QK_NOPE].reshape(n_odd, Q_LORA, N_HEADS * QK_NOPE),
                          wq_r.reshape(n_odd, Q_LORA, N_HEADS * QK_ROPE),
                          _half_swap(wq_r).reshape(n_odd, Q_LORA, N_HEADS * QK_ROPE)], axis=-1)
    return {
        "norm_pre": norm_pre[:, None, :],
        "norm_post": norm_post[:, None, :],
        "win_e": w_in_even.astype(BF16),
        "wpool": w_pool.astype(BF16),
        "pscale": pool_scale[:, None, :],
        "lng": sgu_ln_g[:, None, :],
        "lnb": sgu_ln_b[:, None, :],
        "ws": w_spatial,
        "bst": jnp.swapaxes(b_spatial, 1, 2),
        "wout_e": w_out_even.astype(BF16),
        "win_o": win_o.astype(BF16),
        "qn": q_norm[:, None, :],
        "kvn": kv_norm[:, None, :],
        "wq": wq.astype(BF16),
        "wuk": jnp.transpose(w_kv_up[..., :QK_NOPE], (0, 2, 3, 1)).astype(BF16),
        "wuv": jnp.transpose(w_kv_up[..., QK_NOPE:], (0, 2, 1, 3)).astype(BF16),
        "wukv": jnp.concatenate([w_kv_up[..., :QK_NOPE].reshape(n_odd, KV_LORA, N_HEADS * QK_NOPE),
                                 w_kv_up[..., QK_NOPE:].reshape(n_odd, KV_LORA, C_MIX)], axis=-1).astype(BF16),
        "wo": w_o.astype(BF16),
    }


def _tile_sizes(t_prompt):
    tt_even = min(t_prompt, 512)
    tt_proj = min(t_prompt, 512)
    tile = min(t_prompt, 1024)
    tt_out = min(t_prompt, 1024)
    return tt_even, tt_proj, tile, tt_out


def kernel(x_prompt, x_sample, cache_pool, cache_ckv, cache_krope, norm_pre, norm_post, w_in_even, w_pool, pool_scale, sgu_ln_g, sgu_ln_b, w_spatial, b_spatial, w_out_even, w_in_odd, q_norm, kv_norm, w_q_up, w_kv_up, w_o):
    depth = norm_pre.shape[0]
    b, t, _ = x_prompt.shape
    db, s, _ = x_sample.shape
    past = cache_ckv.shape[2]
    p = _prepare_params(norm_pre, norm_post, w_in_even, w_pool, pool_scale, sgu_ln_g, sgu_ln_b, w_spatial,
                        b_spatial, w_out_even, w_in_odd, q_norm, kv_norm, w_q_up, w_kv_up, w_o)
    tt_even, tt_proj, tile, tt_out = _tile_sizes(t)

    cos_p, sin_p, cs_p = _rope_tables(0, t)
    cos_s, sin_s, _ = _rope_tables(past, s, copies=db)
    cache_krope_t = jnp.swapaxes(cache_krope, 2, 3)

    xp, xs = x_prompt, x_sample
    pool_p, pool_s, sgu_s = [], [], []
    ckv_s, kr_s = [], []
    zero_hist = jnp.zeros((b, HIST_ROWS, W_A), F32)
    pending = None
    latent_p = None
    for lyr in range(depth):
        if lyr % 2 == 0:
            e = lyr // 2
            xp, hp = _even_layer(xp, zero_hist, lyr, e, p, nb=1, tt=tt_even, pos0=0, emit_v=False,
                                 name=f"even{e}_prompt", pending=pending)
            pending = None
            hist_s = jnp.pad(cache_pool[e], ((0, 0), (HIST_ROWS - POOL_HIST, 0), (0, 0)))
            xs, hs, vs = _even_layer(xs, hist_s, lyr, e, p, nb=db, tt=s, pos0=past, emit_v=True,
                                     name=f"even{e}_sample")
            pool_p.append(hp[:, HIST_ROWS - POOL_HIST:])
            pool_s.append(hs[:, HIST_ROWS - POOL_HIST:])
            sgu_s.append(vs)
        else:
            o = lyr // 2
            qh, kh, vh, ckv_stack, kr_stack, gate = _odd_project_heads(
                xp, cos_p, sin_p, cs_p, lyr, o, p, tt=tt_proj, tile=tile, name=f"proj{o}_prompt", prev=latent_p)
            latent_p = (ckv_stack, kr_stack)
            o_heads = _flat_attend(qh, kh, vh, tile=tile, name=f"attn{o}_prompt")
            if lyr + 1 < depth:
                pending = (o_heads, gate, o)
            else:
                xp = _odd_output(o_heads, gate, xp, lyr, o, p, tt=tt_out, name=f"out{o}_prompt")

            qa, qr, ckv, kr, ckvb, krb, gate = _odd_project(xs, cos_s, sin_s, lyr, o, p, nb=db, tt=s,
                                                            name=f"proj{o}_sample")
            o_lat = _cached_attend(qa, qr, cache_ckv, cache_krope_t, ckvb, krb, o, name=f"attn{o}_sample")
            xs = _latent_output(o_lat.reshape(N_HEADS, db * s, KV_LORA), gate.reshape(db * s, C_MIX),
                                xs.reshape(db * s, D_MODEL), lyr, o, p, name=f"out{o}_sample").reshape(db, s, D_MODEL)
            ckv_s.append(ckv)
            kr_s.append(kr)
    return (xp, xs, jnp.stack(pool_p), jnp.stack(pool_s), jnp.stack(sgu_s),
            latent_p[0], latent_p[1], jnp.stack(ckv_s), jnp.stack(kr_s))
```
